```python
import math
import jax, jax.numpy as jnp
from jax import lax
import numpy as np

D_MODEL = 2048
BATCH = 1
SEQ = 8192
DEPTH = 4

N_HEADS = 16
HEAD_DIM = D_MODEL // N_HEADS
N_MIXERS = 2
FOX_Q_BLOCK = 128
MOBA_BLOCK = 256
MOBA_TOPK = 3
MOBA_Q_CHUNK = 64
REL_BUCKETS = 32
REL_MAX_DIST = 128
D_FF = 5632
N_EXPERTS = 8
TOP_K = 2
D_FF_EXPERT = 5632
RMS_EPS = 1e-6
FORGET_BIAS_INIT = 4.0
NEG_INF = -1e30

kernel_name = "hybrid_fox_moba_moe_adaln"


def rmsnorm(x, g):
    xf = x.astype(jnp.float32)
    y = xf * lax.rsqrt(jnp.mean(xf * xf, axis=-1, keepdims=True) + RMS_EPS)
    return (y * g.astype(jnp.float32)).astype(x.dtype)


def split_heads(t):
    B, S, _ = t.shape
    return t.reshape(B, S, N_HEADS, HEAD_DIM).transpose(0, 2, 1, 3)


def rel_bucket(dist):
    n = jnp.maximum(dist, 0)
    max_exact = REL_BUCKETS // 2
    nf = jnp.maximum(n, 1).astype(jnp.float32)
    large = max_exact + (jnp.log(nf / max_exact) / math.log(REL_MAX_DIST / max_exact)
                         * (REL_BUCKETS - max_exact)).astype(jnp.int32)
    large = jnp.minimum(large, REL_BUCKETS - 1)
    return jnp.where(n < max_exact, n, large)


def fox_attention(h, w_qkv, w_f, b_f, w_o):
    B, S, _ = h.shape
    q, k, v = jnp.split(h @ w_qkv, 3, axis=-1)
    q, k, v = split_heads(q), split_heads(k), split_heads(v)
    log_f = jax.nn.log_sigmoid((h @ w_f + b_f).astype(jnp.float32))
    F = jnp.cumsum(log_f, axis=1).transpose(0, 2, 1)
    scale = HEAD_DIM ** -0.5
    key_pos = jnp.arange(S)
    n_blocks = S // FOX_Q_BLOCK

    def block(i):
        start = i * FOX_Q_BLOCK
        qb = lax.dynamic_slice_in_dim(q, start, FOX_Q_BLOCK, axis=2)
        Fq = lax.dynamic_slice_in_dim(F, start, FOX_Q_BLOCK, axis=2)
        s = (jnp.einsum('bhqd,bhkd->bhqk', qb, k).astype(jnp.float32) * scale
             + Fq[..., :, None] - F[..., None, :])
        q_pos = start + jnp.arange(FOX_Q_BLOCK)
        s = jnp.where(key_pos[None, :] <= q_pos[:, None], s, NEG_INF)
        p = jax.nn.softmax(s, axis=-1).astype(v.dtype)
        return jnp.einsum('bhqk,bhkd->bhqd', p, v)

    o = lax.map(block, jnp.arange(n_blocks))
    o = o.transpose(1, 0, 3, 2, 4).reshape(B, S, D_MODEL)
    return o @ w_o


def moba_attention(h, w_qkv, w_o, rel_bias):
    B, S, _ = h.shape
    q, k, v = jnp.split(h @ w_qkv, 3, axis=-1)
    q, k, v = split_heads(q), split_heads(k), split_heads(v)
    S_pad = -(-S // MOBA_BLOCK) * MOBA_BLOCK
    pad = ((0, 0), (0, 0), (0, S_pad - S), (0, 0))
    q, k, v = jnp.pad(q, pad), jnp.pad(k, pad), jnp.pad(v, pad)
    NB = S_pad // MOBA_BLOCK
    topk = min(MOBA_TOPK, NB)
    kb = k.reshape(B, N_HEADS, NB, MOBA_BLOCK, HEAD_DIM)
    vb = v.reshape(B, N_HEADS, NB, MOBA_BLOCK, HEAD_DIM)
    k_mean = jnp.mean(kb.astype(jnp.float32), axis=3)
    scale = HEAD_DIM ** -0.5
    b_idx = jnp.arange(B)[:, None, None, None]
    h_idx = jnp.arange(N_HEADS)[None, :, None, None]
    blk_ids = jnp.arange(NB)
    offs = jnp.arange(MOBA_BLOCK)
    C = MOBA_Q_CHUNK

    def chunk(i):
        start = i * C
        own = start // MOBA_BLOCK
        q_pos = start + jnp.arange(C)
        qc = lax.dynamic_slice_in_dim(q, start, C, axis=2)
        gate = jnp.einsum('bhqd,bhnd->bhqn', qc.astype(jnp.float32), k_mean)
        gate = jnp.where(blk_ids < own, gate, NEG_INF)
        _, sel = lax.top_k(gate, topk)
        valid = sel < own
        k_sel = kb[b_idx, h_idx, sel]
        s_sel = jnp.einsum('bhqd,bhqnld->bhqnl', qc, k_sel).astype(jnp.float32) * scale
        pos_sel = sel[..., None] * MOBA_BLOCK + offs
        bucket_sel = rel_bucket(q_pos[None, None, :, None, None] - pos_sel)
        s_sel = s_sel + rel_bias[bucket_sel, h_idx[..., None]]
        s_sel = jnp.where(valid[..., None], s_sel, NEG_INF)
        k_own = lax.dynamic_slice_in_dim(k, own * MOBA_BLOCK, MOBA_BLOCK, axis=2)
        v_own = lax.dynamic_slice_in_dim(v, own * MOBA_BLOCK, MOBA_BLOCK, axis=2)
        own_pos = own * MOBA_BLOCK + offs
        s_own = jnp.einsum('bhqd,bhld->bhql', qc, k_own).astype(jnp.float32) * scale
        bucket_own = rel_bucket(q_pos[:, None] - own_pos[None, :])
        s_own = s_own + rel_bias[bucket_own].transpose(2, 0, 1)[None]
        s_own = jnp.where(own_pos[None, :] <= q_pos[:, None], s_own, NEG_INF)
        s_all = jnp.concatenate([s_sel.reshape(B, N_HEADS, C, topk * MOBA_BLOCK), s_own], axis=-1)
        p = jax.nn.softmax(s_all, axis=-1).astype(v.dtype)
        p_sel = p[..., :topk * MOBA_BLOCK].reshape(B, N_HEADS, C, topk, MOBA_BLOCK)
        p_own = p[..., topk * MOBA_BLOCK:]
        v_sel = vb[b_idx, h_idx, sel]
        return (jnp.einsum('bhqnl,bhqnld->bhqd', p_sel, v_sel)
                + jnp.einsum('bhql,bhld->bhqd', p_own, v_own))

    o = lax.map(chunk, jnp.arange(S_pad // C))
    o = o.transpose(1, 0, 3, 2, 4).reshape(B, S_pad, D_MODEL)[:, :S]
    return o @ w_o


def swiglu(h, w1, w3, w2):
    return (jax.nn.silu(h @ w1) * (h @ w3)) @ w2


def moe_swiglu(h, w_router, w1, w3, w2):
    B, S, D = h.shape
    xt = h.reshape(B * S, D)
    logits = (xt @ w_router).astype(jnp.float32)
    top_vals, top_idx = lax.top_k(logits, TOP_K)
    top_w = jax.nn.softmax(top_vals, axis=-1)
    combine = jnp.sum(jax.nn.one_hot(top_idx, N_EXPERTS, dtype=jnp.float32)
                      * top_w[..., None], axis=1).astype(xt.dtype)
    y = jnp.zeros_like(xt)
    for e in range(N_EXPERTS):
        y = y + combine[:, e:e + 1] * swiglu(xt, w1[e], w3[e], w2[e])
    return y.reshape(B, S, D)


def setup_inputs(seed: int = 0) -> dict:
    key = jax.random.key(seed)
    ks = jax.random.split(key, 24)
    D, H = D_MODEL, N_HEADS
    n_even = (DEPTH + 1) // 2
    n_odd = DEPTH // 2

    def nrm(k, shape, s):
        return jax.random.normal(k, shape, jnp.float32) * s

    return {
        "x": nrm(ks[0], (BATCH, SEQ, D), 1.0),
        "c": nrm(ks[1], (BATCH, D), 1.0),
        "w_ada": nrm(ks[2], (DEPTH, D, 6 * D), 0.5 * D ** -0.5),
        "b_ada": nrm(ks[3], (DEPTH, 6 * D), 0.02),
        "g_mix": 1.0 + nrm(ks[4], (DEPTH, D), 0.02),
        "g_ffn": 1.0 + nrm(ks[5], (DEPTH, D), 0.02),
        "g_final": 1.0 + nrm(ks[6], (D,), 0.02),
        "rel_bias": nrm(ks[7], (REL_BUCKETS, H), 0.5),
        "w_qkv_fox": nrm(ks[8], (n_even, D, 3 * D), D ** -0.5),
        "w_f_fox": nrm(ks[9], (n_even, D, H), 0.5 * D ** -0.5),
        "b_f_fox": FORGET_BIAS_INIT + nrm(ks[10], (n_even, H), 0.1),
        "w_o_fox": nrm(ks[11], (n_even, D, D), D ** -0.5),
        "w_qkv_moba": nrm(ks[12], (n_odd, D, 3 * D), D ** -0.5),
        "w_o_moba": nrm(ks[13], (n_odd, D, D), D ** -0.5),
        "w1_dense": nrm(ks[14], (n_even, D, D_FF), D ** -0.5),
        "w3_dense": nrm(ks[15], (n_even, D, D_FF), D ** -0.5),
        "w2_dense": nrm(ks[16], (n_even, D_FF, D), D_FF ** -0.5),
        "w_router": nrm(ks[17], (n_odd, D, N_EXPERTS), D ** -0.5),
        "w1_moe": nrm(ks[18], (n_odd, N_EXPERTS, D, D_FF_EXPERT), D ** -0.5),
        "w3_moe": nrm(ks[19], (n_odd, N_EXPERTS, D, D_FF_EXPERT), D ** -0.5),
        "w2_moe": nrm(ks[20], (n_odd, N_EXPERTS, D_FF_EXPERT, D), D_FF_EXPERT ** -0.5),
    }


def reference(x, c, w_ada, b_ada, g_mix, g_ffn, g_final, rel_bias,
              w_qkv_fox, w_f_fox, b_f_fox, w_o_fox, w_qkv_moba, w_o_moba,
              w1_dense, w3_dense, w2_dense, w_router, w1_moe, w3_moe, w2_moe):
    c_act = jax.nn.silu(c)
    for i in range(DEPTH):
        j = i // 2
        mod = (c_act @ w_ada[i] + b_ada[i])[:, None, :]
        sh1, sc1, gt1, sh2, sc2, gt2 = jnp.split(mod, 6, axis=-1)
        hm = rmsnorm(x, g_mix[i]) * (1.0 + sc1) + sh1
        if i % N_MIXERS == 0:
            y = fox_attention(hm, w_qkv_fox[j], w_f_fox[j], b_f_fox[j], w_o_fox[j])
        else:
            y = moba_attention(hm, w_qkv_moba[j], w_o_moba[j], rel_bias)
        x = x + gt1 * y
        hf = rmsnorm(x, g_ffn[i]) * (1.0 + sc2) + sh2
        if i % 2 == 0:
            y = swiglu(hf, w1_dense[j], w3_dense[j], w2_dense[j])
        else:
            y = moe_swiglu(hf, w_router[j], w1_moe[j], w3_moe[j], w2_moe[j])
        x = x + gt2 * y
    return rmsnorm(x, g_final)
```

```python
import functools
import math

import numpy as np
import jax
import jax.numpy as jnp
from jax import lax
from jax.experimental import pallas as pl
from jax.experimental.pallas import tpu as pltpu

HEAD_DIM = 128
MOBA_BLOCK = 256
MOBA_TOPK = 3
REL_BUCKETS = 32
REL_MAX_DIST = 128
TOP_K = 2
RMS_EPS = 1e-6
NEG_INF = -1e30
BELOW_NEG_INF = -3e38

LANES = 128
SUBLANES = 8
VMEM_LIMIT_BYTES = 56 * 1024 * 1024

MXU_DTYPE = jnp.bfloat16
F32 = jnp.float32

ROW_TILE = 1024
FFN_F_TILE = 256
FFN_SUB_ROWS = 256
ATTN_TILE = 512
NORM_TILE = 512
COMBINE_TILE = 256


def _cparams(semantics):
    return pltpu.CompilerParams(dimension_semantics=semantics,
                                vmem_limit_bytes=VMEM_LIMIT_BYTES)


def _largest_divisor(n, candidates):
    for c in candidates:
        if n % c == 0:
            return c
    raise ValueError(f"no tile in {candidates} divides {n}")


def _ada_kernel(c_ref, w_ref, b_ref, o_ref, ca_ref):
    c = c_ref[...]
    ca_ref[...] = c * jax.nn.sigmoid(c)
    d, tn = w_ref.shape[1], w_ref.shape[2]
    ch = min(d, 256)

    def body(r, acc):
        rows = pl.ds(pl.multiple_of(r * ch, ch), ch)
        prod = w_ref[0, rows, :] * ca_ref[rows, :]
        return acc + jnp.sum(prod.reshape(ch // SUBLANES, SUBLANES, tn), axis=0)

    acc = lax.fori_loop(0, d // ch, body, jnp.zeros((SUBLANES, tn), F32))
    o_ref[0] = jnp.sum(acc, axis=0, keepdims=True) + b_ref[0]


def _ada_modulation(c, w_ada, b_ada):
    depth, d, n = w_ada.shape
    tn = _largest_divisor(n, (1024, 512, 256, 128))
    return pl.pallas_call(
        _ada_kernel,
        out_shape=jax.ShapeDtypeStruct((depth, 1, n), F32),
        grid=(depth, n // tn),
        in_specs=[pl.BlockSpec((d, 1), lambda i, j: (0, 0)),
                  pl.BlockSpec((1, d, tn), lambda i, j: (i, 0, j)),
                  pl.BlockSpec((1, 1, tn), lambda i, j: (i, 0, j))],
        out_specs=pl.BlockSpec((1, 1, tn), lambda i, j: (i, 0, j)),
        scratch_shapes=[pltpu.VMEM((d, 1), F32)],
        compiler_params=_cparams(("arbitrary", "arbitrary")),
        name="ada_modulation",
    )(c.reshape(d, 1), w_ada, b_ada.reshape(depth, 1, n))


def _norm_kernel(*refs, has_res, mode, n_experts):
    refs = list(refs)
    x_ref = refs.pop(0)
    if has_res:
        y_ref, gt_ref = refs.pop(0), refs.pop(0)
    g_ref, sc_ref, sh_ref = refs.pop(0), refs.pop(0), refs.pop(0)
    if mode == "moe":
        wr_ref = refs.pop(0)
    if mode == "fox":
        wf_ref, bf_ref = refs.pop(0), refs.pop(0)
    if has_res:
        xo_ref = refs.pop(0)
    h_ref = refs.pop(0)

    x = x_ref[...]
    if has_res:
        x = x + gt_ref[...] * y_ref[...]
        xo_ref[...] = x
    ms = jnp.mean(x * x, axis=-1, keepdims=True)
    h = (x * lax.rsqrt(ms + RMS_EPS) * g_ref[...]) * (1.0 + sc_ref[...]) + sh_ref[...]
    h_ref[...] = h.astype(h_ref.dtype)

    if mode == "moe":
        idx_ref, wts_ref = refs.pop(0), refs.pop(0)
        logits = jnp.dot(h.astype(MXU_DTYPE), wr_ref[...], preferred_element_type=F32)
        lane = lax.broadcasted_iota(jnp.int32, logits.shape, 1)
        lg = jnp.where(lane < n_experts, logits, BELOW_NEG_INF)
        v0 = jnp.max(lg, axis=-1, keepdims=True)
        i0 = jnp.min(jnp.where(lg == v0, lane, LANES), axis=-1, keepdims=True)
        lg = jnp.where(lane == i0, BELOW_NEG_INF, lg)
        v1 = jnp.max(lg, axis=-1, keepdims=True)
        i1 = jnp.min(jnp.where(lg == v1, lane, LANES), axis=-1, keepdims=True)
        e1 = jnp.exp(v1 - v0)
        den = 1.0 + e1
        idx_ref[...] = jnp.where(lane == 0, i0, jnp.where(lane == 1, i1, 0))
        wts_ref[...] = jnp.where(lane == 0, 1.0 / den, jnp.where(lane == 1, e1 / den, 0.0))

    if mode == "fox":
        f_ref, carry_ref = refs.pop(0), refs.pop(0)

        @pl.when(pl.program_id(0) == 0)
        def _():
            carry_ref[...] = jnp.zeros_like(carry_ref)

        tm = x.shape[0]
        z = lax.dot_general(wf_ref[...], h.astype(MXU_DTYPE), (((1,), (1,)), ((), ())),
                            preferred_element_type=F32) + bf_ref[...]
        log_f = jnp.minimum(z, 0.0) - jnp.log1p(jnp.exp(-jnp.abs(z)))
        upper = (lax.broadcasted_iota(jnp.int32, (tm, tm), 0)
                 <= lax.broadcasted_iota(jnp.int32, (tm, tm), 1)).astype(F32)
        cum = jnp.dot(log_f, upper, preferred_element_type=F32,
                      precision=lax.Precision.HIGHEST) + carry_ref[...]
        f_ref[0] = cum
        carry_ref[...] = cum[:, tm - 1:tm]


def _norm_modulate(x, g, sc, sh, *, mode, res=None, w_router=None, w_f=None, b_f=None):
    s, d = x.shape
    tm = min(NORM_TILE, s)
    row = pl.BlockSpec((tm, d), lambda i: (i, 0))
    vec = pl.BlockSpec((1, d), lambda i: (0, 0))
    args, in_specs, out_shape, out_specs, scratch = [x], [row], [], [], []
    if res is not None:
        y, gt = res
        args += [y, gt]
        in_specs += [row, vec]
        out_shape.append(jax.ShapeDtypeStruct((s, d), F32))
        out_specs.append(row)
    args += [g, sc, sh]
    in_specs += [vec, vec, vec]
    h_dtype = {"attn": MXU_DTYPE, "fox": MXU_DTYPE, "dense": MXU_DTYPE,
               "moe": F32, "final": F32}[mode]
    out_shape.append(jax.ShapeDtypeStruct((s, d), h_dtype))
    out_specs.append(row)
    n_experts = 0
    if mode == "moe":
        n_experts = w_router.shape[1]
        wr = jnp.zeros((d, LANES), MXU_DTYPE).at[:, :n_experts].set(w_router.astype(MXU_DTYPE))
        args.append(wr)
        in_specs.append(pl.BlockSpec((d, LANES), lambda i: (0, 0)))
        lane_blk = pl.BlockSpec((tm, LANES), lambda i: (i, 0))
        out_shape += [jax.ShapeDtypeStruct((s, LANES), jnp.int32),
                      jax.ShapeDtypeStruct((s, LANES), F32)]
        out_specs += [lane_blk, lane_blk]
    if mode == "fox":
        nh = w_f.shape[1]
        args += [w_f.T.astype(MXU_DTYPE), b_f.reshape(nh, 1)]
        in_specs += [pl.BlockSpec((nh, d), lambda i: (0, 0)),
                     pl.BlockSpec((nh, 1), lambda i: (0, 0))]
        out_shape.append(jax.ShapeDtypeStruct((s // tm, nh, tm), F32))
        out_specs.append(pl.BlockSpec((1, nh, tm), lambda i: (i, 0, 0)))
        scratch.append(pltpu.VMEM((nh, 1), F32))
    outs = pl.pallas_call(
        functools.partial(_norm_kernel, has_res=res is not None, mode=mode, n_experts=n_experts),
        out_shape=out_shape,
        grid=(s // tm,),
        in_specs=in_specs,
        out_specs=out_specs,
        scratch_shapes=scratch,
        compiler_params=_cparams(("arbitrary",)),
        name=f"norm_{mode}",
    )(*args)
    outs = list(outs)
    result = {}
    if res is not None:
        result["x"] = outs.pop(0)
    result["h"] = outs.pop(0)
    if mode == "moe":
        result["idx"], result["wts"] = outs.pop(0), outs.pop(0)
    if mode == "fox":
        result["f"] = outs.pop(0)
    return result


def _qkv_kernel(a_ref, w_ref, o_ref, wb_ref):
    @pl.when(pl.program_id(1) == 0)
    def _():
        wb_ref[...] = w_ref[0].astype(wb_ref.dtype)

    acc = jnp.dot(a_ref[...], wb_ref[...], preferred_element_type=F32)
    for j in range(o_ref.shape[0]):
        o_ref[j] = acc[:, j * HEAD_DIM:(j + 1) * HEAD_DIM].astype(o_ref.dtype)


def _qkv_projection(h, w, layer):
    s, d = h.shape
    n = w.shape[2]
    n_heads3 = n // HEAD_DIM
    tm = min(ROW_TILE, s)
    nh = _largest_divisor(n_heads3, (6, 4, 3, 2, 1))
    tn = nh * HEAD_DIM
    return pl.pallas_call(
        _qkv_kernel,
        out_shape=jax.ShapeDtypeStruct((n_heads3, s, HEAD_DIM), MXU_DTYPE),
        grid=(n // tn, s // tm),
        in_specs=[pl.BlockSpec((tm, d), lambda j, i: (i, 0)),
                  pl.BlockSpec((1, d, tn), lambda j, i: (layer, 0, j))],
        out_specs=pl.BlockSpec((nh, tm, HEAD_DIM), lambda j, i: (j, i, 0)),
        scratch_shapes=[pltpu.VMEM((d, tn), MXU_DTYPE)],
        compiler_params=_cparams(("arbitrary", "arbitrary")),
        name="qkv_projection",
    )(h, w)


def _out_proj_kernel(a_ref, w_ref, x_ref, gt_ref, o_ref, wb_ref):
    @pl.when(pl.program_id(1) == 0)
    def _():
        wb_ref[...] = w_ref[0].astype(wb_ref.dtype)

    acc = jnp.dot(a_ref[...], wb_ref[...], preferred_element_type=F32)
    o_ref[...] = x_ref[...] + gt_ref[...] * acc


def _out_projection_residual(a, w, layer, x, gt):
    s, k = a.shape
    n = w.shape[2]
    tm = min(ROW_TILE, s)
    tn = _largest_divisor(n, (512, 256, 128))
    return pl.pallas_call(
        _out_proj_kernel,
        out_shape=jax.ShapeDtypeStruct((s, n), F32),
        grid=(n // tn, s // tm),
        in_specs=[pl.BlockSpec((tm, k), lambda j, i: (i, 0)),
                  pl.BlockSpec((1, k, tn), lambda j, i: (layer, 0, j)),
                  pl.BlockSpec((tm, tn), lambda j, i: (i, j)),
                  pl.BlockSpec((1, tn), lambda j, i: (0, j))],
        out_specs=pl.BlockSpec((tm, tn), lambda j, i: (i, j)),
        scratch_shapes=[pltpu.VMEM((k, tn), MXU_DTYPE)],
        compiler_params=_cparams(("arbitrary", "arbitrary")),
        name="out_projection",
    )(a, w, x, gt)


def _fox_kernel(q_ref, k_ref, v_ref, f_ref, o_ref, m_ref, l_ref, acc_ref):
    i = pl.program_id(1)
    q = q_ref[0]
    tq = q.shape[0]
    scale = HEAD_DIM ** -0.5

    def scores(j):
        s = lax.dot_general(q, k_ref[0, j], (((1,), (1,)), ((), ())),
                            preferred_element_type=F32)
        return s * scale - f_ref[0, j]

    s = scores(i)
    causal = (lax.broadcasted_iota(jnp.int32, (tq, tq), 1)
              <= lax.broadcasted_iota(jnp.int32, (tq, tq), 0))
    s = jnp.where(causal, s, NEG_INF)
    m = jnp.max(s, axis=-1, keepdims=True)
    p = jnp.exp(s - m)
    m_ref[...] = m
    l_ref[...] = jnp.sum(p, axis=-1, keepdims=True)
    acc_ref[...] = jnp.dot(p.astype(MXU_DTYPE), v_ref[0, i], preferred_element_type=F32)

    def body(j, carry):
        s = scores(j)
        m_old = m_ref[...]
        m_new = jnp.maximum(m_old, jnp.max(s, axis=-1, keepdims=True))
        alpha = jnp.exp(m_old - m_new)
        p = jnp.exp(s - m_new)
        l_ref[...] = alpha * l_ref[...] + jnp.sum(p, axis=-1, keepdims=True)
        acc_ref[...] = alpha * acc_ref[...] + jnp.dot(p.astype(MXU_DTYPE), v_ref[0, j],
                                                      preferred_element_type=F32)
        m_ref[...] = m_new
        return carry

    lax.fori_loop(0, i, body, 0)
    o_ref[...] = (acc_ref[...] * (1.0 / l_ref[...])).astype(o_ref.dtype)


def _fox_attention(qkv, f_rows, n_heads):
    _, s, hd = qkv.shape
    t = min(ATTN_TILE, s)
    nb = s // t
    kv = qkv.reshape(3 * n_heads, nb, t, hd)
    return pl.pallas_call(
        _fox_kernel,
        out_shape=jax.ShapeDtypeStruct((s, n_heads * hd), MXU_DTYPE),
        grid=(n_heads, nb),
        in_specs=[pl.BlockSpec((1, t, hd), lambda h, i: (h, i, 0)),
                  pl.BlockSpec((1, nb, t, hd), lambda h, i: (n_heads + h, 0, 0, 0)),
                  pl.BlockSpec((1, nb, t, hd), lambda h, i: (2 * n_heads + h, 0, 0, 0)),
                  pl.BlockSpec((1, nb, 1, t), lambda h, i: (h, 0, 0, 0))],
        out_specs=pl.BlockSpec((t, hd), lambda h, i: (i, h)),
        scratch_shapes=[pltpu.VMEM((t, 1), F32), pltpu.VMEM((t, 1), F32),
                        pltpu.VMEM((t, hd), F32)],
        compiler_params=_cparams(("arbitrary", "arbitrary")),
        name="fox_attention",
    )(qkv, kv, kv, f_rows)


def _rel_bucket_np(dist):
    n = np.maximum(dist, 0)
    max_exact = REL_BUCKETS // 2
    nf = np.maximum(n, 1).astype(np.float32)
    large = max_exact + (np.log(nf / np.float32(max_exact))
                         / np.float32(math.log(REL_MAX_DIST / max_exact))
                         * np.float32(REL_BUCKETS - max_exact)).astype(np.int32)
    large = np.minimum(large, REL_BUCKETS - 1)
    return np.where(n < max_exact, n, large).astype(np.int32)


def _bias_table_kernel(rb_ref, own_idx_ref, prev_idx_ref, own_ref, prev_ref, *, n_heads):
    h = pl.program_id(0)
    own_idx = own_idx_ref[...]
    prev_idx = prev_idx_ref[...]
    own = jnp.zeros(own_idx.shape, F32)
    prev = jnp.zeros(prev_idx.shape, F32)
    for b in range(REL_BUCKETS):
        val = rb_ref[b * n_heads + h]
        own = jnp.where(own_idx == b, val, own)
        prev = jnp.where(prev_idx == b, val, prev)
    blk = own_idx.shape[0]
    causal = (lax.broadcasted_iota(jnp.int32, (blk, blk), 1)
              <= lax.broadcasted_iota(jnp.int32, (blk, blk), 0))
    own_ref[0] = jnp.where(causal, own, NEG_INF)
    prev_ref[0] = prev


def _moba_bias_tables(rel_bias):
    n_heads = rel_bias.shape[1]
    blk = MOBA_BLOCK
    qpos = np.arange(blk)[:, None]
    kpos = np.arange(blk)[None, :]
    own_idx = jnp.asarray(_rel_bucket_np(qpos - kpos))
    prev_idx = jnp.asarray(_rel_bucket_np(blk + qpos - kpos))
    tile = pl.BlockSpec((blk, blk), lambda h, rb: (0, 0))
    out = pl.BlockSpec((1, blk, blk), lambda h, rb: (h, 0, 0))
    return pl.pallas_call(
        functools.partial(_bias_table_kernel, n_heads=n_heads),
        out_shape=[jax.ShapeDtypeStruct((n_heads, blk, blk), F32)] * 2,
        grid_spec=pltpu.PrefetchScalarGridSpec(
            num_scalar_prefetch=1, grid=(n_heads,),
            in_specs=[tile, tile], out_specs=[out, out]),
        compiler_params=_cparams(("arbitrary",)),
        name="moba_bias_tables",
    )(rel_bias.reshape(-1), own_idx, prev_idx)


def _moba_kernel(q_ref, k_ref, v_ref, own_ref, prev_ref, far_ref, o_ref,
                 kmean_ref, selb_ref, m_ref, l_ref, acc_ref):
    i = pl.program_id(1)
    nb = k_ref.shape[1]
    blk = q_ref.shape[1]
    scale = HEAD_DIM ** -0.5

    @pl.when(i == 0)
    def _():
        kmean_ref[...] = jnp.zeros_like(kmean_ref)
        for n in range(nb):
            kmean_ref[n:n + 1, :] = jnp.mean(k_ref[0, n].astype(F32), axis=0, keepdims=True)

    q = q_ref[0]

    gate = lax.dot_general(q, kmean_ref[...].astype(MXU_DTYPE), (((1,), (1,)), ((), ())),
                           preferred_element_type=F32)
    lane = lax.broadcasted_iota(jnp.int32, gate.shape, 1)
    past = lane < i
    g = jnp.where(past, gate, NEG_INF)
    selb = jnp.full(gate.shape, NEG_INF, F32)
    for _ in range(MOBA_TOPK):
        mx = jnp.max(g, axis=-1, keepdims=True)
        pick = lane == jnp.min(jnp.where(g == mx, lane, LANES), axis=-1, keepdims=True)
        selb = jnp.where(pick & past, 0.0, selb)
        g = jnp.where(pick, BELOW_NEG_INF, g)
    selb_ref[...] = selb.astype(selb_ref.dtype)

    def qk(n):
        return lax.dot_general(q, k_ref[0, n], (((1,), (1,)), ((), ())),
                               preferred_element_type=F32) * scale

    def selection_bias(n):
        onehot = (lax.broadcasted_iota(jnp.int32, (LANES, LANES), 0) == n).astype(MXU_DTYPE)
        col = jnp.dot(selb_ref[...], onehot, preferred_element_type=F32)
        return jnp.concatenate([col] * (blk // LANES), axis=1)

    def accumulate(s, n):
        m_old = m_ref[...]
        m_new = jnp.maximum(m_old, jnp.max(s, axis=-1, keepdims=True))
        alpha = jnp.exp(m_old - m_new)
        p = jnp.exp(s - m_new)
        l_ref[...] = alpha * l_ref[...] + jnp.sum(p, axis=-1, keepdims=True)
        acc_ref[...] = alpha * acc_ref[...] + jnp.dot(p.astype(MXU_DTYPE), v_ref[0, n],
                                                      preferred_element_type=F32)
        m_ref[...] = m_new

    s = qk(i) + own_ref[0]
    m = jnp.max(s, axis=-1, keepdims=True)
    p = jnp.exp(s - m)
    m_ref[...] = m
    l_ref[...] = jnp.sum(p, axis=-1, keepdims=True)
    acc_ref[...] = jnp.dot(p.astype(MXU_DTYPE), v_ref[0, i], preferred_element_type=F32)

    @pl.when(i >= 1)
    def _():
        n = i - 1
        accumulate(qk(n) + prev_ref[0] + selection_bias(n), n)

    far = far_ref[0][:, 0:1]

    def body(n, carry):
        accumulate(qk(n) + far + selection_bias(n), n)
        return carry

    lax.fori_loop(0, jnp.maximum(i - 1, 0), body, 0)
    o_ref[...] = (acc_ref[...] * (1.0 / l_ref[...])).astype(o_ref.dtype)


def _moba_attention(qkv, bias_own, bias_prev, rel_bias, n_heads):
    _, s, hd = qkv.shape
    blk = MOBA_BLOCK
    assert s % blk == 0 and blk >= REL_MAX_DIST
    nb = s // blk
    assert nb <= LANES
    kv = qkv.reshape(3 * n_heads, nb, blk, hd)
    far = jnp.broadcast_to(rel_bias[REL_BUCKETS - 1][:, None, None], (n_heads, 1, LANES))
    return pl.pallas_call(
        _moba_kernel,
        out_shape=jax.ShapeDtypeStruct((s, n_heads * hd), MXU_DTYPE),
        grid=(n_heads, nb),
        in_specs=[pl.BlockSpec((1, blk, hd), lambda h, i: (h, i, 0)),
                  pl.BlockSpec((1, nb, blk, hd), lambda h, i: (n_heads + h, 0, 0, 0)),
                  pl.BlockSpec((1, nb, blk, hd), lambda h, i: (2 * n_heads + h, 0, 0, 0)),
                  pl.BlockSpec((1, blk, blk), lambda h, i: (h, 0, 0)),
                  pl.BlockSpec((1, blk, blk), lambda h, i: (h, 0, 0)),
                  pl.BlockSpec((1, 1, LANES), lambda h, i: (h, 0, 0))],
        out_specs=pl.BlockSpec((blk, hd), lambda h, i: (i, h)),
        scratch_shapes=[pltpu.VMEM((LANES, hd), F32),
                        pltpu.VMEM((blk, LANES), MXU_DTYPE),
                        pltpu.VMEM((blk, 1), F32), pltpu.VMEM((blk, 1), F32),
                        pltpu.VMEM((blk, hd), F32)],
        compiler_params=_cparams(("arbitrary", "arbitrary")),
        name="moba_attention",
    )(qkv, kv, kv, bias_own, bias_prev, far)


def _ffn_kernel(te_ref, tr_ref, x_ref, w1_ref, w3_ref, w2_ref, o_ref, w1b_ref, w3b_ref, w2b_ref):
    t = pl.program_id(0)
    f = pl.program_id(1)
    rows = tr_ref[t]
    sub = min(FFN_SUB_ROWS, x_ref.shape[0])

    @pl.when(f == 0)
    def _():
        o_ref[...] = jnp.zeros_like(o_ref)

    @pl.when(rows > 0)
    def _():
        w1b_ref[...] = w1_ref[0].astype(w1b_ref.dtype)
        w3b_ref[...] = w3_ref[0].astype(w3b_ref.dtype)
        w2b_ref[...] = w2_ref[0].astype(w2b_ref.dtype)

        def body(c, carry):
            r = pl.ds(pl.multiple_of(c * sub, sub), sub)
            x = x_ref[r, :]
            a = jnp.dot(x, w1b_ref[...], preferred_element_type=F32)
            b = jnp.dot(x, w3b_ref[...], preferred_element_type=F32)
            g = (a * jax.nn.sigmoid(a) * b).astype(MXU_DTYPE)
            o_ref[r, :] += jnp.dot(g, w2b_ref[...], preferred_element_type=F32)
            return carry

        lax.fori_loop(0, (rows + sub - 1) // sub, body, 0)


def _swiglu_ffn(xs, w1, w3, w2, tile_expert, tile_rows, tm):
    r, d = xs.shape
    ff = w1.shape[2]
    tf = _largest_divisor(ff, (FFN_F_TILE, 128))
    nf = ff // tf

    def f_eff(t, f, tr):
        return jnp.where(tr[t] > 0, f, nf - 1)

    return pl.pallas_call(
        _ffn_kernel,
        out_shape=jax.ShapeDtypeStruct((r, d), F32),
        grid_spec=pltpu.PrefetchScalarGridSpec(
            num_scalar_prefetch=2, grid=(r // tm, nf),
            in_specs=[pl.BlockSpec((tm, d), lambda t, f, te, tr: (t, 0)),
                      pl.BlockSpec((1, d, tf), lambda t, f, te, tr: (te[t], 0, f_eff(t, f, tr))),
                      pl.BlockSpec((1, d, tf), lambda t, f, te, tr: (te[t], 0, f_eff(t, f, tr))),
                      pl.BlockSpec((1, tf, d), lambda t, f, te, tr: (te[t], f_eff(t, f, tr), 0))],
            out_specs=pl.BlockSpec((tm, d), lambda t, f, te, tr: (t, 0)),
            scratch_shapes=[pltpu.VMEM((d, tf), MXU_DTYPE), pltpu.VMEM((d, tf), MXU_DTYPE),
                            pltpu.VMEM((tf, d), MXU_DTYPE)]),
        compiler_params=_cparams(("arbitrary", "arbitrary")),
        name="swiglu_ffn",
    )(tile_expert, tile_rows, xs, w1, w3, w2)


def _gather_kernel(tok_ref, nrows_ref, h_ref, o_ref, buf_ref, sem):
    t = pl.program_id(0)
    sub = o_ref.shape[0]

    def row_copy(r, tok):
        return pltpu.make_async_copy(h_ref.at[pl.ds(tok, 1)], buf_ref.at[pl.ds(r, 1)], sem)

    @pl.when(nrows_ref[t] > 0)
    def _():
        def start(r, carry):
            row_copy(r, tok_ref[t * sub + r]).start()
            return carry

        def wait(r, carry):
            row_copy(r, 0).wait()
            return carry

        lax.fori_loop(0, sub, start, 0)
        lax.fori_loop(0, sub, wait, 0)
        o_ref[...] = buf_ref[...].astype(o_ref.dtype)

    @pl.when(nrows_ref[t] == 0)
    def _():
        o_ref[...] = jnp.zeros_like(o_ref)


def _gather_rows(h, row_token, sub_rows, sub):
    s, d = h.shape
    r = row_token.shape[0]
    return pl.pallas_call(
        _gather_kernel,
        out_shape=jax.ShapeDtypeStruct((r, d), MXU_DTYPE),
        grid_spec=pltpu.PrefetchScalarGridSpec(
            num_scalar_prefetch=2, grid=(r // sub,),
            in_specs=[pl.BlockSpec(memory_space=pl.ANY)],
            out_specs=pl.BlockSpec((sub, d), lambda t, tok, nr: (t, 0)),
            scratch_shapes=[pltpu.VMEM((sub, d), F32), pltpu.SemaphoreType.DMA]),
        compiler_params=_cparams(("arbitrary",)),
        name="moe_gather",
    )(row_token, sub_rows, h)


def _combine_kernel(pos_ref, x_ref, wts_ref, gt_ref, ys_ref, o_ref, buf_ref, sem):
    t = pl.program_id(0)
    tm = x_ref.shape[0]

    def row_copy(k, r, src):
        return pltpu.make_async_copy(ys_ref.at[pl.ds(src, 1)], buf_ref.at[k, pl.ds(r, 1)], sem)

    def start(r, carry):
        for k in range(TOP_K):
            row_copy(k, r, pos_ref[(t * tm + r) * TOP_K + k]).start()
        return carry

    def wait(r, carry):
        for k in range(TOP_K):
            row_copy(k, r, 0).wait()
        return carry

    lax.fori_loop(0, tm, start, 0)
    lax.fori_loop(0, tm, wait, 0)
    wts = wts_ref[...]
    y = wts[:, 0:1] * buf_ref[0]
    for k in range(1, TOP_K):
        y = y + wts[:, k:k + 1] * buf_ref[k]
    o_ref[...] = x_ref[...] + gt_ref[...] * y


def _combine_residual(x, wts, gt, ys, pos):
    s, d = x.shape
    tm = min(COMBINE_TILE, s)
    return pl.pallas_call(
        _combine_kernel,
        out_shape=jax.ShapeDtypeStruct((s, d), F32),
        grid_spec=pltpu.PrefetchScalarGridSpec(
            num_scalar_prefetch=1, grid=(s // tm,),
            in_specs=[pl.BlockSpec((tm, d), lambda t, pos: (t, 0)),
                      pl.BlockSpec((tm, LANES), lambda t, pos: (t, 0)),
                      pl.BlockSpec((1, d), lambda t, pos: (0, 0)),
                      pl.BlockSpec(memory_space=pl.ANY)],
            out_specs=pl.BlockSpec((tm, d), lambda t, pos: (t, 0)),
            scratch_shapes=[pltpu.VMEM((TOP_K, tm, d), F32), pltpu.SemaphoreType.DMA]),
        compiler_params=_cparams(("arbitrary",)),
        name="moe_combine",
    )(pos, x, wts, gt, ys)


def _moe_plan(idx, n_experts, tm, sub):
    s = idx.shape[0]
    n_slots = s * TOP_K
    e_flat = idx.reshape(-1)
    onehot = (e_flat[:, None] == jnp.arange(n_experts, dtype=jnp.int32)[None, :]).astype(jnp.int32)
    cnt = jnp.sum(onehot, axis=0)
    rank = jnp.sum((jnp.cumsum(onehot, axis=0) - onehot) * onehot, axis=1)
    ntile_e = (cnt + tm - 1) // tm
    tile_end = jnp.cumsum(ntile_e)
    tile_start = tile_end - ntile_e
    pos = tile_start[e_flat] * tm + rank
    n_tiles = n_slots // tm + n_experts
    row_token = jnp.zeros((n_tiles * tm,), jnp.int32).at[pos].set(
        jnp.arange(n_slots, dtype=jnp.int32) // TOP_K)
    t_ids = jnp.arange(n_tiles, dtype=jnp.int32)
    te = jnp.sum((t_ids[:, None] >= tile_end[None, :]).astype(jnp.int32), axis=1)
    used = te < n_experts
    te_c = jnp.minimum(te, n_experts - 1)
    rows = jnp.where(used, jnp.clip(cnt[te_c] - (t_ids - tile_start[te_c]) * tm, 0, tm), 0)
    e_last = jnp.max(jnp.where(ntile_e > 0, jnp.arange(n_experts, dtype=jnp.int32), 0))
    tile_expert = jnp.where(used, te_c, e_last).astype(jnp.int32)
    per = tm // sub
    s_ids = jnp.arange(n_tiles * per, dtype=jnp.int32)
    sub_rows = jnp.clip(rows[s_ids // per] - (s_ids % per) * sub, 0, sub)
    return pos.astype(jnp.int32), row_token, tile_expert, rows.astype(jnp.int32), sub_rows.astype(jnp.int32)


def kernel(x, c, w_ada, b_ada, g_mix, g_ffn, g_final, rel_bias, w_qkv_fox, w_f_fox, b_f_fox, w_o_fox, w_qkv_moba, w_o_moba, w1_dense, w3_dense, w2_dense, w_router, w1_moe, w3_moe, w2_moe):
    b, s, d = x.shape
    assert b == 1 and d % HEAD_DIM == 0
    depth = w_ada.shape[0]
    n_heads = d // HEAD_DIM
    n_experts = w_router.shape[2]
    tm = min(ROW_TILE, s)
    sub = min(FFN_SUB_ROWS, tm)

    mod = _ada_modulation(c, w_ada, b_ada)
    bias_own, bias_prev = _moba_bias_tables(rel_bias)
    dense_rows = jnp.full((s // tm,), tm, jnp.int32)
    moe_w13_shape = (-1,) + w1_moe.shape[2:]
    moe_w2_shape = (-1,) + w2_moe.shape[2:]

    xc = x.reshape(s, d)
    pending = None
    for i in range(depth):
        j = i // 2
        sh1, sc1, gt1, sh2, sc2, gt2 = [mod[i, :, k * d:(k + 1) * d] for k in range(6)]
        g1 = g_mix[i].reshape(1, d)
        g2 = g_ffn[i].reshape(1, d)

        if i % 2 == 0:
            r = _norm_modulate(xc, g1, sc1, sh1, mode="fox", res=pending,
                               w_f=w_f_fox[j], b_f=b_f_fox[j])
        else:
            r = _norm_modulate(xc, g1, sc1, sh1, mode="attn", res=pending)
        xc = r.get("x", xc)
        pending = None
        if i % 2 == 0:
            qkv = _qkv_projection(r["h"], w_qkv_fox, j)
            nt = r["f"].shape[0]
            f_rows = r["f"].transpose(1, 0, 2).reshape(n_heads, nt, 1, r["f"].shape[2])
            o = _fox_attention(qkv, f_rows, n_heads)
            xc = _out_projection_residual(o, w_o_fox, j, xc, gt1)
        else:
            qkv = _qkv_projection(r["h"], w_qkv_moba, j)
            o = _moba_attention(qkv, bias_own, bias_prev, rel_bias, n_heads)
            xc = _out_projection_residual(o, w_o_moba, j, xc, gt1)

        if i % 2 == 0:
            r = _norm_modulate(xc, g2, sc2, sh2, mode="dense")
            y = _swiglu_ffn(r["h"], w1_dense, w3_dense, w2_dense,
                            jnp.full((s // tm,), j, jnp.int32), dense_rows, tm)
            pending = (y, gt2)
        else:
            r = _norm_modulate(xc, g2, sc2, sh2, mode="moe", w_router=w_router[j])
            pos, row_token, tile_expert, tile_rows, sub_rows = _moe_plan(
                r["idx"][:, :TOP_K], n_experts, tm, sub)
            xs = _gather_rows(r["h"], row_token, sub_rows, sub)
            ys = _swiglu_ffn(xs, w1_moe.reshape(moe_w13_shape), w3_moe.reshape(moe_w13_shape),
                             w2_moe.reshape(moe_w2_shape), tile_expert + j * n_experts, tile_rows, tm)
            xc = _combine_residual(xc, r["wts"], gt2, ys, pos)

    zero = jnp.zeros((1, d), F32)
    r = _norm_modulate(xc, g_final.reshape(1, d), zero, zero, mode="final", res=pending)
    return r["h"].reshape(b, s, d)
```

```python
import functools
import math

import numpy as np
import jax
import jax.numpy as jnp
from jax import lax
from jax.experimental import pallas as pl
from jax.experimental.pallas import tpu as pltpu

HEAD_DIM = 128
MOBA_BLOCK = 256
MOBA_TOPK = 3
REL_BUCKETS = 32
REL_MAX_DIST = 128
TOP_K = 2
RMS_EPS = 1e-6
NEG_INF = -1e30
BELOW_NEG_INF = -3e38
LOG2E = math.log2(math.e)
QK_SCALE_LOG2 = HEAD_DIM ** -0.5 * LOG2E

LANES = 128
SUBLANES = 8
VMEM_LIMIT_BYTES = 56 * 1024 * 1024

MXU_DTYPE = jnp.bfloat16
F32 = jnp.float32

ROW_TILE = 1024
FFN_F_TILE = 256
FFN_SUB_ROWS = 256
ATTN_TILE = 512
NORM_TILE = 512
COMBINE_TILE = 256


def _cparams(semantics):
    return pltpu.CompilerParams(dimension_semantics=semantics,
                                vmem_limit_bytes=VMEM_LIMIT_BYTES)


def _largest_divisor(n, candidates):
    for c in candidates:
        if n % c == 0:
            return c
    raise ValueError(f"no tile in {candidates} divides {n}")


def _ada_kernel(c_ref, w_ref, b_ref, o_ref, ca_ref):
    c = c_ref[...]
    ca_ref[...] = c * jax.nn.sigmoid(c)
    d, tn = w_ref.shape[1], w_ref.shape[2]
    ch = min(d, 256)

    def body(r, acc):
        rows = pl.ds(pl.multiple_of(r * ch, ch), ch)
        prod = w_ref[0, rows, :] * ca_ref[rows, :]
        return acc + jnp.sum(prod.reshape(ch // SUBLANES, SUBLANES, tn), axis=0)

    acc = lax.fori_loop(0, d // ch, body, jnp.zeros((SUBLANES, tn), F32))
    o_ref[0] = jnp.sum(acc, axis=0, keepdims=True) + b_ref[0]


def _ada_modulation(c, w_ada, b_ada):
    depth, d, n = w_ada.shape
    tn = _largest_divisor(n, (1024, 512, 256, 128))
    return pl.pallas_call(
        _ada_kernel,
        out_shape=jax.ShapeDtypeStruct((depth, 1, n), F32),
        grid=(depth, n // tn),
        in_specs=[pl.BlockSpec((d, 1), lambda i, j: (0, 0)),
                  pl.BlockSpec((1, d, tn), lambda i, j: (i, 0, j)),
                  pl.BlockSpec((1, 1, tn), lambda i, j: (i, 0, j))],
        out_specs=pl.BlockSpec((1, 1, tn), lambda i, j: (i, 0, j)),
        scratch_shapes=[pltpu.VMEM((d, 1), F32)],
        compiler_params=_cparams(("arbitrary", "arbitrary")),
        name="ada_modulation",
    )(c.reshape(d, 1), w_ada, b_ada.reshape(depth, 1, n))


def _norm_kernel(*refs, has_res, mode, n_experts):
    refs = list(refs)
    x_ref = refs.pop(0)
    if has_res:
        y_ref, gt_ref = refs.pop(0), refs.pop(0)
    g_ref, sc_ref, sh_ref = refs.pop(0), refs.pop(0), refs.pop(0)
    if mode == "moe":
        wr_ref = refs.pop(0)
    if mode == "fox":
        wf_ref, bf_ref = refs.pop(0), refs.pop(0)
    if has_res:
        xo_ref = refs.pop(0)
    h_ref = refs.pop(0)

    x = x_ref[...]
    if has_res:
        x = x + gt_ref[...] * y_ref[...]
        xo_ref[...] = x
    ms = jnp.mean(x * x, axis=-1, keepdims=True)
    h = (x * lax.rsqrt(ms + RMS_EPS) * g_ref[...]) * (1.0 + sc_ref[...]) + sh_ref[...]
    h_ref[...] = h.astype(h_ref.dtype)

    if mode == "moe":
        idx_ref, wts_ref = refs.pop(0), refs.pop(0)
        logits = jnp.dot(h.astype(MXU_DTYPE), wr_ref[...], preferred_element_type=F32)
        lane = lax.broadcasted_iota(jnp.int32, logits.shape, 1)
        lg = jnp.where(lane < n_experts, logits, BELOW_NEG_INF)
        v0 = jnp.max(lg, axis=-1, keepdims=True)
        i0 = jnp.min(jnp.where(lg == v0, lane, LANES), axis=-1, keepdims=True)
        lg = jnp.where(lane == i0, BELOW_NEG_INF, lg)
        v1 = jnp.max(lg, axis=-1, keepdims=True)
        i1 = jnp.min(jnp.where(lg == v1, lane, LANES), axis=-1, keepdims=True)
        e1 = jnp.exp(v1 - v0)
        den = 1.0 + e1
        idx_ref[...] = jnp.where(lane == 0, i0, jnp.where(lane == 1, i1, 0))
        wts_ref[...] = jnp.where(lane == 0, 1.0 / den, jnp.where(lane == 1, e1 / den, 0.0))

    if mode == "fox":
        f_ref, carry_ref = refs.pop(0), refs.pop(0)

        @pl.when(pl.program_id(0) == 0)
        def _():
            carry_ref[...] = jnp.zeros_like(carry_ref)

        tm = x.shape[0]
        z = lax.dot_general(wf_ref[...], h.astype(MXU_DTYPE), (((1,), (1,)), ((), ())),
                            preferred_element_type=F32) + bf_ref[...]
        log_f = jnp.minimum(z, 0.0) - jnp.log1p(jnp.exp(-jnp.abs(z)))
        upper = (lax.broadcasted_iota(jnp.int32, (tm, tm), 0)
                 <= lax.broadcasted_iota(jnp.int32, (tm, tm), 1)).astype(F32)
        cum = jnp.dot(log_f, upper, preferred_element_type=F32,
                      precision=lax.Precision.HIGHEST) + carry_ref[...]
        f_ref[0] = cum * LOG2E
        carry_ref[...] = cum[:, tm - 1:tm]


def _norm_modulate(x, g, sc, sh, *, mode, res=None, w_router=None, w_f=None, b_f=None):
    s, d = x.shape
    tm = min(NORM_TILE, s)
    row = pl.BlockSpec((tm, d), lambda i: (i, 0))
    vec = pl.BlockSpec((1, d), lambda i: (0, 0))
    args, in_specs, out_shape, out_specs, scratch = [x], [row], [], [], []
    if res is not None:
        y, gt = res
        args += [y, gt]
        in_specs += [row, vec]
        out_shape.append(jax.ShapeDtypeStruct((s, d), F32))
        out_specs.append(row)
    args += [g, sc, sh]
    in_specs += [vec, vec, vec]
    h_dtype = {"attn": MXU_DTYPE, "fox": MXU_DTYPE, "dense": MXU_DTYPE,
               "moe": F32, "final": F32}[mode]
    out_shape.append(jax.ShapeDtypeStruct((s, d), h_dtype))
    out_specs.append(row)
    n_experts = 0
    if mode == "moe":
        n_experts = w_router.shape[1]
        wr = jnp.zeros((d, LANES), MXU_DTYPE).at[:, :n_experts].set(w_router.astype(MXU_DTYPE))
        args.append(wr)
        in_specs.append(pl.BlockSpec((d, LANES), lambda i: (0, 0)))
        lane_blk = pl.BlockSpec((tm, LANES), lambda i: (i, 0))
        out_shape += [jax.ShapeDtypeStruct((s, LANES), jnp.int32),
                      jax.ShapeDtypeStruct((s, LANES), F32)]
        out_specs += [lane_blk, lane_blk]
    if mode == "fox":
        nh = w_f.shape[1]
        args += [w_f.T.astype(MXU_DTYPE), b_f.reshape(nh, 1)]
        in_specs += [pl.BlockSpec((nh, d), lambda i: (0, 0)),
                     pl.BlockSpec((nh, 1), lambda i: (0, 0))]
        out_shape.append(jax.ShapeDtypeStruct((s // tm, nh, tm), F32))
        out_specs.append(pl.BlockSpec((1, nh, tm), lambda i: (i, 0, 0)))
        scratch.append(pltpu.VMEM((nh, 1), F32))
    outs = pl.pallas_call(
        functools.partial(_norm_kernel, has_res=res is not None, mode=mode, n_experts=n_experts),
        out_shape=out_shape,
        grid=(s // tm,),
        in_specs=in_specs,
        out_specs=out_specs,
        scratch_shapes=scratch,
        compiler_params=_cparams(("arbitrary",)),
        name=f"norm_{mode}",
    )(*args)
    outs = list(outs)
    result = {}
    if res is not None:
        result["x"] = outs.pop(0)
    result["h"] = outs.pop(0)
    if mode == "moe":
        result["idx"], result["wts"] = outs.pop(0), outs.pop(0)
    if mode == "fox":
        result["f"] = outs.pop(0)
    return result


def _qkv_kernel(a_ref, w_ref, cs_ref, o_ref, wb_ref):
    @pl.when(pl.program_id(1) == 0)
    def _():
        wb_ref[...] = w_ref[0].astype(wb_ref.dtype)

    acc = jnp.dot(a_ref[...], wb_ref[...], preferred_element_type=F32) * cs_ref[...]
    for j in range(o_ref.shape[0]):
        o_ref[j] = acc[:, j * HEAD_DIM:(j + 1) * HEAD_DIM].astype(o_ref.dtype)


def _qkv_projection(h, w, layer):
    s, d = h.shape
    n = w.shape[2]
    n_heads3 = n // HEAD_DIM
    tm = min(ROW_TILE, s)
    nh = _largest_divisor(n_heads3, (6, 4, 3, 2, 1))
    tn = nh * HEAD_DIM
    col_scale = jnp.where(jnp.arange(n) < n // 3, QK_SCALE_LOG2, 1.0).astype(F32).reshape(1, n)
    return pl.pallas_call(
        _qkv_kernel,
        out_shape=jax.ShapeDtypeStruct((n_heads3, s, HEAD_DIM), MXU_DTYPE),
        grid=(n // tn, s // tm),
        in_specs=[pl.BlockSpec((tm, d), lambda j, i: (i, 0)),
                  pl.BlockSpec((1, d, tn), lambda j, i: (layer, 0, j)),
                  pl.BlockSpec((1, tn), lambda j, i: (0, j))],
        out_specs=pl.BlockSpec((nh, tm, HEAD_DIM), lambda j, i: (j, i, 0)),
        scratch_shapes=[pltpu.VMEM((d, tn), MXU_DTYPE)],
        compiler_params=_cparams(("arbitrary", "arbitrary")),
        name="qkv_projection",
    )(h, w, col_scale)


def _out_proj_kernel(a_ref, w_ref, x_ref, gt_ref, o_ref, wb_ref):
    @pl.when(pl.program_id(1) == 0)
    def _():
        wb_ref[...] = w_ref[0].astype(wb_ref.dtype)

    acc = jnp.dot(a_ref[...], wb_ref[...], preferred_element_type=F32)
    o_ref[...] = x_ref[...] + gt_ref[...] * acc


def _out_projection_residual(a, w, layer, x, gt):
    s, k = a.shape
    n = w.shape[2]
    tm = min(ROW_TILE, s)
    tn = _largest_divisor(n, (512, 256, 128))
    return pl.pallas_call(
        _out_proj_kernel,
        out_shape=jax.ShapeDtypeStruct((s, n), F32),
        grid=(n // tn, s // tm),
        in_specs=[pl.BlockSpec((tm, k), lambda j, i: (i, 0)),
                  pl.BlockSpec((1, k, tn), lambda j, i: (layer, 0, j)),
                  pl.BlockSpec((tm, tn), lambda j, i: (i, j)),
                  pl.BlockSpec((1, tn), lambda j, i: (0, j))],
        out_specs=pl.BlockSpec((tm, tn), lambda j, i: (i, j)),
        scratch_shapes=[pltpu.VMEM((k, tn), MXU_DTYPE)],
        compiler_params=_cparams(("arbitrary", "arbitrary")),
        name="out_projection",
    )(a, w, x, gt)


def _lane_tiles(x):
    return [x[:, c * LANES:(c + 1) * LANES] for c in range(x.shape[1] // LANES)]


def _softmax_step(s, v_aug, m_ref, acc_ref, first):
    tiles = _lane_tiles(s)
    m_cur = jnp.max(functools.reduce(jnp.maximum, tiles), axis=-1, keepdims=True)
    if first:
        m_new = jnp.broadcast_to(m_cur, m_ref.shape)
    else:
        m_old = m_ref[...]
        m_new = jnp.maximum(m_old, m_cur)
    p = jnp.concatenate([jnp.exp2(t - m_new) for t in tiles], axis=1).astype(MXU_DTYPE)
    pv = jnp.dot(p, v_aug, preferred_element_type=F32)
    if first:
        acc_ref[...] = pv
    else:
        alpha = jnp.exp2(m_old - m_new)
        acc_ref[...] = jnp.concatenate([alpha] * (acc_ref.shape[1] // LANES), axis=1) * acc_ref[...] + pv
    m_ref[...] = m_new


def _augment_values(v_ref, vaug_ref):
    hd = v_ref.shape[-1]
    for j in range(v_ref.shape[1]):
        vaug_ref[j, :, :hd] = v_ref[0, j]
        vaug_ref[j, :, hd:] = jnp.ones((v_ref.shape[2], hd), vaug_ref.dtype)


def _fox_kernel(q_ref, k_ref, v_ref, f_ref, o_ref, vaug_ref, s_ref, m_ref, acc_ref):
    i = pl.program_id(1)
    hd = q_ref.shape[2]

    @pl.when(i == 0)
    def _():
        _augment_values(v_ref, vaug_ref)

    q = q_ref[0]
    tq = q.shape[0]

    def qk(j):
        return lax.dot_general(q, k_ref[0, j], (((1,), (1,)), ((), ())),
                               preferred_element_type=F32)

    causal = (lax.broadcasted_iota(jnp.int32, (tq, tq), 1)
              <= lax.broadcasted_iota(jnp.int32, (tq, tq), 0))
    _softmax_step(jnp.where(causal, qk(i) - f_ref[0, i], NEG_INF), vaug_ref[i], m_ref, acc_ref, True)

    @pl.when(i > 0)
    def _():
        s_ref[0] = qk(0)

        def body(j, carry):
            slot = j % 2
            s = s_ref[slot] - f_ref[0, j]
            s_ref[1 - slot] = qk(jnp.minimum(j + 1, i - 1))
            _softmax_step(s, vaug_ref[j], m_ref, acc_ref, False)
            return carry

        lax.fori_loop(0, i, body, 0)

    acc = acc_ref[...]
    o_ref[...] = (acc[:, :hd] / acc[:, hd:]).astype(o_ref.dtype)


def _fox_attention(qkv, f_rows, n_heads):
    _, s, hd = qkv.shape
    t = min(ATTN_TILE, s)
    nb = s // t
    kv = qkv.reshape(3 * n_heads, nb, t, hd)
    return pl.pallas_call(
        _fox_kernel,
        out_shape=jax.ShapeDtypeStruct((s, n_heads * hd), MXU_DTYPE),
        grid=(n_heads, nb),
        in_specs=[pl.BlockSpec((1, t, hd), lambda h, i: (h, i, 0)),
                  pl.BlockSpec((1, nb, t, hd), lambda h, i: (n_heads + h, 0, 0, 0)),
                  pl.BlockSpec((1, nb, t, hd), lambda h, i: (2 * n_heads + h, 0, 0, 0)),
                  pl.BlockSpec((1, nb, 1, t), lambda h, i: (h, 0, 0, 0))],
        out_specs=pl.BlockSpec((t, hd), lambda h, i: (i, h)),
        scratch_shapes=[pltpu.VMEM((nb, t, 2 * hd), MXU_DTYPE),
                        pltpu.VMEM((2, t, t), F32),
                        pltpu.VMEM((t, LANES), F32),
                        pltpu.VMEM((t, 2 * hd), F32)],
        compiler_params=_cparams(("arbitrary", "arbitrary")),
        name="fox_attention",
    )(qkv, kv, kv, f_rows)


def _rel_bucket_np(dist):
    n = np.maximum(dist, 0)
    max_exact = REL_BUCKETS // 2
    nf = np.maximum(n, 1).astype(np.float32)
    large = max_exact + (np.log(nf / np.float32(max_exact))
                         / np.float32(math.log(REL_MAX_DIST / max_exact))
                         * np.float32(REL_BUCKETS - max_exact)).astype(np.int32)
    large = np.minimum(large, REL_BUCKETS - 1)
    return np.where(n < max_exact, n, large).astype(np.int32)


def _bias_table_kernel(rb_ref, idx_ref, o_ref, *, n_heads):
    h = pl.program_id(0)
    idx = idx_ref[0]
    bias = jnp.zeros(idx.shape, F32)
    for b in range(REL_BUCKETS):
        bias = jnp.where(idx == b, rb_ref[b * n_heads + h] * LOG2E, bias)
    o_ref[0, 0] = jnp.where(idx < 0, NEG_INF, bias)


def _moba_bias_tiles(rel_bias, t):
    n_heads = rel_bias.shape[1]
    dist = np.arange(2)[:, None, None] * t + np.arange(t)[None, :, None] - np.arange(t)[None, None, :]
    idx = jnp.asarray(np.where(dist < 0, -1, _rel_bucket_np(dist)).astype(np.int32))
    return pl.pallas_call(
        functools.partial(_bias_table_kernel, n_heads=n_heads),
        out_shape=jax.ShapeDtypeStruct((n_heads, 2, t, t), F32),
        grid_spec=pltpu.PrefetchScalarGridSpec(
            num_scalar_prefetch=1, grid=(n_heads, 2),
            in_specs=[pl.BlockSpec((1, t, t), lambda h, dt, rb: (dt, 0, 0))],
            out_specs=pl.BlockSpec((1, 1, t, t), lambda h, dt, rb: (h, dt, 0, 0))),
        compiler_params=_cparams(("arbitrary", "arbitrary")),
        name="moba_bias_tiles",
    )(rel_bias.reshape(-1), idx)


def _moba_kernel(q_ref, k_ref, v_ref, bias_ref, far_ref, o_ref,
                 vaug_ref, kmean_ref, selb_ref, s_ref, m_ref, acc_ref):
    t = pl.program_id(1)
    n_tiles, tq, hd = k_ref.shape[1], k_ref.shape[2], k_ref.shape[3]
    bpt = tq // MOBA_BLOCK
    lanes_per_blk = MOBA_BLOCK // LANES

    @pl.when(t == 0)
    def _():
        _augment_values(v_ref, vaug_ref)
        kmean_ref[...] = jnp.zeros_like(kmean_ref)
        for n in range(n_tiles * bpt):
            rows = slice((n % bpt) * MOBA_BLOCK, (n % bpt + 1) * MOBA_BLOCK)
            kmean_ref[n:n + 1, :] = jnp.mean(k_ref[0, n // bpt, rows, :].astype(F32), axis=0,
                                             keepdims=True)

    q = q_ref[0]

    gate = lax.dot_general(q, kmean_ref[...].astype(MXU_DTYPE), (((1,), (1,)), ((), ())),
                           preferred_element_type=F32)
    lane = lax.broadcasted_iota(jnp.int32, gate.shape, 1)
    own = t * bpt + lax.broadcasted_iota(jnp.int32, gate.shape, 0) // MOBA_BLOCK
    past = lane < own
    g = jnp.where(past, gate, NEG_INF)
    selb = jnp.where(lane == own, 0.0, NEG_INF)
    for _ in range(MOBA_TOPK):
        mx = jnp.max(g, axis=-1, keepdims=True)
        pick = lane == jnp.min(jnp.where(g == mx, lane, LANES), axis=-1, keepdims=True)
        selb = jnp.where(pick & past, 0.0, selb)
        g = jnp.where(pick, BELOW_NEG_INF, g)
    selb_ref[...] = selb.astype(selb_ref.dtype)

    def qk(j):
        return lax.dot_general(q, k_ref[0, j], (((1,), (1,)), ((), ())),
                               preferred_element_type=F32)

    def selection_tiles(j, extra):
        shape = (LANES, bpt * LANES)
        onehot = (lax.broadcasted_iota(jnp.int32, shape, 0)
                  == j * bpt + lax.broadcasted_iota(jnp.int32, shape, 1) // LANES).astype(MXU_DTYPE)
        cols = jnp.dot(selb_ref[...], onehot, preferred_element_type=F32)
        return [c if extra is None else c + extra for c in _lane_tiles(cols)]

    def biased(s, sel_tiles):
        return jnp.concatenate([st + sel_tiles[c // lanes_per_blk]
                                for c, st in enumerate(_lane_tiles(s))], axis=1)

    _softmax_step(biased(qk(t) + bias_ref[0, 0], selection_tiles(t, None)), vaug_ref[t], m_ref, acc_ref, True)

    @pl.when(t >= 1)
    def _():
        _softmax_step(biased(qk(t - 1) + bias_ref[0, 1], selection_tiles(t - 1, None)),
                      vaug_ref[t - 1], m_ref, acc_ref, False)

    @pl.when(t >= 2)
    def _():
        far = far_ref[0]
        s_ref[0] = qk(0)

        def body(j, carry):
            slot = j % 2
            s = biased(s_ref[slot], selection_tiles(j, far))
            s_ref[1 - slot] = qk(jnp.minimum(j + 1, t - 2))
            _softmax_step(s, vaug_ref[j], m_ref, acc_ref, False)
            return carry

        lax.fori_loop(0, t - 1, body, 0)

    acc = acc_ref[...]
    o_ref[...] = (acc[:, :hd] / acc[:, hd:]).astype(o_ref.dtype)


def _moba_attention(qkv, bias_tiles, rel_bias, n_heads):
    _, s, hd = qkv.shape
    t = bias_tiles.shape[2]
    assert s % t == 0 and t % MOBA_BLOCK == 0 and t >= REL_MAX_DIST and s // MOBA_BLOCK <= LANES
    n_tiles = s // t
    kv = qkv.reshape(3 * n_heads, n_tiles, t, hd)
    far = jnp.broadcast_to(rel_bias[REL_BUCKETS - 1][:, None, None] * LOG2E, (n_heads, 1, LANES))
    return pl.pallas_call(
        _moba_kernel,
        out_shape=jax.ShapeDtypeStruct((s, n_heads * hd), MXU_DTYPE),
        grid=(n_heads, n_tiles),
        in_specs=[pl.BlockSpec((1, t, hd), lambda h, i: (h, i, 0)),
                  pl.BlockSpec((1, n_tiles, t, hd), lambda h, i: (n_heads + h, 0, 0, 0)),
                  pl.BlockSpec((1, n_tiles, t, hd), lambda h, i: (2 * n_heads + h, 0, 0, 0)),
                  pl.BlockSpec((1, 2, t, t), lambda h, i: (h, 0, 0, 0)),
                  pl.BlockSpec((1, 1, LANES), lambda h, i: (h, 0, 0))],
        out_specs=pl.BlockSpec((t, hd), lambda h, i: (i, h)),
        scratch_shapes=[pltpu.VMEM((n_tiles, t, 2 * hd), MXU_DTYPE),
                        pltpu.VMEM((LANES, hd), F32),
                        pltpu.VMEM((t, LANES), MXU_DTYPE),
                        pltpu.VMEM((2, t, t), F32),
                        pltpu.VMEM((t, LANES), F32),
                        pltpu.VMEM((t, 2 * hd), F32)],
        compiler_params=_cparams(("arbitrary", "arbitrary")),
        name="moba_attention",
    )(qkv, kv, kv, bias_tiles, far)


def _ffn_kernel(te_ref, tr_ref, x_ref, w1_ref, w3_ref, w2_ref, o_ref, w1b_ref, w3b_ref, w2b_ref):
    t = pl.program_id(0)
    f = pl.program_id(1)
    rows = tr_ref[t]
    sub = min(FFN_SUB_ROWS, x_ref.shape[0])

    @pl.when(f == 0)
    def _():
        o_ref[...] = jnp.zeros_like(o_ref)

    @pl.when(rows > 0)
    def _():
        w1b_ref[...] = w1_ref[0].astype(w1b_ref.dtype)
        w3b_ref[...] = w3_ref[0].astype(w3b_ref.dtype)
        w2b_ref[...] = w2_ref[0].astype(w2b_ref.dtype)

        def body(c, carry):
            r = pl.ds(pl.multiple_of(c * sub, sub), sub)
            x = x_ref[r, :]
            a = jnp.dot(x, w1b_ref[...], preferred_element_type=F32)
            b = jnp.dot(x, w3b_ref[...], preferred_element_type=F32)
            g = (a * jax.nn.sigmoid(a) * b).astype(MXU_DTYPE)
            o_ref[r, :] += jnp.dot(g, w2b_ref[...], preferred_element_type=F32)
            return carry

        lax.fori_loop(0, (rows + sub - 1) // sub, body, 0)


def _swiglu_ffn(xs, w1, w3, w2, tile_expert, tile_rows, tm):
    r, d = xs.shape
    ff = w1.shape[2]
    tf = _largest_divisor(ff, (FFN_F_TILE, 128))
    nf = ff // tf

    def f_eff(t, f, tr):
        return jnp.where(tr[t] > 0, f, nf - 1)

    return pl.pallas_call(
        _ffn_kernel,
        out_shape=jax.ShapeDtypeStruct((r, d), F32),
        grid_spec=pltpu.PrefetchScalarGridSpec(
            num_scalar_prefetch=2, grid=(r // tm, nf),
            in_specs=[pl.BlockSpec((tm, d), lambda t, f, te, tr: (t, 0)),
                      pl.BlockSpec((1, d, tf), lambda t, f, te, tr: (te[t], 0, f_eff(t, f, tr))),
                      pl.BlockSpec((1, d, tf), lambda t, f, te, tr: (te[t], 0, f_eff(t, f, tr))),
                      pl.BlockSpec((1, tf, d), lambda t, f, te, tr: (te[t], f_eff(t, f, tr), 0))],
            out_specs=pl.BlockSpec((tm, d), lambda t, f, te, tr: (t, 0)),
            scratch_shapes=[pltpu.VMEM((d, tf), MXU_DTYPE), pltpu.VMEM((d, tf), MXU_DTYPE),
                            pltpu.VMEM((tf, d), MXU_DTYPE)]),
        compiler_params=_cparams(("arbitrary", "arbitrary")),
        name="swiglu_ffn",
    )(tile_expert, tile_rows, xs, w1, w3, w2)


def _gather_kernel(tok_ref, nrows_ref, h_ref, o_ref, buf_ref, sem):
    t = pl.program_id(0)
    sub = o_ref.shape[0]

    def row_copy(r, tok):
        return pltpu.make_async_copy(h_ref.at[pl.ds(tok, 1)], buf_ref.at[pl.ds(r, 1)], sem)

    @pl.when(nrows_ref[t] > 0)
    def _():
        def start(r, carry):
            row_copy(r, tok_ref[t * sub + r]).start()
            return carry

        def wait(r, carry):
            row_copy(r, 0).wait()
            return carry

        lax.fori_loop(0, sub, start, 0)
        lax.fori_loop(0, sub, wait, 0)
        o_ref[...] = buf_ref[...].astype(o_ref.dtype)

    @pl.when(nrows_ref[t] == 0)
    def _():
        o_ref[...] = jnp.zeros_like(o_ref)


def _gather_rows(h, row_token, sub_rows, sub):
    s, d = h.shape
    r = row_token.shape[0]
    return pl.pallas_call(
        _gather_kernel,
        out_shape=jax.ShapeDtypeStruct((r, d), MXU_DTYPE),
        grid_spec=pltpu.PrefetchScalarGridSpec(
            num_scalar_prefetch=2, grid=(r // sub,),
            in_specs=[pl.BlockSpec(memory_space=pl.ANY)],
            out_specs=pl.BlockSpec((sub, d), lambda t, tok, nr: (t, 0)),
            scratch_shapes=[pltpu.VMEM((sub, d), F32), pltpu.SemaphoreType.DMA]),
        compiler_params=_cparams(("arbitrary",)),
        name="moe_gather",
    )(row_token, sub_rows, h)


def _combine_kernel(pos_ref, x_ref, wts_ref, gt_ref, ys_ref, o_ref, buf_ref, sem):
    t = pl.program_id(0)
    tm = x_ref.shape[0]

    def row_copy(k, r, src):
        return pltpu.make_async_copy(ys_ref.at[pl.ds(src, 1)], buf_ref.at[k, pl.ds(r, 1)], sem)

    def start(r, carry):
        for k in range(TOP_K):
            row_copy(k, r, pos_ref[(t * tm + r) * TOP_K + k]).start()
        return carry

    def wait(r, carry):
        for k in range(TOP_K):
            row_copy(k, r, 0).wait()
        return carry

    lax.fori_loop(0, tm, start, 0)
    lax.fori_loop(0, tm, wait, 0)
    wts = wts_ref[...]
    y = wts[:, 0:1] * buf_ref[0]
    for k in range(1, TOP_K):
        y = y + wts[:, k:k + 1] * buf_ref[k]
    o_ref[...] = x_ref[...] + gt_ref[...] * y


def _combine_residual(x, wts, gt, ys, pos):
    s, d = x.shape
    tm = min(COMBINE_TILE, s)
    return pl.pallas_call(
        _combine_kernel,
        out_shape=jax.ShapeDtypeStruct((s, d), F32),
        grid_spec=pltpu.PrefetchScalarGridSpec(
            num_scalar_prefetch=1, grid=(s // tm,),
            in_specs=[pl.BlockSpec((tm, d), lambda t, pos: (t, 0)),
                      pl.BlockSpec((tm, LANES), lambda t, pos: (t, 0)),
                      pl.BlockSpec((1, d), lambda t, pos: (0, 0)),
                      pl.BlockSpec(memory_space=pl.ANY)],
            out_specs=pl.BlockSpec((tm, d), lambda t, pos: (t, 0)),
            scratch_shapes=[pltpu.VMEM((TOP_K, tm, d), F32), pltpu.SemaphoreType.DMA]),
        compiler_params=_cparams(("arbitrary",)),
        name="moe_combine",
    )(pos, x, wts, gt, ys)


def _moe_plan(idx, n_experts, tm, sub):
    s = idx.shape[0]
    n_slots = s * TOP_K
    e_flat = idx.reshape(-1)
    onehot = (e_flat[:, None] == jnp.arange(n_experts, dtype=jnp.int32)[None, :]).astype(jnp.int32)
    cnt = jnp.sum(onehot, axis=0)
    rank = jnp.sum((jnp.cumsum(onehot, axis=0) - onehot) * onehot, axis=1)
    ntile_e = (cnt + tm - 1) // tm
    tile_end = jnp.cumsum(ntile_e)
    tile_start = tile_end - ntile_e
    pos = tile_start[e_flat] * tm + rank
    n_tiles = n_slots // tm + n_experts
    row_token = jnp.zeros((n_tiles * tm,), jnp.int32).at[pos].set(
        jnp.arange(n_slots, dtype=jnp.int32) // TOP_K)
    t_ids = jnp.arange(n_tiles, dtype=jnp.int32)
    te = jnp.sum((t_ids[:, None] >= tile_end[None, :]).astype(jnp.int32), axis=1)
    used = te < n_experts
    te_c = jnp.minimum(te, n_experts - 1)
    rows = jnp.where(used, jnp.clip(cnt[te_c] - (t_ids - tile_start[te_c]) * tm, 0, tm), 0)
    e_last = jnp.max(jnp.where(ntile_e > 0, jnp.arange(n_experts, dtype=jnp.int32), 0))
    tile_expert = jnp.where(used, te_c, e_last).astype(jnp.int32)
    per = tm // sub
    s_ids = jnp.arange(n_tiles * per, dtype=jnp.int32)
    sub_rows = jnp.clip(rows[s_ids // per] - (s_ids % per) * sub, 0, sub)
    return pos.astype(jnp.int32), row_token, tile_expert, rows.astype(jnp.int32), sub_rows.astype(jnp.int32)


def kernel(x, c, w_ada, b_ada, g_mix, g_ffn, g_final, rel_bias, w_qkv_fox, w_f_fox, b_f_fox, w_o_fox, w_qkv_moba, w_o_moba, w1_dense, w3_dense, w2_dense, w_router, w1_moe, w3_moe, w2_moe):
    b, s, d = x.shape
    assert b == 1 and d % HEAD_DIM == 0
    depth = w_ada.shape[0]
    n_heads = d // HEAD_DIM
    n_experts = w_router.shape[2]
    tm = min(ROW_TILE, s)
    sub = min(FFN_SUB_ROWS, tm)

    mod = _ada_modulation(c, w_ada, b_ada)
    bias_tiles = _moba_bias_tiles(rel_bias, min(ATTN_TILE, s))
    dense_rows = jnp.full((s // tm,), tm, jnp.int32)
    moe_w13_shape = (-1,) + w1_moe.shape[2:]
    moe_w2_shape = (-1,) + w2_moe.shape[2:]

    xc = x.reshape(s, d)
    pending = None
    for i in range(depth):
        j = i // 2
        sh1, sc1, gt1, sh2, sc2, gt2 = [mod[i, :, k * d:(k + 1) * d] for k in range(6)]
        g1 = g_mix[i].reshape(1, d)
        g2 = g_ffn[i].reshape(1, d)

        if i % 2 == 0:
            r = _norm_modulate(xc, g1, sc1, sh1, mode="fox", res=pending,
                               w_f=w_f_fox[j], b_f=b_f_fox[j])
        else:
            r = _norm_modulate(xc, g1, sc1, sh1, mode="attn", res=pending)
        xc = r.get("x", xc)
        pending = None
        if i % 2 == 0:
            qkv = _qkv_projection(r["h"], w_qkv_fox, j)
            nt = r["f"].shape[0]
            f_rows = r["f"].transpose(1, 0, 2).reshape(n_heads, nt, 1, r["f"].shape[2])
            o = _fox_attention(qkv, f_rows, n_heads)
            xc = _out_projection_residual(o, w_o_fox, j, xc, gt1)
        else:
            qkv = _qkv_projection(r["h"], w_qkv_moba, j)
            o = _moba_attention(qkv, bias_tiles, rel_bias, n_heads)
            xc = _out_projection_residual(o, w_o_moba, j, xc, gt1)

        if i % 2 == 0:
            r = _norm_modulate(xc, g2, sc2, sh2, mode="dense")
            y = _swiglu_ffn(r["h"], w1_dense, w3_dense, w2_dense,
                            jnp.full((s // tm,), j, jnp.int32), dense_rows, tm)
            pending = (y, gt2)
        else:
            r = _norm_modulate(xc, g2, sc2, sh2, mode="moe", w_router=w_router[j])
            pos, row_token, tile_expert, tile_rows, sub_rows = _moe_plan(
                r["idx"][:, :TOP_K], n_experts, tm, sub)
            xs = _gather_rows(r["h"], row_token, sub_rows, sub)
            ys = _swiglu_ffn(xs, w1_moe.reshape(moe_w13_shape), w3_moe.reshape(moe_w13_shape),
                             w2_moe.reshape(moe_w2_shape), tile_expert + j * n_experts, tile_rows, tm)
            xc = _combine_residual(xc, r["wts"], gt2, ys, pos)

    zero = jnp.zeros((1, d), F32)
    r = _norm_modulate(xc, g_final.reshape(1, d), zero, zero, mode="final", res=pending)
    return r["h"].reshape(b, s, d)
```

```python
import functools
import math

import numpy as np
import jax
import jax.numpy as jnp
from jax import lax
from jax.experimental import pallas as pl
from jax.experimental.pallas import tpu as pltpu

HEAD_DIM = 128
MOBA_BLOCK = 256
MOBA_TOPK = 3
REL_BUCKETS = 32
REL_MAX_DIST = 128
TOP_K = 2
RMS_EPS = 1e-6
NEG_INF = -1e30
BELOW_NEG_INF = -3e38
LOG2E = math.log2(math.e)
QK_SCALE_LOG2 = HEAD_DIM ** -0.5 * LOG2E

LANES = 128
SUBLANES = 8
VMEM_LIMIT_BYTES = 56 * 1024 * 1024

MXU_DTYPE = jnp.bfloat16
F32 = jnp.float32

ROW_TILE = 1024
FFN_F_TILE = 256
FFN_SUB_ROWS = 256
ATTN_TILE = 512
NORM_TILE = 512
COMBINE_TILE = 256


def _cparams(semantics):
    return pltpu.CompilerParams(dimension_semantics=semantics,
                                vmem_limit_bytes=VMEM_LIMIT_BYTES)


def _largest_divisor(n, candidates):
    for c in candidates:
        if n % c == 0:
            return c
    raise ValueError(f"no tile in {candidates} divides {n}")


def _ada_kernel(c_ref, w_ref, b_ref, o_ref, ca_ref):
    c = c_ref[...]
    ca_ref[...] = c * jax.nn.sigmoid(c)
    d, tn = w_ref.shape[1], w_ref.shape[2]
    ch = min(d, 256)

    def body(r, acc):
        rows = pl.ds(pl.multiple_of(r * ch, ch), ch)
        prod = w_ref[0, rows, :] * ca_ref[rows, :]
        return acc + jnp.sum(prod.reshape(ch // SUBLANES, SUBLANES, tn), axis=0)

    acc = lax.fori_loop(0, d // ch, body, jnp.zeros((SUBLANES, tn), F32))
    o_ref[0] = jnp.sum(acc, axis=0, keepdims=True) + b_ref[0]


def _ada_modulation(c, w_ada, b_ada):
    depth, d, n = w_ada.shape
    tn = _largest_divisor(n, (1024, 512, 256, 128))
    return pl.pallas_call(
        _ada_kernel,
        out_shape=jax.ShapeDtypeStruct((depth, 1, n), F32),
        grid=(depth, n // tn),
        in_specs=[pl.BlockSpec((d, 1), lambda i, j: (0, 0)),
                  pl.BlockSpec((1, d, tn), lambda i, j: (i, 0, j)),
                  pl.BlockSpec((1, 1, tn), lambda i, j: (i, 0, j))],
        out_specs=pl.BlockSpec((1, 1, tn), lambda i, j: (i, 0, j)),
        scratch_shapes=[pltpu.VMEM((d, 1), F32)],
        compiler_params=_cparams(("arbitrary", "arbitrary")),
        name="ada_modulation",
    )(c.reshape(d, 1), w_ada, b_ada.reshape(depth, 1, n))


def _norm_kernel(*refs, has_res, mode, n_experts):
    refs = list(refs)
    x_ref = refs.pop(0)
    if has_res:
        y_ref, gt_ref = refs.pop(0), refs.pop(0)
    g_ref, sc_ref, sh_ref = refs.pop(0), refs.pop(0), refs.pop(0)
    if mode == "moe":
        wr_ref = refs.pop(0)
    if mode == "fox":
        wf_ref, bf_ref = refs.pop(0), refs.pop(0)
    if has_res:
        xo_ref = refs.pop(0)
    h_ref = refs.pop(0)

    x = x_ref[...]
    if has_res:
        x = x + gt_ref[...] * y_ref[...]
        xo_ref[...] = x
    ms = jnp.mean(x * x, axis=-1, keepdims=True)
    h = (x * lax.rsqrt(ms + RMS_EPS) * g_ref[...]) * (1.0 + sc_ref[...]) + sh_ref[...]
    h_ref[...] = h.astype(h_ref.dtype)

    if mode == "moe":
        idx_ref, wts_ref = refs.pop(0), refs.pop(0)
        logits = jnp.dot(h.astype(MXU_DTYPE), wr_ref[...], preferred_element_type=F32)
        lane = lax.broadcasted_iota(jnp.int32, logits.shape, 1)
        lg = jnp.where(lane < n_experts, logits, BELOW_NEG_INF)
        v0 = jnp.max(lg, axis=-1, keepdims=True)
        i0 = jnp.min(jnp.where(lg == v0, lane, LANES), axis=-1, keepdims=True)
        lg = jnp.where(lane == i0, BELOW_NEG_INF, lg)
        v1 = jnp.max(lg, axis=-1, keepdims=True)
        i1 = jnp.min(jnp.where(lg == v1, lane, LANES), axis=-1, keepdims=True)
        e1 = jnp.exp(v1 - v0)
        den = 1.0 + e1
        idx_ref[...] = jnp.where(lane == 0, i0, jnp.where(lane == 1, i1, 0))
        wts_ref[...] = jnp.where(lane == 0, 1.0 / den, jnp.where(lane == 1, e1 / den, 0.0))

    if mode == "fox":
        f_ref, carry_ref = refs.pop(0), refs.pop(0)

        @pl.when(pl.program_id(0) == 0)
        def _():
            carry_ref[...] = jnp.zeros_like(carry_ref)

        tm = x.shape[0]
        z = lax.dot_general(wf_ref[...], h.astype(MXU_DTYPE), (((1,), (1,)), ((), ())),
                            preferred_element_type=F32) + bf_ref[...]
        log_f = jnp.minimum(z, 0.0) - jnp.log1p(jnp.exp(-jnp.abs(z)))
        upper = (lax.broadcasted_iota(jnp.int32, (tm, tm), 0)
                 <= lax.broadcasted_iota(jnp.int32, (tm, tm), 1)).astype(F32)
        cum = jnp.dot(log_f, upper, preferred_element_type=F32,
                      precision=lax.Precision.HIGHEST) + carry_ref[...]
        f_ref[0] = cum * LOG2E
        carry_ref[...] = cum[:, tm - 1:tm]


def _norm_modulate(x, g, sc, sh, *, mode, res=None, w_router=None, w_f=None, b_f=None):
    s, d = x.shape
    tm = min(NORM_TILE, s)
    row = pl.BlockSpec((tm, d), lambda i: (i, 0))
    vec = pl.BlockSpec((1, d), lambda i: (0, 0))
    args, in_specs, out_shape, out_specs, scratch = [x], [row], [], [], []
    if res is not None:
        y, gt = res
        args += [y, gt]
        in_specs += [row, vec]
        out_shape.append(jax.ShapeDtypeStruct((s, d), F32))
        out_specs.append(row)
    args += [g, sc, sh]
    in_specs += [vec, vec, vec]
    h_dtype = {"attn": MXU_DTYPE, "fox": MXU_DTYPE, "dense": MXU_DTYPE,
               "moe": F32, "final": F32}[mode]
    out_shape.append(jax.ShapeDtypeStruct((s, d), h_dtype))
    out_specs.append(row)
    n_experts = 0
    if mode == "moe":
        n_experts = w_router.shape[1]
        wr = jnp.zeros((d, LANES), MXU_DTYPE).at[:, :n_experts].set(w_router.astype(MXU_DTYPE))
        args.append(wr)
        in_specs.append(pl.BlockSpec((d, LANES), lambda i: (0, 0)))
        lane_blk = pl.BlockSpec((tm, LANES), lambda i: (i, 0))
        out_shape += [jax.ShapeDtypeStruct((s, LANES), jnp.int32),
                      jax.ShapeDtypeStruct((s, LANES), F32)]
        out_specs += [lane_blk, lane_blk]
    if mode == "fox":
        nh = w_f.shape[1]
        args += [w_f.T.astype(MXU_DTYPE), b_f.reshape(nh, 1)]
        in_specs += [pl.BlockSpec((nh, d), lambda i: (0, 0)),
                     pl.BlockSpec((nh, 1), lambda i: (0, 0))]
        out_shape.append(jax.ShapeDtypeStruct((s // tm, nh, tm), F32))
        out_specs.append(pl.BlockSpec((1, nh, tm), lambda i: (i, 0, 0)))
        scratch.append(pltpu.VMEM((nh, 1), F32))
    outs = pl.pallas_call(
        functools.partial(_norm_kernel, has_res=res is not None, mode=mode, n_experts=n_experts),
        out_shape=out_shape,
        grid=(s // tm,),
        in_specs=in_specs,
        out_specs=out_specs,
        scratch_shapes=scratch,
        compiler_params=_cparams(("arbitrary",)),
        name=f"norm_{mode}",
    )(*args)
    outs = list(outs)
    result = {}
    if res is not None:
        result["x"] = outs.pop(0)
    result["h"] = outs.pop(0)
    if mode == "moe":
        result["idx"], result["wts"] = outs.pop(0), outs.pop(0)
    if mode == "fox":
        result["f"] = outs.pop(0)
    return result


def _qkv_kernel(a_ref, w_ref, cs_ref, o_ref, wb_ref):
    @pl.when(pl.program_id(1) == 0)
    def _():
        wb_ref[...] = w_ref[0].astype(wb_ref.dtype)

    acc = jnp.dot(a_ref[...], wb_ref[...], preferred_element_type=F32) * cs_ref[...]
    for j in range(o_ref.shape[0]):
        o_ref[j] = acc[:, j * HEAD_DIM:(j + 1) * HEAD_DIM].astype(o_ref.dtype)


def _qkv_projection(h, w, layer):
    s, d = h.shape
    n = w.shape[2]
    n_heads3 = n // HEAD_DIM
    tm = min(ROW_TILE, s)
    nh = _largest_divisor(n_heads3, (6, 4, 3, 2, 1))
    tn = nh * HEAD_DIM
    col_scale = jnp.where(jnp.arange(n) < n // 3, QK_SCALE_LOG2, 1.0).astype(F32).reshape(1, n)
    return pl.pallas_call(
        _qkv_kernel,
        out_shape=jax.ShapeDtypeStruct((n_heads3, s, HEAD_DIM), MXU_DTYPE),
        grid=(n // tn, s // tm),
        in_specs=[pl.BlockSpec((tm, d), lambda j, i: (i, 0)),
                  pl.BlockSpec((1, d, tn), lambda j, i: (layer, 0, j)),
                  pl.BlockSpec((1, tn), lambda j, i: (0, j))],
        out_specs=pl.BlockSpec((nh, tm, HEAD_DIM), lambda j, i: (j, i, 0)),
        scratch_shapes=[pltpu.VMEM((d, tn), MXU_DTYPE)],
        compiler_params=_cparams(("arbitrary", "arbitrary")),
        name="qkv_projection",
    )(h, w, col_scale)


def _out_proj_kernel(a_ref, w_ref, x_ref, gt_ref, o_ref, wb_ref):
    @pl.when(pl.program_id(1) == 0)
    def _():
        wb_ref[...] = w_ref[0].astype(wb_ref.dtype)

    acc = jnp.dot(a_ref[...], wb_ref[...], preferred_element_type=F32)
    o_ref[...] = x_ref[...] + gt_ref[...] * acc


def _out_projection_residual(a, w, layer, x, gt):
    s, k = a.shape
    n = w.shape[2]
    tm = min(ROW_TILE, s)
    tn = _largest_divisor(n, (512, 256, 128))
    return pl.pallas_call(
        _out_proj_kernel,
        out_shape=jax.ShapeDtypeStruct((s, n), F32),
        grid=(n // tn, s // tm),
        in_specs=[pl.BlockSpec((tm, k), lambda j, i: (i, 0)),
                  pl.BlockSpec((1, k, tn), lambda j, i: (layer, 0, j)),
                  pl.BlockSpec((tm, tn), lambda j, i: (i, j)),
                  pl.BlockSpec((1, tn), lambda j, i: (0, j))],
        out_specs=pl.BlockSpec((tm, tn), lambda j, i: (i, j)),
        scratch_shapes=[pltpu.VMEM((k, tn), MXU_DTYPE)],
        compiler_params=_cparams(("arbitrary", "arbitrary")),
        name="out_projection",
    )(a, w, x, gt)


def _lane_tiles(x):
    return [x[:, c * LANES:(c + 1) * LANES] for c in range(x.shape[1] // LANES)]


def _softmax_step(s, v_aug, m_ref, acc_ref, first):
    tiles = _lane_tiles(s)
    m_cur = jnp.max(functools.reduce(jnp.maximum, tiles), axis=-1, keepdims=True)
    if first:
        m_new = jnp.broadcast_to(m_cur, m_ref.shape)
    else:
        m_old = m_ref[...]
        m_new = jnp.maximum(m_old, m_cur)
    p = jnp.concatenate([jnp.exp2(t - m_new) for t in tiles], axis=1).astype(MXU_DTYPE)
    pv = jnp.dot(p, v_aug, preferred_element_type=F32)
    if first:
        acc_ref[...] = pv
    else:
        alpha = jnp.exp2(m_old - m_new)
        acc_ref[...] = jnp.concatenate([alpha] * (acc_ref.shape[1] // LANES), axis=1) * acc_ref[...] + pv
    m_ref[...] = m_new


def _augment_values(v_ref, vaug_ref):
    hd = v_ref.shape[-1]
    for j in range(v_ref.shape[1]):
        vaug_ref[j, :, :hd] = v_ref[0, j]
        vaug_ref[j, :, hd:] = jnp.ones((v_ref.shape[2], hd), vaug_ref.dtype)


def _fox_kernel(q_ref, k_ref, v_ref, f_ref, o_ref, vaug_ref, s_ref, m_ref, acc_ref):
    i = pl.program_id(1)
    hd = q_ref.shape[2]

    @pl.when(i == 0)
    def _():
        _augment_values(v_ref, vaug_ref)

    q = q_ref[0]
    tq = q.shape[0]

    def qk(j):
        return lax.dot_general(q, k_ref[0, j], (((1,), (1,)), ((), ())),
                               preferred_element_type=F32)

    causal = (lax.broadcasted_iota(jnp.int32, (tq, tq), 1)
              <= lax.broadcasted_iota(jnp.int32, (tq, tq), 0))
    _softmax_step(jnp.where(causal, qk(i) - f_ref[0, i], NEG_INF), vaug_ref[i], m_ref, acc_ref, True)

    @pl.when(i > 0)
    def _():
        s_ref[0] = qk(0)

        def body(j, carry):
            slot = j % 2
            s = s_ref[slot] - f_ref[0, j]
            s_ref[1 - slot] = qk(jnp.minimum(j + 1, i - 1))
            _softmax_step(s, vaug_ref[j], m_ref, acc_ref, False)
            return carry

        lax.fori_loop(0, i, body, 0)

    acc = acc_ref[...]
    o_ref[...] = (acc[:, :hd] / acc[:, hd:]).astype(o_ref.dtype)


def _fox_attention(qkv, f_rows, n_heads):
    _, s, hd = qkv.shape
    t = min(ATTN_TILE, s)
    nb = s // t
    kv = qkv.reshape(3 * n_heads, nb, t, hd)
    return pl.pallas_call(
        _fox_kernel,
        out_shape=jax.ShapeDtypeStruct((s, n_heads * hd), MXU_DTYPE),
        grid=(n_heads, nb),
        in_specs=[pl.BlockSpec((1, t, hd), lambda h, i: (h, i, 0)),
                  pl.BlockSpec((1, nb, t, hd), lambda h, i: (n_heads + h, 0, 0, 0)),
                  pl.BlockSpec((1, nb, t, hd), lambda h, i: (2 * n_heads + h, 0, 0, 0)),
                  pl.BlockSpec((1, nb, 1, t), lambda h, i: (h, 0, 0, 0))],
        out_specs=pl.BlockSpec((t, hd), lambda h, i: (i, h)),
        scratch_shapes=[pltpu.VMEM((nb, t, 2 * hd), MXU_DTYPE),
                        pltpu.VMEM((2, t, t), F32),
                        pltpu.VMEM((t, LANES), F32),
                        pltpu.VMEM((t, 2 * hd), F32)],
        compiler_params=_cparams(("arbitrary", "arbitrary")),
        name="fox_attention",
    )(qkv, kv, kv, f_rows)


def _rel_bucket_np(dist):
    n = np.maximum(dist, 0)
    max_exact = REL_BUCKETS // 2
    nf = np.maximum(n, 1).astype(np.float32)
    large = max_exact + (np.log(nf / np.float32(max_exact))
                         / np.float32(math.log(REL_MAX_DIST / max_exact))
                         * np.float32(REL_BUCKETS - max_exact)).astype(np.int32)
    large = np.minimum(large, REL_BUCKETS - 1)
    return np.where(n < max_exact, n, large).astype(np.int32)


def _bias_table_kernel(rb_ref, idx_ref, o_ref, *, n_heads):
    h = pl.program_id(0)
    idx = idx_ref[0]
    bias = jnp.zeros(idx.shape, F32)
    for b in range(REL_BUCKETS):
        bias = jnp.where(idx == b, rb_ref[b * n_heads + h] * LOG2E, bias)
    o_ref[0, 0] = jnp.where(idx < 0, NEG_INF, bias)


def _moba_bias_tiles(rel_bias, t):
    n_heads = rel_bias.shape[1]
    dist = np.arange(2)[:, None, None] * t + np.arange(t)[None, :, None] - np.arange(t)[None, None, :]
    idx = jnp.asarray(np.where(dist < 0, -1, _rel_bucket_np(dist)).astype(np.int32))
    return pl.pallas_call(
        functools.partial(_bias_table_kernel, n_heads=n_heads),
        out_shape=jax.ShapeDtypeStruct((n_heads, 2, t, t), F32),
        grid_spec=pltpu.PrefetchScalarGridSpec(
            num_scalar_prefetch=1, grid=(n_heads, 2),
            in_specs=[pl.BlockSpec((1, t, t), lambda h, dt, rb: (dt, 0, 0))],
            out_specs=pl.BlockSpec((1, 1, t, t), lambda h, dt, rb: (h, dt, 0, 0))),
        compiler_params=_cparams(("arbitrary", "arbitrary")),
        name="moba_bias_tiles",
    )(rel_bias.reshape(-1), idx)


def _moba_kernel(q_ref, k_ref, v_ref, bias_ref, far_ref, o_ref,
                 vaug_ref, kmean_ref, selb_ref, s_ref, m_ref, acc_ref):
    t = pl.program_id(1)
    n_tiles, tq, hd = k_ref.shape[1], k_ref.shape[2], k_ref.shape[3]
    bpt = tq // MOBA_BLOCK
    lanes_per_blk = MOBA_BLOCK // LANES

    @pl.when(t == 0)
    def _():
        _augment_values(v_ref, vaug_ref)
        kmean_ref[...] = jnp.zeros_like(kmean_ref)
        for n in range(n_tiles * bpt):
            rows = slice((n % bpt) * MOBA_BLOCK, (n % bpt + 1) * MOBA_BLOCK)
            kmean_ref[n:n + 1, :] = jnp.mean(k_ref[0, n // bpt, rows, :].astype(F32), axis=0,
                                             keepdims=True)

    q = q_ref[0]

    gate = lax.dot_general(q, kmean_ref[...].astype(MXU_DTYPE), (((1,), (1,)), ((), ())),
                           preferred_element_type=F32)
    lane = lax.broadcasted_iota(jnp.int32, gate.shape, 1)
    own = t * bpt + lax.broadcasted_iota(jnp.int32, gate.shape, 0) // MOBA_BLOCK
    past = lane < own
    g = jnp.where(past, gate, NEG_INF)
    selb = jnp.where(lane == own, 0.0, NEG_INF)
    for _ in range(MOBA_TOPK):
        mx = jnp.max(g, axis=-1, keepdims=True)
        pick = lane == jnp.min(jnp.where(g == mx, lane, LANES), axis=-1, keepdims=True)
        selb = jnp.where(pick & past, 0.0, selb)
        g = jnp.where(pick, BELOW_NEG_INF, g)
    selb_ref[...] = selb.astype(selb_ref.dtype)

    def qk(j):
        return lax.dot_general(q, k_ref[0, j], (((1,), (1,)), ((), ())),
                               preferred_element_type=F32)

    def selection_tiles(j, extra):
        shape = (LANES, bpt * LANES)
        onehot = (lax.broadcasted_iota(jnp.int32, shape, 0)
                  == j * bpt + lax.broadcasted_iota(jnp.int32, shape, 1) // LANES).astype(MXU_DTYPE)
        cols = jnp.dot(selb_ref[...], onehot, preferred_element_type=F32)
        return [c if extra is None else c + extra for c in _lane_tiles(cols)]

    def biased(s, sel_tiles):
        return jnp.concatenate([st + sel_tiles[c // lanes_per_blk]
                                for c, st in enumerate(_lane_tiles(s))], axis=1)

    _softmax_step(biased(qk(t) + bias_ref[0, 0], selection_tiles(t, None)), vaug_ref[t], m_ref, acc_ref, True)

    @pl.when(t >= 1)
    def _():
        _softmax_step(biased(qk(t - 1) + bias_ref[0, 1], selection_tiles(t - 1, None)),
                      vaug_ref[t - 1], m_ref, acc_ref, False)

    @pl.when(t >= 2)
    def _():
        far = far_ref[0]
        s_ref[0] = qk(0)

        def body(j, carry):
            slot = j % 2
            s = biased(s_ref[slot], selection_tiles(j, far))
            s_ref[1 - slot] = qk(jnp.minimum(j + 1, t - 2))
            _softmax_step(s, vaug_ref[j], m_ref, acc_ref, False)
            return carry

        lax.fori_loop(0, t - 1, body, 0)

    acc = acc_ref[...]
    o_ref[...] = (acc[:, :hd] / acc[:, hd:]).astype(o_ref.dtype)


def _moba_attention(qkv, bias_tiles, rel_bias, n_heads):
    _, s, hd = qkv.shape
    t = bias_tiles.shape[2]
    assert s % t == 0 and t % MOBA_BLOCK == 0 and t >= REL_MAX_DIST and s // MOBA_BLOCK <= LANES
    n_tiles = s // t
    kv = qkv.reshape(3 * n_heads, n_tiles, t, hd)
    far = jnp.broadcast_to(rel_bias[REL_BUCKETS - 1][:, None, None] * LOG2E, (n_heads, 1, LANES))
    return pl.pallas_call(
        _moba_kernel,
        out_shape=jax.ShapeDtypeStruct((s, n_heads * hd), MXU_DTYPE),
        grid=(n_heads, n_tiles),
        in_specs=[pl.BlockSpec((1, t, hd), lambda h, i: (h, i, 0)),
                  pl.BlockSpec((1, n_tiles, t, hd), lambda h, i: (n_heads + h, 0, 0, 0)),
                  pl.BlockSpec((1, n_tiles, t, hd), lambda h, i: (2 * n_heads + h, 0, 0, 0)),
                  pl.BlockSpec((1, 2, t, t), lambda h, i: (h, 0, 0, 0)),
                  pl.BlockSpec((1, 1, LANES), lambda h, i: (h, 0, 0))],
        out_specs=pl.BlockSpec((t, hd), lambda h, i: (i, h)),
        scratch_shapes=[pltpu.VMEM((n_tiles, t, 2 * hd), MXU_DTYPE),
                        pltpu.VMEM((LANES, hd), F32),
                        pltpu.VMEM((t, LANES), MXU_DTYPE),
                        pltpu.VMEM((2, t, t), F32),
                        pltpu.VMEM((t, LANES), F32),
                        pltpu.VMEM((t, 2 * hd), F32)],
        compiler_params=_cparams(("arbitrary", "arbitrary")),
        name="moba_attention",
    )(qkv, kv, kv, bias_tiles, far)


def _ffn_kernel(te_ref, tr_ref, x_ref, w1_ref, w3_ref, w2_ref, o_ref, w1b_ref, w3b_ref, w2b_ref, g_ref):
    t = pl.program_id(0)
    f = pl.program_id(1)
    rows = tr_ref[t]
    tm = x_ref.shape[0]
    sub = min(FFN_SUB_ROWS, tm)

    def gate_up(r):
        x = x_ref[r, :]
        a = jnp.dot(x, w1b_ref[...], preferred_element_type=F32)
        b = jnp.dot(x, w3b_ref[...], preferred_element_type=F32)
        return (a * jax.nn.sigmoid(a) * b).astype(MXU_DTYPE)

    def down(g, r):
        o_ref[r, :] += jnp.dot(g, w2b_ref[...], preferred_element_type=F32)

    def cast_gate_up_weights():
        w1b_ref[...] = w1_ref[0].astype(w1b_ref.dtype)
        w3b_ref[...] = w3_ref[0].astype(w3b_ref.dtype)

    def cast_down_weights():
        w2b_ref[...] = w2_ref[0].astype(w2b_ref.dtype)

    @pl.when(f == 0)
    def _():
        o_ref[...] = jnp.zeros_like(o_ref)

    @pl.when(rows == tm)
    def _():
        cast_gate_up_weights()
        cast_down_weights()
        for c in range(tm // sub):
            r = pl.ds(c * sub, sub)
            down(gate_up(r), r)

    @pl.when((rows > 0) & (rows < tm))
    def _():
        def chunk(c):
            return pl.ds(pl.multiple_of(c * sub, sub), sub)

        n_chunks = (rows + sub - 1) // sub
        cast_gate_up_weights()
        g_ref[0] = gate_up(chunk(0))
        cast_down_weights()

        def body(c, carry):
            down(g_ref[(c - 1) % 2], chunk(c - 1))
            g_ref[c % 2] = gate_up(chunk(c))
            return carry

        lax.fori_loop(1, n_chunks, body, 0)
        down(g_ref[(n_chunks - 1) % 2], chunk(n_chunks - 1))


def _swiglu_ffn(xs, w1, w3, w2, tile_expert, tile_rows, tm):
    r, d = xs.shape
    ff = w1.shape[2]
    tf = _largest_divisor(ff, (FFN_F_TILE, 128))
    nf = ff // tf

    def f_eff(t, f, tr):
        return jnp.where(tr[t] > 0, f, nf - 1)

    return pl.pallas_call(
        _ffn_kernel,
        out_shape=jax.ShapeDtypeStruct((r, d), F32),
        grid_spec=pltpu.PrefetchScalarGridSpec(
            num_scalar_prefetch=2, grid=(r // tm, nf),
            in_specs=[pl.BlockSpec((tm, d), lambda t, f, te, tr: (t, 0)),
                      pl.BlockSpec((1, d, tf), lambda t, f, te, tr: (te[t], 0, f_eff(t, f, tr))),
                      pl.BlockSpec((1, d, tf), lambda t, f, te, tr: (te[t], 0, f_eff(t, f, tr))),
                      pl.BlockSpec((1, tf, d), lambda t, f, te, tr: (te[t], f_eff(t, f, tr), 0))],
            out_specs=pl.BlockSpec((tm, d), lambda t, f, te, tr: (t, 0)),
            scratch_shapes=[pltpu.VMEM((d, tf), MXU_DTYPE), pltpu.VMEM((d, tf), MXU_DTYPE),
                            pltpu.VMEM((tf, d), MXU_DTYPE),
                            pltpu.VMEM((2, min(FFN_SUB_ROWS, tm), tf), MXU_DTYPE)]),
        compiler_params=_cparams(("arbitrary", "arbitrary")),
        name="swiglu_ffn",
    )(tile_expert, tile_rows, xs, w1, w3, w2)


def _gather_kernel(tok_ref, nrows_ref, h_ref, o_ref, buf_ref, sem):
    t = pl.program_id(0)
    n_chunks = pl.num_programs(0)
    sub = o_ref.shape[0]
    slot = t % 2

    def request(chunk, into):
        def start(r, carry):
            tok = tok_ref[chunk * sub + r]
            pltpu.make_async_copy(h_ref.at[pl.ds(tok, 1)], buf_ref.at[into, pl.ds(r, 1)],
                                  sem.at[into]).start()
            return carry

        lax.fori_loop(0, sub, start, 0)

    @pl.when((t == 0) & (nrows_ref[0] > 0))
    def _():
        request(0, 0)

    nxt = jnp.minimum(t + 1, n_chunks - 1)

    @pl.when((t + 1 < n_chunks) & (nrows_ref[nxt] > 0))
    def _():
        request(nxt, 1 - slot)

    @pl.when(nrows_ref[t] > 0)
    def _():
        pltpu.make_async_copy(h_ref.at[pl.ds(0, sub)], buf_ref.at[slot], sem.at[slot]).wait()
        o_ref[...] = buf_ref[slot].astype(o_ref.dtype)

    @pl.when(nrows_ref[t] == 0)
    def _():
        o_ref[...] = jnp.zeros_like(o_ref)


def _gather_rows(h, row_token, sub_rows, sub):
    s, d = h.shape
    r = row_token.shape[0]
    return pl.pallas_call(
        _gather_kernel,
        out_shape=jax.ShapeDtypeStruct((r, d), MXU_DTYPE),
        grid_spec=pltpu.PrefetchScalarGridSpec(
            num_scalar_prefetch=2, grid=(r // sub,),
            in_specs=[pl.BlockSpec(memory_space=pl.ANY)],
            out_specs=pl.BlockSpec((sub, d), lambda t, tok, nr: (t, 0)),
            scratch_shapes=[pltpu.VMEM((2, sub, d), F32), pltpu.SemaphoreType.DMA((2,))]),
        compiler_params=_cparams(("arbitrary",)),
        name="moe_gather",
    )(row_token, sub_rows, h)


def _combine_kernel(pos_ref, x_ref, wts_ref, gt_ref, ys_ref, o_ref, buf_ref, sem):
    t = pl.program_id(0)
    n_tiles = pl.num_programs(0)
    tm = x_ref.shape[0]
    slot = t % 2

    def request(tile, into):
        def start(r, carry):
            for k in range(TOP_K):
                src = pos_ref[(tile * tm + r) * TOP_K + k]
                pltpu.make_async_copy(ys_ref.at[pl.ds(src, 1)], buf_ref.at[into, k, pl.ds(r, 1)],
                                      sem.at[into]).start()
            return carry

        lax.fori_loop(0, tm, start, 0)

    @pl.when(t == 0)
    def _():
        request(0, 0)

    @pl.when(t + 1 < n_tiles)
    def _():
        request(t + 1, 1 - slot)

    for k in range(TOP_K):
        pltpu.make_async_copy(ys_ref.at[pl.ds(0, tm)], buf_ref.at[slot, k], sem.at[slot]).wait()
    wts = wts_ref[...]
    y = wts[:, 0:1] * buf_ref[slot, 0]
    for k in range(1, TOP_K):
        y = y + wts[:, k:k + 1] * buf_ref[slot, k]
    o_ref[...] = x_ref[...] + gt_ref[...] * y


def _combine_residual(x, wts, gt, ys, pos):
    s, d = x.shape
    tm = min(COMBINE_TILE, s)
    return pl.pallas_call(
        _combine_kernel,
        out_shape=jax.ShapeDtypeStruct((s, d), F32),
        grid_spec=pltpu.PrefetchScalarGridSpec(
            num_scalar_prefetch=1, grid=(s // tm,),
            in_specs=[pl.BlockSpec((tm, d), lambda t, pos: (t, 0)),
                      pl.BlockSpec((tm, LANES), lambda t, pos: (t, 0)),
                      pl.BlockSpec((1, d), lambda t, pos: (0, 0)),
                      pl.BlockSpec(memory_space=pl.ANY)],
            out_specs=pl.BlockSpec((tm, d), lambda t, pos: (t, 0)),
            scratch_shapes=[pltpu.VMEM((2, TOP_K, tm, d), F32), pltpu.SemaphoreType.DMA((2,))]),
        compiler_params=_cparams(("arbitrary",)),
        name="moe_combine",
    )(pos, x, wts, gt, ys)


def _moe_plan(idx, n_experts, tm, sub):
    s = idx.shape[0]
    n_slots = s * TOP_K
    e_flat = idx.reshape(-1)
    onehot = (e_flat[:, None] == jnp.arange(n_experts, dtype=jnp.int32)[None, :]).astype(jnp.int32)
    cnt = jnp.sum(onehot, axis=0)
    rank = jnp.sum((jnp.cumsum(onehot, axis=0) - onehot) * onehot, axis=1)
    ntile_e = (cnt + tm - 1) // tm
    tile_end = jnp.cumsum(ntile_e)
    tile_start = tile_end - ntile_e
    pos = tile_start[e_flat] * tm + rank
    n_tiles = n_slots // tm + n_experts
    row_token = jnp.zeros((n_tiles * tm,), jnp.int32).at[pos].set(
        jnp.arange(n_slots, dtype=jnp.int32) // TOP_K)
    t_ids = jnp.arange(n_tiles, dtype=jnp.int32)
    te = jnp.sum((t_ids[:, None] >= tile_end[None, :]).astype(jnp.int32), axis=1)
    used = te < n_experts
    te_c = jnp.minimum(te, n_experts - 1)
    rows = jnp.where(used, jnp.clip(cnt[te_c] - (t_ids - tile_start[te_c]) * tm, 0, tm), 0)
    e_last = jnp.max(jnp.where(ntile_e > 0, jnp.arange(n_experts, dtype=jnp.int32), 0))
    tile_expert = jnp.where(used, te_c, e_last).astype(jnp.int32)
    per = tm // sub
    s_ids = jnp.arange(n_tiles * per, dtype=jnp.int32)
    sub_rows = jnp.clip(rows[s_ids // per] - (s_ids % per) * sub, 0, sub)
    return pos.astype(jnp.int32), row_token, tile_expert, rows.astype(jnp.int32), sub_rows.astype(jnp.int32)


def kernel(x, c, w_ada, b_ada, g_mix, g_ffn, g_final, rel_bias, w_qkv_fox, w_f_fox, b_f_fox, w_o_fox, w_qkv_moba, w_o_moba, w1_dense, w3_dense, w2_dense, w_router, w1_moe, w3_moe, w2_moe):
    b, s, d = x.shape
    assert b == 1 and d % HEAD_DIM == 0
    depth = w_ada.shape[0]
    n_heads = d // HEAD_DIM
    n_experts = w_router.shape[2]
    tm = min(ROW_TILE, s)
    sub = min(FFN_SUB_ROWS, tm)

    mod = _ada_modulation(c, w_ada, b_ada)
    bias_tiles = _moba_bias_tiles(rel_bias, min(ATTN_TILE, s))
    dense_rows = jnp.full((s // tm,), tm, jnp.int32)
    moe_w13_shape = (-1,) + w1_moe.shape[2:]
    moe_w2_shape = (-1,) + w2_moe.shape[2:]

    xc = x.reshape(s, d)
    pending = None
    for i in range(depth):
        j = i // 2
        sh1, sc1, gt1, sh2, sc2, gt2 = [mod[i, :, k * d:(k + 1) * d] for k in range(6)]
        g1 = g_mix[i].reshape(1, d)
        g2 = g_ffn[i].reshape(1, d)

        if i % 2 == 0:
            r = _norm_modulate(xc, g1, sc1, sh1, mode="fox", res=pending,
                               w_f=w_f_fox[j], b_f=b_f_fox[j])
        else:
            r = _norm_modulate(xc, g1, sc1, sh1, mode="attn", res=pending)
        xc = r.get("x", xc)
        pending = None
        if i % 2 == 0:
            qkv = _qkv_projection(r["h"], w_qkv_fox, j)
            nt = r["f"].shape[0]
            f_rows = r["f"].transpose(1, 0, 2).reshape(n_heads, nt, 1, r["f"].shape[2])
            o = _fox_attention(qkv, f_rows, n_heads)
            xc = _out_projection_residual(o, w_o_fox, j, xc, gt1)
        else:
            qkv = _qkv_projection(r["h"], w_qkv_moba, j)
            o = _moba_attention(qkv, bias_tiles, rel_bias, n_heads)
            xc = _out_projection_residual(o, w_o_moba, j, xc, gt1)

        if i % 2 == 0:
            r = _norm_modulate(xc, g2, sc2, sh2, mode="dense")
            y = _swiglu_ffn(r["h"], w1_dense, w3_dense, w2_dense,
                            jnp.full((s // tm,), j, jnp.int32), dense_rows, tm)
            pending = (y, gt2)
        else:
            r = _norm_modulate(xc, g2, sc2, sh2, mode="moe", w_router=w_router[j])
            pos, row_token, tile_expert, tile_rows, sub_rows = _moe_plan(
                r["idx"][:, :TOP_K], n_experts, tm, sub)
            xs = _gather_rows(r["h"], row_token, sub_rows, sub)
            ys = _swiglu_ffn(xs, w1_moe.reshape(moe_w13_shape), w3_moe.reshape(moe_w13_shape),
                             w2_moe.reshape(moe_w2_shape), tile_expert + j * n_experts, tile_rows, tm)
            xc = _combine_residual(xc, r["wts"], gt2, ys, pos)

    zero = jnp.zeros((1, d), F32)
    r = _norm_modulate(xc, g_final.reshape(1, d), zero, zero, mode="final", res=pending)
    return r["h"].reshape(b, s, d)
```

```python
import functools
import math

import numpy as np
import jax
import jax.numpy as jnp
from jax import lax
from jax.experimental import pallas as pl
from jax.experimental.pallas import tpu as pltpu

HEAD_DIM = 128
MOBA_BLOCK = 256
MOBA_TOPK = 3
REL_BUCKETS = 32
REL_MAX_DIST = 128
TOP_K = 2
RMS_EPS = 1e-6
NEG_INF = -1e30
BELOW_NEG_INF = -3e38
LOG2E = math.log2(math.e)
QK_SCALE_LOG2 = HEAD_DIM ** -0.5 * LOG2E

LANES = 128
SUBLANES = 8
VMEM_LIMIT_BYTES = 56 * 1024 * 1024

MXU_DTYPE = jnp.bfloat16
F32 = jnp.float32

ROW_TILE = 1024
FFN_F_TILE = 256
FFN_SUB_ROWS = 256
ATTN_TILE = 512
NORM_TILE = 512
COMBINE_TILE = 256


def _cparams(semantics):
    return pltpu.CompilerParams(dimension_semantics=semantics,
                                vmem_limit_bytes=VMEM_LIMIT_BYTES)


def _largest_divisor(n, candidates):
    for c in candidates:
        if n % c == 0:
            return c
    raise ValueError(f"no tile in {candidates} divides {n}")


def _ada_kernel(c_ref, w_ref, b_ref, o_ref, ca_ref):
    c = c_ref[...]
    ca_ref[...] = c * jax.nn.sigmoid(c)
    d, tn = w_ref.shape[1], w_ref.shape[2]
    ch = min(d, 256)

    def body(r, acc):
        rows = pl.ds(pl.multiple_of(r * ch, ch), ch)
        prod = w_ref[0, rows, :] * ca_ref[rows, :]
        return acc + jnp.sum(prod.reshape(ch // SUBLANES, SUBLANES, tn), axis=0)

    acc = lax.fori_loop(0, d // ch, body, jnp.zeros((SUBLANES, tn), F32))
    o_ref[0] = jnp.sum(acc, axis=0, keepdims=True) + b_ref[0]


def _ada_modulation(c, w_ada, b_ada):
    depth, d, n = w_ada.shape
    tn = _largest_divisor(n, (1024, 512, 256, 128))
    return pl.pallas_call(
        _ada_kernel,
        out_shape=jax.ShapeDtypeStruct((depth, 1, n), F32),
        grid=(depth, n // tn),
        in_specs=[pl.BlockSpec((d, 1), lambda i, j: (0, 0)),
                  pl.BlockSpec((1, d, tn), lambda i, j: (i, 0, j)),
                  pl.BlockSpec((1, 1, tn), lambda i, j: (i, 0, j))],
        out_specs=pl.BlockSpec((1, 1, tn), lambda i, j: (i, 0, j)),
        scratch_shapes=[pltpu.VMEM((d, 1), F32)],
        compiler_params=_cparams(("arbitrary", "arbitrary")),
        name="ada_modulation",
    )(c.reshape(d, 1), w_ada, b_ada.reshape(depth, 1, n))


def _norm_kernel(*refs, has_res, mode, n_experts):
    refs = list(refs)
    x_ref = refs.pop(0)
    if has_res:
        y_ref, gt_ref = refs.pop(0), refs.pop(0)
    g_ref, sc_ref, sh_ref = refs.pop(0), refs.pop(0), refs.pop(0)
    if mode == "moe":
        wr_ref = refs.pop(0)
    if mode == "fox":
        wf_ref, bf_ref = refs.pop(0), refs.pop(0)
    if has_res:
        xo_ref = refs.pop(0)
    h_ref = refs.pop(0)

    x = x_ref[...]
    if has_res:
        x = x + gt_ref[...] * y_ref[...]
        xo_ref[...] = x
    ms = jnp.mean(x * x, axis=-1, keepdims=True)
    h = (x * lax.rsqrt(ms + RMS_EPS) * g_ref[...]) * (1.0 + sc_ref[...]) + sh_ref[...]
    h_ref[...] = h.astype(h_ref.dtype)

    if mode == "moe":
        idx_ref, wts_ref = refs.pop(0), refs.pop(0)
        logits = jnp.dot(h.astype(MXU_DTYPE), wr_ref[...], preferred_element_type=F32)
        lane = lax.broadcasted_iota(jnp.int32, logits.shape, 1)
        lg = jnp.where(lane < n_experts, logits, BELOW_NEG_INF)
        v0 = jnp.max(lg, axis=-1, keepdims=True)
        i0 = jnp.min(jnp.where(lg == v0, lane, LANES), axis=-1, keepdims=True)
        lg = jnp.where(lane == i0, BELOW_NEG_INF, lg)
        v1 = jnp.max(lg, axis=-1, keepdims=True)
        i1 = jnp.min(jnp.where(lg == v1, lane, LANES), axis=-1, keepdims=True)
        e1 = jnp.exp(v1 - v0)
        den = 1.0 + e1
        idx_ref[...] = jnp.where(lane == 0, i0, jnp.where(lane == 1, i1, 0))
        wts_ref[...] = jnp.where(lane == 0, 1.0 / den, jnp.where(lane == 1, e1 / den, 0.0))

    if mode == "fox":
        f_ref, carry_ref = refs.pop(0), refs.pop(0)

        @pl.when(pl.program_id(0) == 0)
        def _():
            carry_ref[...] = jnp.zeros_like(carry_ref)

        tm = x.shape[0]
        z = lax.dot_general(wf_ref[...], h.astype(MXU_DTYPE), (((1,), (1,)), ((), ())),
                            preferred_element_type=F32) + bf_ref[...]
        log_f = jnp.minimum(z, 0.0) - jnp.log1p(jnp.exp(-jnp.abs(z)))
        upper = (lax.broadcasted_iota(jnp.int32, (tm, tm), 0)
                 <= lax.broadcasted_iota(jnp.int32, (tm, tm), 1)).astype(F32)
        cum = jnp.dot(log_f, upper, preferred_element_type=F32,
                      precision=lax.Precision.HIGHEST) + carry_ref[...]
        f_ref[0] = cum * LOG2E
        carry_ref[...] = cum[:, tm - 1:tm]


def _norm_modulate(x, g, sc, sh, *, mode, res=None, w_router=None, w_f=None, b_f=None):
    s, d = x.shape
    tm = min(NORM_TILE, s)
    row = pl.BlockSpec((tm, d), lambda i: (i, 0))
    vec = pl.BlockSpec((1, d), lambda i: (0, 0))
    args, in_specs, out_shape, out_specs, scratch = [x], [row], [], [], []
    if res is not None:
        y, gt = res
        args += [y, gt]
        in_specs += [row, vec]
        out_shape.append(jax.ShapeDtypeStruct((s, d), F32))
        out_specs.append(row)
    args += [g, sc, sh]
    in_specs += [vec, vec, vec]
    h_dtype = {"attn": MXU_DTYPE, "fox": MXU_DTYPE, "dense": MXU_DTYPE,
               "moe": F32, "final": F32}[mode]
    out_shape.append(jax.ShapeDtypeStruct((s, d), h_dtype))
    out_specs.append(row)
    n_experts = 0
    if mode == "moe":
        n_experts = w_router.shape[1]
        wr = jnp.zeros((d, LANES), MXU_DTYPE).at[:, :n_experts].set(w_router.astype(MXU_DTYPE))
        args.append(wr)
        in_specs.append(pl.BlockSpec((d, LANES), lambda i: (0, 0)))
        lane_blk = pl.BlockSpec((tm, LANES), lambda i: (i, 0))
        out_shape += [jax.ShapeDtypeStruct((s, LANES), jnp.int32),
                      jax.ShapeDtypeStruct((s, LANES), F32)]
        out_specs += [lane_blk, lane_blk]
    if mode == "fox":
        nh = w_f.shape[1]
        args += [w_f.T.astype(MXU_DTYPE), b_f.reshape(nh, 1)]
        in_specs += [pl.BlockSpec((nh, d), lambda i: (0, 0)),
                     pl.BlockSpec((nh, 1), lambda i: (0, 0))]
        out_shape.append(jax.ShapeDtypeStruct((s // tm, nh, tm), F32))
        out_specs.append(pl.BlockSpec((1, nh, tm), lambda i: (i, 0, 0)))
        scratch.append(pltpu.VMEM((nh, 1), F32))
    outs = pl.pallas_call(
        functools.partial(_norm_kernel, has_res=res is not None, mode=mode, n_experts=n_experts),
        out_shape=out_shape,
        grid=(s // tm,),
        in_specs=in_specs,
        out_specs=out_specs,
        scratch_shapes=scratch,
        compiler_params=_cparams(("arbitrary",)),
        name=f"norm_{mode}",
    )(*args)
    outs = list(outs)
    result = {}
    if res is not None:
        result["x"] = outs.pop(0)
    result["h"] = outs.pop(0)
    if mode == "moe":
        result["idx"], result["wts"] = outs.pop(0), outs.pop(0)
    if mode == "fox":
        result["f"] = outs.pop(0)
    return result


def _qkv_kernel(a_ref, w_ref, cs_ref, o_ref, wb_ref):
    @pl.when(pl.program_id(1) == 0)
    def _():
        wb_ref[...] = w_ref[0].astype(wb_ref.dtype)

    acc = jnp.dot(a_ref[...], wb_ref[...], preferred_element_type=F32) * cs_ref[...]
    for j in range(o_ref.shape[0]):
        o_ref[j] = acc[:, j * HEAD_DIM:(j + 1) * HEAD_DIM].astype(o_ref.dtype)


def _qkv_projection(h, w, layer):
    s, d = h.shape
    n = w.shape[2]
    n_heads3 = n // HEAD_DIM
    tm = min(ROW_TILE, s)
    nh = _largest_divisor(n_heads3, (6, 4, 3, 2, 1))
    tn = nh * HEAD_DIM
    col_scale = jnp.where(jnp.arange(n) < n // 3, QK_SCALE_LOG2, 1.0).astype(F32).reshape(1, n)
    return pl.pallas_call(
        _qkv_kernel,
        out_shape=jax.ShapeDtypeStruct((n_heads3, s, HEAD_DIM), MXU_DTYPE),
        grid=(n // tn, s // tm),
        in_specs=[pl.BlockSpec((tm, d), lambda j, i: (i, 0)),
                  pl.BlockSpec((1, d, tn), lambda j, i: (layer, 0, j)),
                  pl.BlockSpec((1, tn), lambda j, i: (0, j))],
        out_specs=pl.BlockSpec((nh, tm, HEAD_DIM), lambda j, i: (j, i, 0)),
        scratch_shapes=[pltpu.VMEM((d, tn), MXU_DTYPE)],
        compiler_params=_cparams(("arbitrary", "arbitrary")),
        name="qkv_projection",
    )(h, w, col_scale)


def _out_proj_kernel(a_ref, w_ref, x_ref, gt_ref, o_ref, wb_ref):
    @pl.when(pl.program_id(1) == 0)
    def _():
        wb_ref[...] = w_ref[0].astype(wb_ref.dtype)

    acc = jnp.dot(a_ref[...], wb_ref[...], preferred_element_type=F32)
    o_ref[...] = x_ref[...] + gt_ref[...] * acc


def _out_projection_residual(a, w, layer, x, gt):
    s, k = a.shape
    n = w.shape[2]
    tm = min(ROW_TILE, s)
    tn = _largest_divisor(n, (512, 256, 128))
    return pl.pallas_call(
        _out_proj_kernel,
        out_shape=jax.ShapeDtypeStruct((s, n), F32),
        grid=(n // tn, s // tm),
        in_specs=[pl.BlockSpec((tm, k), lambda j, i: (i, 0)),
                  pl.BlockSpec((1, k, tn), lambda j, i: (layer, 0, j)),
                  pl.BlockSpec((tm, tn), lambda j, i: (i, j)),
                  pl.BlockSpec((1, tn), lambda j, i: (0, j))],
        out_specs=pl.BlockSpec((tm, tn), lambda j, i: (i, j)),
        scratch_shapes=[pltpu.VMEM((k, tn), MXU_DTYPE)],
        compiler_params=_cparams(("arbitrary", "arbitrary")),
        name="out_projection",
    )(a, w, x, gt)


def _lane_tiles(x):
    return [x[:, c * LANES:(c + 1) * LANES] for c in range(x.shape[1] // LANES)]


def _softmax_step(s, v_aug, m_ref, acc_ref, first):
    tiles = _lane_tiles(s)
    m_cur = jnp.max(functools.reduce(jnp.maximum, tiles), axis=-1, keepdims=True)
    if first:
        m_new = jnp.broadcast_to(m_cur, m_ref.shape)
    else:
        m_old = m_ref[...]
        m_new = jnp.maximum(m_old, m_cur)
    p = jnp.concatenate([jnp.exp2(t - m_new) for t in tiles], axis=1).astype(MXU_DTYPE)
    pv = jnp.dot(p, v_aug, preferred_element_type=F32)
    if first:
        acc_ref[...] = pv
    else:
        alpha = jnp.exp2(m_old - m_new)
        acc_ref[...] = jnp.concatenate([alpha] * (acc_ref.shape[1] // LANES), axis=1) * acc_ref[...] + pv
    m_ref[...] = m_new


def _pipelined_tiles(n, qk, consume, s_ref):
    s_ref[0] = qk(0)

    def step(j, cur):
        s = s_ref[cur]
        s_ref[1 - cur] = qk(jnp.minimum(j + 1, n - 1))
        consume(j, s)

    def pair(i, carry):
        step(2 * i, 0)
        step(2 * i + 1, 1)
        return carry

    lax.fori_loop(0, n // 2, pair, 0)

    @pl.when(n % 2 == 1)
    def _():
        consume(n - 1, s_ref[0])


def _augment_values(v_ref, vaug_ref):
    hd = v_ref.shape[-1]
    for j in range(v_ref.shape[1]):
        vaug_ref[j, :, :hd] = v_ref[0, j]
        vaug_ref[j, :, hd:] = jnp.ones((v_ref.shape[2], hd), vaug_ref.dtype)


def _fox_kernel(q_ref, k_ref, v_ref, f_ref, o_ref, vaug_ref, s_ref, m_ref, acc_ref):
    i = pl.program_id(1)
    hd = q_ref.shape[2]

    @pl.when(i == 0)
    def _():
        _augment_values(v_ref, vaug_ref)

    q = q_ref[0]
    tq = q.shape[0]

    def qk(j):
        return lax.dot_general(q, k_ref[0, j], (((1,), (1,)), ((), ())),
                               preferred_element_type=F32)

    causal = (lax.broadcasted_iota(jnp.int32, (tq, tq), 1)
              <= lax.broadcasted_iota(jnp.int32, (tq, tq), 0))
    _softmax_step(jnp.where(causal, qk(i) - f_ref[0, i], NEG_INF), vaug_ref[i], m_ref, acc_ref, True)

    @pl.when(i > 0)
    def _():
        def consume(j, s):
            _softmax_step(s - f_ref[0, j], vaug_ref[j], m_ref, acc_ref, False)

        _pipelined_tiles(i, qk, consume, s_ref)

    acc = acc_ref[...]
    o_ref[...] = (acc[:, :hd] / acc[:, hd:]).astype(o_ref.dtype)


def _fox_attention(qkv, f_rows, n_heads):
    _, s, hd = qkv.shape
    t = min(ATTN_TILE, s)
    nb = s // t
    kv = qkv.reshape(3 * n_heads, nb, t, hd)
    return pl.pallas_call(
        _fox_kernel,
        out_shape=jax.ShapeDtypeStruct((s, n_heads * hd), MXU_DTYPE),
        grid=(n_heads, nb),
        in_specs=[pl.BlockSpec((1, t, hd), lambda h, i: (h, i, 0)),
                  pl.BlockSpec((1, nb, t, hd), lambda h, i: (n_heads + h, 0, 0, 0)),
                  pl.BlockSpec((1, nb, t, hd), lambda h, i: (2 * n_heads + h, 0, 0, 0)),
                  pl.BlockSpec((1, nb, 1, t), lambda h, i: (h, 0, 0, 0))],
        out_specs=pl.BlockSpec((t, hd), lambda h, i: (i, h)),
        scratch_shapes=[pltpu.VMEM((nb, t, 2 * hd), MXU_DTYPE),
                        pltpu.VMEM((2, t, t), F32),
                        pltpu.VMEM((t, LANES), F32),
                        pltpu.VMEM((t, 2 * hd), F32)],
        compiler_params=_cparams(("arbitrary", "arbitrary")),
        name="fox_attention",
    )(qkv, kv, kv, f_rows)


def _rel_bucket_np(dist):
    n = np.maximum(dist, 0)
    max_exact = REL_BUCKETS // 2
    nf = np.maximum(n, 1).astype(np.float32)
    large = max_exact + (np.log(nf / np.float32(max_exact))
                         / np.float32(math.log(REL_MAX_DIST / max_exact))
                         * np.float32(REL_BUCKETS - max_exact)).astype(np.int32)
    large = np.minimum(large, REL_BUCKETS - 1)
    return np.where(n < max_exact, n, large).astype(np.int32)


def _bias_table_kernel(rb_ref, idx_ref, o_ref, *, n_heads):
    h = pl.program_id(0)
    idx = idx_ref[0]
    bias = jnp.zeros(idx.shape, F32)
    far = rb_ref[(REL_BUCKETS - 1) * n_heads + h]
    for b in range(REL_BUCKETS):
        bias = jnp.where(idx == b, (rb_ref[b * n_heads + h] - far) * LOG2E, bias)
    o_ref[0, 0] = jnp.where(idx < 0, NEG_INF, bias)


def _moba_bias_tiles(rel_bias, t):
    n_heads = rel_bias.shape[1]
    dist = np.arange(2)[:, None, None] * t + np.arange(t)[None, :, None] - np.arange(t)[None, None, :]
    idx = jnp.asarray(np.where(dist < 0, -1, _rel_bucket_np(dist)).astype(np.int32))
    return pl.pallas_call(
        functools.partial(_bias_table_kernel, n_heads=n_heads),
        out_shape=jax.ShapeDtypeStruct((n_heads, 2, t, t), F32),
        grid_spec=pltpu.PrefetchScalarGridSpec(
            num_scalar_prefetch=1, grid=(n_heads, 2),
            in_specs=[pl.BlockSpec((1, t, t), lambda h, dt, rb: (dt, 0, 0))],
            out_specs=pl.BlockSpec((1, 1, t, t), lambda h, dt, rb: (h, dt, 0, 0))),
        compiler_params=_cparams(("arbitrary", "arbitrary")),
        name="moba_bias_tiles",
    )(rel_bias.reshape(-1), idx)


def _split3(x):
    hi = x.astype(MXU_DTYPE)
    r1 = x - hi.astype(F32)
    mid = r1.astype(MXU_DTYPE)
    lo = (r1 - mid.astype(F32)).astype(MXU_DTYPE)
    return hi, mid, lo


def _moba_kernel(q_ref, k_ref, v_ref, bias_ref, far_ref, o_ref,
                 vaug_ref, kaug_ref, qaug_ref, kmean_ref, selt_ref, s_ref, m_ref, acc_ref):
    t = pl.program_id(1)
    n_tiles, tq, hd = k_ref.shape[1], k_ref.shape[2], k_ref.shape[3]
    bpt = tq // MOBA_BLOCK
    n_blocks = n_tiles * bpt
    nbp = kmean_ref.shape[0]
    far_lane0 = LANES - 3

    @pl.when(t == 0)
    def _():
        _augment_values(v_ref, vaug_ref)
        kmean_ref[...] = jnp.zeros_like(kmean_ref)
        selt_ref[...] = jnp.zeros_like(selt_ref)
        lane = lax.broadcasted_iota(jnp.int32, (tq, LANES), 1)
        row_blk = lax.broadcasted_iota(jnp.int32, (tq, LANES), 0) // MOBA_BLOCK
        for j in range(n_tiles):
            kaug_ref[j, :, :hd] = k_ref[0, j]
            kaug_ref[j, :, hd:] = ((lane == j * bpt + row_blk) | (lane >= far_lane0)).astype(kaug_ref.dtype)
        for n in range(n_blocks):
            rows = slice((n % bpt) * MOBA_BLOCK, (n % bpt + 1) * MOBA_BLOCK)
            kmean_ref[n:n + 1, :] = jnp.mean(k_ref[0, n // bpt, rows, :].astype(F32), axis=0,
                                             keepdims=True)

    q = q_ref[0]

    gate = lax.dot_general(kmean_ref[...].astype(MXU_DTYPE), q, (((1,), (1,)), ((), ())),
                           preferred_element_type=F32)
    blk = lax.broadcasted_iota(jnp.int32, gate.shape, 0)
    own = t * bpt + lax.broadcasted_iota(jnp.int32, gate.shape, 1) // MOBA_BLOCK
    past = blk < own
    g = jnp.where(past, gate, NEG_INF)
    sel = jnp.where(blk == own, 0.0, NEG_INF)
    for _ in range(MOBA_TOPK):
        mx = jnp.max(g, axis=0, keepdims=True)
        pick = blk == jnp.min(jnp.where(g == mx, blk, nbp), axis=0, keepdims=True)
        sel = jnp.where(pick & past, 0.0, sel)
        g = jnp.where(pick, BELOW_NEG_INF, g)
    selt_ref[0:nbp, :] = sel
    far_hi, far_mid, far_lo = _split3(far_ref[0])
    lane = lax.broadcasted_iota(jnp.int32, (tq, LANES), 1)
    extra = selt_ref[...].T.astype(MXU_DTYPE)
    extra = jnp.where(lane == far_lane0, far_hi,
                      jnp.where(lane == far_lane0 + 1, far_mid,
                                jnp.where(lane == far_lane0 + 2, far_lo, extra)))
    qaug_ref[:, :hd] = q
    qaug_ref[:, hd:] = extra
    q_aug = qaug_ref[...]

    def qk(j):
        return lax.dot_general(q_aug, kaug_ref[j], (((1,), (1,)), ((), ())),
                               preferred_element_type=F32)

    _softmax_step(qk(t) + bias_ref[0, 0], vaug_ref[t], m_ref, acc_ref, True)

    @pl.when(t >= 1)
    def _():
        _softmax_step(qk(t - 1) + bias_ref[0, 1], vaug_ref[t - 1], m_ref, acc_ref, False)

    @pl.when(t >= 2)
    def _():
        def consume(j, s):
            _softmax_step(s, vaug_ref[j], m_ref, acc_ref, False)

        _pipelined_tiles(t - 1, qk, consume, s_ref)

    acc = acc_ref[...]
    o_ref[...] = (acc[:, :hd] / acc[:, hd:]).astype(o_ref.dtype)


def _moba_attention(qkv, bias_tiles, rel_bias, n_heads):
    _, s, hd = qkv.shape
    t = bias_tiles.shape[2]
    assert s % t == 0 and t % MOBA_BLOCK == 0 and t >= REL_MAX_DIST and hd == LANES
    assert s // MOBA_BLOCK <= LANES - 3
    n_tiles = s // t
    nbp = -(-(s // MOBA_BLOCK) // SUBLANES) * SUBLANES
    kv = qkv.reshape(3 * n_heads, n_tiles, t, hd)
    far = jnp.broadcast_to(rel_bias[REL_BUCKETS - 1][:, None, None] * LOG2E, (n_heads, 1, LANES))
    return pl.pallas_call(
        _moba_kernel,
        out_shape=jax.ShapeDtypeStruct((s, n_heads * hd), MXU_DTYPE),
        grid=(n_heads, n_tiles),
        in_specs=[pl.BlockSpec((1, t, hd), lambda h, i: (h, i, 0)),
                  pl.BlockSpec((1, n_tiles, t, hd), lambda h, i: (n_heads + h, 0, 0, 0)),
                  pl.BlockSpec((1, n_tiles, t, hd), lambda h, i: (2 * n_heads + h, 0, 0, 0)),
                  pl.BlockSpec((1, 2, t, t), lambda h, i: (h, 0, 0, 0)),
                  pl.BlockSpec((1, 1, LANES), lambda h, i: (h, 0, 0))],
        out_specs=pl.BlockSpec((t, hd), lambda h, i: (i, h)),
        scratch_shapes=[pltpu.VMEM((n_tiles, t, 2 * hd), MXU_DTYPE),
                        pltpu.VMEM((n_tiles, t, 2 * hd), MXU_DTYPE),
                        pltpu.VMEM((t, 2 * hd), MXU_DTYPE),
                        pltpu.VMEM((nbp, hd), F32),
                        pltpu.VMEM((LANES, t), F32),
                        pltpu.VMEM((2, t, t), F32),
                        pltpu.VMEM((t, LANES), F32),
                        pltpu.VMEM((t, 2 * hd), F32)],
        compiler_params=_cparams(("arbitrary", "arbitrary")),
        name="moba_attention",
    )(qkv, kv, kv, bias_tiles, far)


def _ffn_kernel(te_ref, tr_ref, x_ref, w1_ref, w3_ref, w2_ref, o_ref, w1b_ref, w3b_ref, w2b_ref, g_ref):
    t = pl.program_id(0)
    f = pl.program_id(1)
    rows = tr_ref[t]
    tm = x_ref.shape[0]
    sub = min(FFN_SUB_ROWS, tm)

    def gate_up(r):
        x = x_ref[r, :]
        a = jnp.dot(x, w1b_ref[...], preferred_element_type=F32)
        b = jnp.dot(x, w3b_ref[...], preferred_element_type=F32)
        return (a * jax.nn.sigmoid(a) * b).astype(MXU_DTYPE)

    def down(g, r):
        o_ref[r, :] += jnp.dot(g, w2b_ref[...], preferred_element_type=F32)

    def cast_gate_up_weights():
        w1b_ref[...] = w1_ref[0].astype(w1b_ref.dtype)
        w3b_ref[...] = w3_ref[0].astype(w3b_ref.dtype)

    def cast_down_weights():
        w2b_ref[...] = w2_ref[0].astype(w2b_ref.dtype)

    @pl.when(f == 0)
    def _():
        o_ref[...] = jnp.zeros_like(o_ref)

    @pl.when(rows == tm)
    def _():
        cast_gate_up_weights()
        cast_down_weights()
        for c in range(tm // sub):
            r = pl.ds(c * sub, sub)
            down(gate_up(r), r)

    @pl.when((rows > 0) & (rows < tm))
    def _():
        def chunk(c):
            return pl.ds(pl.multiple_of(c * sub, sub), sub)

        n_chunks = (rows + sub - 1) // sub
        cast_gate_up_weights()
        g_ref[0] = gate_up(chunk(0))
        cast_down_weights()

        def body(c, carry):
            down(g_ref[(c - 1) % 2], chunk(c - 1))
            g_ref[c % 2] = gate_up(chunk(c))
            return carry

        lax.fori_loop(1, n_chunks, body, 0)
        down(g_ref[(n_chunks - 1) % 2], chunk(n_chunks - 1))


def _swiglu_ffn(xs, w1, w3, w2, tile_expert, tile_rows, tm):
    r, d = xs.shape
    ff = w1.shape[2]
    tf = _largest_divisor(ff, (FFN_F_TILE, 128))
    nf = ff // tf

    def f_eff(t, f, tr):
        return jnp.where(tr[t] > 0, f, nf - 1)

    return pl.pallas_call(
        _ffn_kernel,
        out_shape=jax.ShapeDtypeStruct((r, d), F32),
        grid_spec=pltpu.PrefetchScalarGridSpec(
            num_scalar_prefetch=2, grid=(r // tm, nf),
            in_specs=[pl.BlockSpec((tm, d), lambda t, f, te, tr: (t, 0)),
                      pl.BlockSpec((1, d, tf), lambda t, f, te, tr: (te[t], 0, f_eff(t, f, tr))),
                      pl.BlockSpec((1, d, tf), lambda t, f, te, tr: (te[t], 0, f_eff(t, f, tr))),
                      pl.BlockSpec((1, tf, d), lambda t, f, te, tr: (te[t], f_eff(t, f, tr), 0))],
            out_specs=pl.BlockSpec((tm, d), lambda t, f, te, tr: (t, 0)),
            scratch_shapes=[pltpu.VMEM((d, tf), MXU_DTYPE), pltpu.VMEM((d, tf), MXU_DTYPE),
                            pltpu.VMEM((tf, d), MXU_DTYPE),
                            pltpu.VMEM((2, min(FFN_SUB_ROWS, tm), tf), MXU_DTYPE)]),
        compiler_params=_cparams(("arbitrary", "arbitrary")),
        name="swiglu_ffn",
    )(tile_expert, tile_rows, xs, w1, w3, w2)


def _gather_kernel(tok_ref, nrows_ref, h_ref, o_ref, buf_ref, sem):
    t = pl.program_id(0)
    n_chunks = pl.num_programs(0)
    sub = o_ref.shape[0]
    slot = t % 2

    def request(chunk, into):
        def start(r, carry):
            tok = tok_ref[chunk * sub + r]
            pltpu.make_async_copy(h_ref.at[pl.ds(tok, 1)], buf_ref.at[into, pl.ds(r, 1)],
                                  sem.at[into]).start()
            return carry

        lax.fori_loop(0, sub, start, 0)

    @pl.when((t == 0) & (nrows_ref[0] > 0))
    def _():
        request(0, 0)

    nxt = jnp.minimum(t + 1, n_chunks - 1)

    @pl.when((t + 1 < n_chunks) & (nrows_ref[nxt] > 0))
    def _():
        request(nxt, 1 - slot)

    @pl.when(nrows_ref[t] > 0)
    def _():
        pltpu.make_async_copy(h_ref.at[pl.ds(0, sub)], buf_ref.at[slot], sem.at[slot]).wait()
        o_ref[...] = buf_ref[slot].astype(o_ref.dtype)

    @pl.when(nrows_ref[t] == 0)
    def _():
        o_ref[...] = jnp.zeros_like(o_ref)


def _gather_rows(h, row_token, sub_rows, sub):
    s, d = h.shape
    r = row_token.shape[0]
    return pl.pallas_call(
        _gather_kernel,
        out_shape=jax.ShapeDtypeStruct((r, d), MXU_DTYPE),
        grid_spec=pltpu.PrefetchScalarGridSpec(
            num_scalar_prefetch=2, grid=(r // sub,),
            in_specs=[pl.BlockSpec(memory_space=pl.ANY)],
            out_specs=pl.BlockSpec((sub, d), lambda t, tok, nr: (t, 0)),
            scratch_shapes=[pltpu.VMEM((2, sub, d), F32), pltpu.SemaphoreType.DMA((2,))]),
        compiler_params=_cparams(("arbitrary",)),
        name="moe_gather",
    )(row_token, sub_rows, h)


def _combine_kernel(pos_ref, x_ref, wts_ref, gt_ref, ys_ref, o_ref, buf_ref, sem):
    t = pl.program_id(0)
    n_tiles = pl.num_programs(0)
    tm = x_ref.shape[0]
    slot = t % 2

    def request(tile, into):
        def start(r, carry):
            for k in range(TOP_K):
                src = pos_ref[(tile * tm + r) * TOP_K + k]
                pltpu.make_async_copy(ys_ref.at[pl.ds(src, 1)], buf_ref.at[into, k, pl.ds(r, 1)],
                                      sem.at[into]).start()
            return carry

        lax.fori_loop(0, tm, start, 0)

    @pl.when(t == 0)
    def _():
        request(0, 0)

    @pl.when(t + 1 < n_tiles)
    def _():
        request(t + 1, 1 - slot)

    for k in range(TOP_K):
        pltpu.make_async_copy(ys_ref.at[pl.ds(0, tm)], buf_ref.at[slot, k], sem.at[slot]).wait()
    wts = wts_ref[...]
    y = wts[:, 0:1] * buf_ref[slot, 0]
    for k in range(1, TOP_K):
        y = y + wts[:, k:k + 1] * buf_ref[slot, k]
    o_ref[...] = x_ref[...] + gt_ref[...] * y


def _combine_residual(x, wts, gt, ys, pos):
    s, d = x.shape
    tm = min(COMBINE_TILE, s)
    return pl.pallas_call(
        _combine_kernel,
        out_shape=jax.ShapeDtypeStruct((s, d), F32),
        grid_spec=pltpu.PrefetchScalarGridSpec(
            num_scalar_prefetch=1, grid=(s // tm,),
            in_specs=[pl.BlockSpec((tm, d), lambda t, pos: (t, 0)),
                      pl.BlockSpec((tm, LANES), lambda t, pos: (t, 0)),
                      pl.BlockSpec((1, d), lambda t, pos: (0, 0)),
                      pl.BlockSpec(memory_space=pl.ANY)],
            out_specs=pl.BlockSpec((tm, d), lambda t, pos: (t, 0)),
            scratch_shapes=[pltpu.VMEM((2, TOP_K, tm, d), F32), pltpu.SemaphoreType.DMA((2,))]),
        compiler_params=_cparams(("arbitrary",)),
        name="moe_combine",
    )(pos, x, wts, gt, ys)


def _moe_plan(idx, n_experts, tm, sub):
    s = idx.shape[0]
    n_slots = s * TOP_K
    e_flat = idx.reshape(-1)
    onehot = (e_flat[:, None] == jnp.arange(n_experts, dtype=jnp.int32)[None, :]).astype(jnp.int32)
    cnt = jnp.sum(onehot, axis=0)
    rank = jnp.sum((jnp.cumsum(onehot, axis=0) - onehot) * onehot, axis=1)
    ntile_e = (cnt + tm - 1) // tm
    tile_end = jnp.cumsum(ntile_e)
    tile_start = tile_end - ntile_e
    pos = tile_start[e_flat] * tm + rank
    n_tiles = n_slots // tm + n_experts
    row_token = jnp.zeros((n_tiles * tm,), jnp.int32).at[pos].set(
        jnp.arange(n_slots, dtype=jnp.int32) // TOP_K)
    t_ids = jnp.arange(n_tiles, dtype=jnp.int32)
    te = jnp.sum((t_ids[:, None] >= tile_end[None, :]).astype(jnp.int32), axis=1)
    used = te < n_experts
    te_c = jnp.minimum(te, n_experts - 1)
    rows = jnp.where(used, jnp.clip(cnt[te_c] - (t_ids - tile_start[te_c]) * tm, 0, tm), 0)
    e_last = jnp.max(jnp.where(ntile_e > 0, jnp.arange(n_experts, dtype=jnp.int32), 0))
    tile_expert = jnp.where(used, te_c, e_last).astype(jnp.int32)
    per = tm // sub
    s_ids = jnp.arange(n_tiles * per, dtype=jnp.int32)
    sub_rows = jnp.clip(rows[s_ids // per] - (s_ids % per) * sub, 0, sub)
    return pos.astype(jnp.int32), row_token, tile_expert, rows.astype(jnp.int32), sub_rows.astype(jnp.int32)


def kernel(x, c, w_ada, b_ada, g_mix, g_ffn, g_final, rel_bias, w_qkv_fox, w_f_fox, b_f_fox, w_o_fox, w_qkv_moba, w_o_moba, w1_dense, w3_dense, w2_dense, w_router, w1_moe, w3_moe, w2_moe):
    b, s, d = x.shape
    assert b == 1 and d % HEAD_DIM == 0
    depth = w_ada.shape[0]
    n_heads = d // HEAD_DIM
    n_experts = w_router.shape[2]
    tm = min(ROW_TILE, s)
    sub = min(FFN_SUB_ROWS, tm)

    mod = _ada_modulation(c, w_ada, b_ada)
    bias_tiles = _moba_bias_tiles(rel_bias, min(ATTN_TILE, s))
    dense_rows = jnp.full((s // tm,), tm, jnp.int32)
    moe_w13_shape = (-1,) + w1_moe.shape[2:]
    moe_w2_shape = (-1,) + w2_moe.shape[2:]

    xc = x.reshape(s, d)
    pending = None
    for i in range(depth):
        j = i // 2
        sh1, sc1, gt1, sh2, sc2, gt2 = [mod[i, :, k * d:(k + 1) * d] for k in range(6)]
        g1 = g_mix[i].reshape(1, d)
        g2 = g_ffn[i].reshape(1, d)

        if i % 2 == 0:
            r = _norm_modulate(xc, g1, sc1, sh1, mode="fox", res=pending,
                               w_f=w_f_fox[j], b_f=b_f_fox[j])
        else:
            r = _norm_modulate(xc, g1, sc1, sh1, mode="attn", res=pending)
        xc = r.get("x", xc)
        pending = None
        if i % 2 == 0:
            qkv = _qkv_projection(r["h"], w_qkv_fox, j)
            nt = r["f"].shape[0]
            f_rows = r["f"].transpose(1, 0, 2).reshape(n_heads, nt, 1, r["f"].shape[2])
            o = _fox_attention(qkv, f_rows, n_heads)
            xc = _out_projection_residual(o, w_o_fox, j, xc, gt1)
        else:
            qkv = _qkv_projection(r["h"], w_qkv_moba, j)
            o = _moba_attention(qkv, bias_tiles, rel_bias, n_heads)
            xc = _out_projection_residual(o, w_o_moba, j, xc, gt1)

        if i % 2 == 0:
            r = _norm_modulate(xc, g2, sc2, sh2, mode="dense")
            y = _swiglu_ffn(r["h"], w1_dense, w3_dense, w2_dense,
                            jnp.full((s // tm,), j, jnp.int32), dense_rows, tm)
            pending = (y, gt2)
        else:
            r = _norm_modulate(xc, g2, sc2, sh2, mode="moe", w_router=w_router[j])
            pos, row_token, tile_expert, tile_rows, sub_rows = _moe_plan(
                r["idx"][:, :TOP_K], n_experts, tm, sub)
            xs = _gather_rows(r["h"], row_token, sub_rows, sub)
            ys = _swiglu_ffn(xs, w1_moe.reshape(moe_w13_shape), w3_moe.reshape(moe_w13_shape),
                             w2_moe.reshape(moe_w2_shape), tile_expert + j * n_experts, tile_rows, tm)
            xc = _combine_residual(xc, r["wts"], gt2, ys, pos)

    zero = jnp.zeros((1, d), F32)
    r = _norm_modulate(xc, g_final.reshape(1, d), zero, zero, mode="final", res=pending)
    return r["h"].reshape(b, s, d)
```

```python
import functools
import math

import numpy as np
import jax
import jax.numpy as jnp
from jax import lax
from jax.experimental import pallas as pl
from jax.experimental.pallas import tpu as pltpu

HEAD_DIM = 128
MOBA_BLOCK = 256
MOBA_TOPK = 3
REL_BUCKETS = 32
REL_MAX_DIST = 128
TOP_K = 2
RMS_EPS = 1e-6
NEG_INF = -1e30
BELOW_NEG_INF = -3e38
LOG2E = math.log2(math.e)
QK_SCALE_LOG2 = HEAD_DIM ** -0.5 * LOG2E

LANES = 128
SUBLANES = 8
VMEM_LIMIT_BYTES = 56 * 1024 * 1024

MXU_DTYPE = jnp.bfloat16
F32 = jnp.float32

ROW_TILE = 1024
FFN_F_TILE = 256
FFN_SUB_ROWS = 256
ATTN_TILE = 512
NORM_TILE = 512
COMBINE_TILE = 256
DMA_ISSUE_UNROLL = 8


def _cparams(semantics):
    return pltpu.CompilerParams(dimension_semantics=semantics,
                                vmem_limit_bytes=VMEM_LIMIT_BYTES)


def _largest_divisor(n, candidates):
    for c in candidates:
        if n % c == 0:
            return c
    raise ValueError(f"no tile in {candidates} divides {n}")


def _ada_kernel(c_ref, w_ref, b_ref, o_ref, ca_ref):
    c = c_ref[...]
    ca_ref[...] = c * jax.nn.sigmoid(c)
    d, tn = w_ref.shape[1], w_ref.shape[2]
    ch = min(d, 256)

    def body(r, acc):
        rows = pl.ds(pl.multiple_of(r * ch, ch), ch)
        prod = w_ref[0, rows, :] * ca_ref[rows, :]
        return acc + jnp.sum(prod.reshape(ch // SUBLANES, SUBLANES, tn), axis=0)

    acc = lax.fori_loop(0, d // ch, body, jnp.zeros((SUBLANES, tn), F32))
    o_ref[0] = jnp.sum(acc, axis=0, keepdims=True) + b_ref[0]


def _ada_modulation(c, w_ada, b_ada):
    depth, d, n = w_ada.shape
    tn = _largest_divisor(n, (1024, 512, 256, 128))
    return pl.pallas_call(
        _ada_kernel,
        out_shape=jax.ShapeDtypeStruct((depth, 1, n), F32),
        grid=(depth, n // tn),
        in_specs=[pl.BlockSpec((d, 1), lambda i, j: (0, 0)),
                  pl.BlockSpec((1, d, tn), lambda i, j: (i, 0, j)),
                  pl.BlockSpec((1, 1, tn), lambda i, j: (i, 0, j))],
        out_specs=pl.BlockSpec((1, 1, tn), lambda i, j: (i, 0, j)),
        scratch_shapes=[pltpu.VMEM((d, 1), F32)],
        compiler_params=_cparams(("arbitrary", "arbitrary")),
        name="ada_modulation",
    )(c.reshape(d, 1), w_ada, b_ada.reshape(depth, 1, n))


def _norm_kernel(*refs, has_res, mode, n_experts):
    refs = list(refs)
    x_ref = refs.pop(0)
    if has_res:
        y_ref, gt_ref = refs.pop(0), refs.pop(0)
    g_ref, sc_ref, sh_ref = refs.pop(0), refs.pop(0), refs.pop(0)
    if mode == "moe":
        wr_ref = refs.pop(0)
    if mode == "fox":
        wf_ref, bf_ref = refs.pop(0), refs.pop(0)
    if has_res:
        xo_ref = refs.pop(0)
    h_ref = refs.pop(0)

    x = x_ref[...]
    if has_res:
        x = x + gt_ref[...] * y_ref[...]
        xo_ref[...] = x
    ms = jnp.mean(x * x, axis=-1, keepdims=True)
    h = (x * lax.rsqrt(ms + RMS_EPS) * g_ref[...]) * (1.0 + sc_ref[...]) + sh_ref[...]
    h_ref[...] = h.astype(h_ref.dtype)

    if mode == "moe":
        idx_ref, wts_ref = refs.pop(0), refs.pop(0)
        logits = jnp.dot(h.astype(MXU_DTYPE), wr_ref[...], preferred_element_type=F32)
        lane = lax.broadcasted_iota(jnp.int32, logits.shape, 1)
        lg = jnp.where(lane < n_experts, logits, BELOW_NEG_INF)
        v0 = jnp.max(lg, axis=-1, keepdims=True)
        i0 = jnp.min(jnp.where(lg == v0, lane, LANES), axis=-1, keepdims=True)
        lg = jnp.where(lane == i0, BELOW_NEG_INF, lg)
        v1 = jnp.max(lg, axis=-1, keepdims=True)
        i1 = jnp.min(jnp.where(lg == v1, lane, LANES), axis=-1, keepdims=True)
        e1 = jnp.exp(v1 - v0)
        den = 1.0 + e1
        idx_ref[...] = jnp.where(lane == 0, i0, jnp.where(lane == 1, i1, 0))
        wts_ref[...] = jnp.where(lane == 0, 1.0 / den, jnp.where(lane == 1, e1 / den, 0.0))

    if mode == "fox":
        f_ref, carry_ref = refs.pop(0), refs.pop(0)

        @pl.when(pl.program_id(0) == 0)
        def _():
            carry_ref[...] = jnp.zeros_like(carry_ref)

        tm = x.shape[0]
        z = lax.dot_general(wf_ref[...], h.astype(MXU_DTYPE), (((1,), (1,)), ((), ())),
                            preferred_element_type=F32) + bf_ref[...]
        log_f = jnp.minimum(z, 0.0) - jnp.log1p(jnp.exp(-jnp.abs(z)))
        upper = (lax.broadcasted_iota(jnp.int32, (tm, tm), 0)
                 <= lax.broadcasted_iota(jnp.int32, (tm, tm), 1)).astype(F32)
        cum = jnp.dot(log_f, upper, preferred_element_type=F32,
                      precision=lax.Precision.HIGHEST) + carry_ref[...]
        f_ref[0] = cum * LOG2E
        carry_ref[...] = cum[:, tm - 1:tm]


def _norm_modulate(x, g, sc, sh, *, mode, res=None, w_router=None, w_f=None, b_f=None):
    s, d = x.shape
    tm = min(NORM_TILE, s)
    row = pl.BlockSpec((tm, d), lambda i: (i, 0))
    vec = pl.BlockSpec((1, d), lambda i: (0, 0))
    args, in_specs, out_shape, out_specs, scratch = [x], [row], [], [], []
    if res is not None:
        y, gt = res
        args += [y, gt]
        in_specs += [row, vec]
        out_shape.append(jax.ShapeDtypeStruct((s, d), F32))
        out_specs.append(row)
    args += [g, sc, sh]
    in_specs += [vec, vec, vec]
    h_dtype = {"attn": MXU_DTYPE, "fox": MXU_DTYPE, "dense": MXU_DTYPE,
               "moe": F32, "final": F32}[mode]
    out_shape.append(jax.ShapeDtypeStruct((s, d), h_dtype))
    out_specs.append(row)
    n_experts = 0
    if mode == "moe":
        n_experts = w_router.shape[1]
        wr = jnp.zeros((d, LANES), MXU_DTYPE).at[:, :n_experts].set(w_router.astype(MXU_DTYPE))
        args.append(wr)
        in_specs.append(pl.BlockSpec((d, LANES), lambda i: (0, 0)))
        lane_blk = pl.BlockSpec((tm, LANES), lambda i: (i, 0))
        out_shape += [jax.ShapeDtypeStruct((s, LANES), jnp.int32),
                      jax.ShapeDtypeStruct((s, LANES), F32)]
        out_specs += [lane_blk, lane_blk]
    if mode == "fox":
        nh = w_f.shape[1]
        args += [w_f.T.astype(MXU_DTYPE), b_f.reshape(nh, 1)]
        in_specs += [pl.BlockSpec((nh, d), lambda i: (0, 0)),
                     pl.BlockSpec((nh, 1), lambda i: (0, 0))]
        out_shape.append(jax.ShapeDtypeStruct((s // tm, nh, tm), F32))
        out_specs.append(pl.BlockSpec((1, nh, tm), lambda i: (i, 0, 0)))
        scratch.append(pltpu.VMEM((nh, 1), F32))
    outs = pl.pallas_call(
        functools.partial(_norm_kernel, has_res=res is not None, mode=mode, n_experts=n_experts),
        out_shape=out_shape,
        grid=(s // tm,),
        in_specs=in_specs,
        out_specs=out_specs,
        scratch_shapes=scratch,
        compiler_params=_cparams(("arbitrary",)),
        name=f"norm_{mode}",
    )(*args)
    outs = list(outs)
    result = {}
    if res is not None:
        result["x"] = outs.pop(0)
    result["h"] = outs.pop(0)
    if mode == "moe":
        result["idx"], result["wts"] = outs.pop(0), outs.pop(0)
    if mode == "fox":
        result["f"] = outs.pop(0)
    return result


def _qkv_kernel(a_ref, w_ref, cs_ref, o_ref, wb_ref):
    @pl.when(pl.program_id(1) == 0)
    def _():
        wb_ref[...] = w_ref[0].astype(wb_ref.dtype)

    acc = jnp.dot(a_ref[...], wb_ref[...], preferred_element_type=F32) * cs_ref[...]
    for j in range(o_ref.shape[0]):
        o_ref[j] = acc[:, j * HEAD_DIM:(j + 1) * HEAD_DIM].astype(o_ref.dtype)


def _qkv_projection(h, w, layer):
    s, d = h.shape
    n = w.shape[2]
    n_heads3 = n // HEAD_DIM
    tm = min(ROW_TILE, s)
    nh = _largest_divisor(n_heads3, (6, 4, 3, 2, 1))
    tn = nh * HEAD_DIM
    col_scale = jnp.where(jnp.arange(n) < n // 3, QK_SCALE_LOG2, 1.0).astype(F32).reshape(1, n)
    return pl.pallas_call(
        _qkv_kernel,
        out_shape=jax.ShapeDtypeStruct((n_heads3, s, HEAD_DIM), MXU_DTYPE),
        grid=(n // tn, s // tm),
        in_specs=[pl.BlockSpec((tm, d), lambda j, i: (i, 0)),
                  pl.BlockSpec((1, d, tn), lambda j, i: (layer, 0, j)),
                  pl.BlockSpec((1, tn), lambda j, i: (0, j))],
        out_specs=pl.BlockSpec((nh, tm, HEAD_DIM), lambda j, i: (j, i, 0)),
        scratch_shapes=[pltpu.VMEM((d, tn), MXU_DTYPE)],
        compiler_params=_cparams(("arbitrary", "arbitrary")),
        name="qkv_projection",
    )(h, w, col_scale)


def _out_proj_kernel(a_ref, w_ref, x_ref, gt_ref, o_ref, wb_ref):
    @pl.when(pl.program_id(1) == 0)
    def _():
        wb_ref[...] = w_ref[0].astype(wb_ref.dtype)

    acc = jnp.dot(a_ref[...], wb_ref[...], preferred_element_type=F32)
    o_ref[...] = x_ref[...] + gt_ref[...] * acc


def _out_projection_residual(a, w, layer, x, gt):
    s, k = a.shape
    n = w.shape[2]
    tm = min(ROW_TILE, s)
    tn = _largest_divisor(n, (1024, 512, 256, 128))
    return pl.pallas_call(
        _out_proj_kernel,
        out_shape=jax.ShapeDtypeStruct((s, n), F32),
        grid=(n // tn, s // tm),
        in_specs=[pl.BlockSpec((tm, k), lambda j, i: (i, 0)),
                  pl.BlockSpec((1, k, tn), lambda j, i: (layer, 0, j)),
                  pl.BlockSpec((tm, tn), lambda j, i: (i, j)),
                  pl.BlockSpec((1, tn), lambda j, i: (0, j))],
        out_specs=pl.BlockSpec((tm, tn), lambda j, i: (i, j)),
        scratch_shapes=[pltpu.VMEM((k, tn), MXU_DTYPE)],
        compiler_params=_cparams(("arbitrary", "arbitrary")),
        name="out_projection",
    )(a, w, x, gt)


def _lane_tiles(x):
    return [x[:, c * LANES:(c + 1) * LANES] for c in range(x.shape[1] // LANES)]


def _softmax_init(m_ref, acc_ref):
    m_ref[...] = jnp.full(m_ref.shape, BELOW_NEG_INF, F32)
    acc_ref[...] = jnp.zeros_like(acc_ref)


def _softmax_step(s, v_aug, m_ref, acc_ref):
    tiles = _lane_tiles(s)
    m_cur = jnp.max(functools.reduce(jnp.maximum, tiles), axis=-1, keepdims=True)
    m_old = m_ref[...]
    m_new = jnp.maximum(m_old, m_cur)
    p = jnp.concatenate([jnp.exp2(t - m_new) for t in tiles], axis=1).astype(MXU_DTYPE)
    pv = jnp.dot(p, v_aug, preferred_element_type=F32)
    alpha = jnp.exp2(m_old - m_new)
    acc_ref[...] = jnp.concatenate([alpha] * (acc_ref.shape[1] // LANES), axis=1) * acc_ref[...] + pv
    m_ref[...] = m_new


def _pipelined_tiles(n_plain, tail, qk, consume, s_ref, loop=True):
    s_ref[0] = qk(0)

    def step(j, cur, kind, prefetch=True):
        s = s_ref[cur]
        if prefetch:
            s_ref[1 - cur] = qk(j + 1)
        consume(j, s, kind)

    def pair(i, carry):
        step(2 * i, 0, None)
        step(2 * i + 1, 1, None)
        return carry

    def run_tail(cur):
        for k, kind in enumerate(tail):
            step(n_plain + k, cur, kind, prefetch=k + 1 < len(tail))
            cur = 1 - cur

    if not loop:
        run_tail(0)
        return
    lax.fori_loop(0, n_plain // 2, pair, 0)

    @pl.when(n_plain % 2 == 0)
    def _():
        run_tail(0)

    @pl.when(n_plain % 2 == 1)
    def _():
        step(n_plain - 1, 0, None)
        run_tail(1)


def _augment_values(v_ref, vaug_ref):
    hd = v_ref.shape[-1]
    for j in range(v_ref.shape[1]):
        vaug_ref[j, :, :hd] = v_ref[0, j]
        vaug_ref[j, :, hd:] = jnp.ones((v_ref.shape[2], hd), vaug_ref.dtype)


def _fox_kernel(q_ref, k_ref, v_ref, f_ref, o_ref, vaug_ref, s_ref, m_ref, acc_ref):
    i = pl.program_id(1)
    hd = q_ref.shape[2]

    @pl.when(i == 0)
    def _():
        _augment_values(v_ref, vaug_ref)

    q = q_ref[0]
    tq = q.shape[0]

    def qk(j):
        return lax.dot_general(q, k_ref[0, j], (((1,), (1,)), ((), ())),
                               preferred_element_type=F32)

    def consume(j, s, kind):
        s = s - f_ref[0, j]
        if kind == "diagonal":
            causal = (lax.broadcasted_iota(jnp.int32, (tq, tq), 1)
                      <= lax.broadcasted_iota(jnp.int32, (tq, tq), 0))
            s = jnp.where(causal, s, NEG_INF)
        _softmax_step(s, vaug_ref[j], m_ref, acc_ref)

    _softmax_init(m_ref, acc_ref)
    _pipelined_tiles(i, ["diagonal"], qk, consume, s_ref)

    acc = acc_ref[...]
    o_ref[...] = (acc[:, :hd] / acc[:, hd:]).astype(o_ref.dtype)


def _fox_attention(qkv, f_rows, n_heads):
    _, s, hd = qkv.shape
    t = min(ATTN_TILE, s)
    nb = s // t
    kv = qkv.reshape(3 * n_heads, nb, t, hd)
    return pl.pallas_call(
        _fox_kernel,
        out_shape=jax.ShapeDtypeStruct((s, n_heads * hd), MXU_DTYPE),
        grid=(n_heads, nb),
        in_specs=[pl.BlockSpec((1, t, hd), lambda h, i: (h, i, 0)),
                  pl.BlockSpec((1, nb, t, hd), lambda h, i: (n_heads + h, 0, 0, 0)),
                  pl.BlockSpec((1, nb, t, hd), lambda h, i: (2 * n_heads + h, 0, 0, 0)),
                  pl.BlockSpec((1, nb, 1, t), lambda h, i: (h, 0, 0, 0))],
        out_specs=pl.BlockSpec((t, hd), lambda h, i: (i, h)),
        scratch_shapes=[pltpu.VMEM((nb, t, 2 * hd), MXU_DTYPE),
                        pltpu.VMEM((2, t, t), F32),
                        pltpu.VMEM((t, LANES), F32),
                        pltpu.VMEM((t, 2 * hd), F32)],
        compiler_params=_cparams(("arbitrary", "arbitrary")),
        name="fox_attention",
    )(qkv, kv, kv, f_rows)


def _rel_bucket_np(dist):
    n = np.maximum(dist, 0)
    max_exact = REL_BUCKETS // 2
    nf = np.maximum(n, 1).astype(np.float32)
    large = max_exact + (np.log(nf / np.float32(max_exact))
                         / np.float32(math.log(REL_MAX_DIST / max_exact))
                         * np.float32(REL_BUCKETS - max_exact)).astype(np.int32)
    large = np.minimum(large, REL_BUCKETS - 1)
    return np.where(n < max_exact, n, large).astype(np.int32)


def _bias_table_kernel(rb_ref, idx_ref, o_ref, *, n_heads):
    h = pl.program_id(0)
    idx = idx_ref[0]
    bias = jnp.zeros(idx.shape, F32)
    far = rb_ref[(REL_BUCKETS - 1) * n_heads + h]
    for b in range(REL_BUCKETS):
        bias = jnp.where(idx == b, (rb_ref[b * n_heads + h] - far) * LOG2E, bias)
    o_ref[0, 0] = jnp.where(idx < 0, NEG_INF, bias)


def _moba_bias_tiles(rel_bias, t):
    n_heads = rel_bias.shape[1]
    dist = np.arange(2)[:, None, None] * t + np.arange(t)[None, :, None] - np.arange(t)[None, None, :]
    idx = jnp.asarray(np.where(dist < 0, -1, _rel_bucket_np(dist)).astype(np.int32))
    return pl.pallas_call(
        functools.partial(_bias_table_kernel, n_heads=n_heads),
        out_shape=jax.ShapeDtypeStruct((n_heads, 2, t, t), F32),
        grid_spec=pltpu.PrefetchScalarGridSpec(
            num_scalar_prefetch=1, grid=(n_heads, 2),
            in_specs=[pl.BlockSpec((1, t, t), lambda h, dt, rb: (dt, 0, 0))],
            out_specs=pl.BlockSpec((1, 1, t, t), lambda h, dt, rb: (h, dt, 0, 0))),
        compiler_params=_cparams(("arbitrary", "arbitrary")),
        name="moba_bias_tiles",
    )(rel_bias.reshape(-1), idx)


def _split3(x):
    hi = x.astype(MXU_DTYPE)
    r1 = x - hi.astype(F32)
    mid = r1.astype(MXU_DTYPE)
    lo = (r1 - mid.astype(F32)).astype(MXU_DTYPE)
    return hi, mid, lo


def _moba_kernel(q_ref, k_ref, v_ref, bias_ref, far_ref, o_ref,
                 vaug_ref, kaug_ref, qaug_ref, kmean_ref, selt_ref, s_ref, m_ref, acc_ref):
    t = pl.program_id(1)
    n_tiles, tq, hd = k_ref.shape[1], k_ref.shape[2], k_ref.shape[3]
    bpt = tq // MOBA_BLOCK
    n_blocks = n_tiles * bpt
    nbp = kmean_ref.shape[0]
    far_lane0 = LANES - 3

    @pl.when(t == 0)
    def _():
        _augment_values(v_ref, vaug_ref)
        kmean_ref[...] = jnp.zeros_like(kmean_ref)
        selt_ref[...] = jnp.zeros_like(selt_ref)
        lane = lax.broadcasted_iota(jnp.int32, (tq, LANES), 1)
        row_blk = lax.broadcasted_iota(jnp.int32, (tq, LANES), 0) // MOBA_BLOCK
        for j in range(n_tiles):
            kaug_ref[j, :, :hd] = k_ref[0, j]
            kaug_ref[j, :, hd:] = ((lane == j * bpt + row_blk) | (lane >= far_lane0)).astype(kaug_ref.dtype)
        for n in range(n_blocks):
            rows = slice((n % bpt) * MOBA_BLOCK, (n % bpt + 1) * MOBA_BLOCK)
            kmean_ref[n:n + 1, :] = jnp.mean(k_ref[0, n // bpt, rows, :].astype(F32), axis=0,
                                             keepdims=True)

    q = q_ref[0]

    gate = lax.dot_general(kmean_ref[...].astype(MXU_DTYPE), q, (((1,), (1,)), ((), ())),
                           preferred_element_type=F32)
    blk = lax.broadcasted_iota(jnp.int32, gate.shape, 0)
    own = t * bpt + lax.broadcasted_iota(jnp.int32, gate.shape, 1) // MOBA_BLOCK
    past = blk < own
    g = jnp.where(past, gate, NEG_INF)
    sel = jnp.where(blk == own, 0.0, NEG_INF)
    for _ in range(MOBA_TOPK):
        mx = jnp.max(g, axis=0, keepdims=True)
        pick = blk == jnp.min(jnp.where(g == mx, blk, nbp), axis=0, keepdims=True)
        sel = jnp.where(pick & past, 0.0, sel)
        g = jnp.where(pick, BELOW_NEG_INF, g)
    selt_ref[0:nbp, :] = sel
    far_hi, far_mid, far_lo = _split3(far_ref[0])
    lane = lax.broadcasted_iota(jnp.int32, (tq, LANES), 1)
    extra = selt_ref[...].T.astype(MXU_DTYPE)
    extra = jnp.where(lane == far_lane0, far_hi,
                      jnp.where(lane == far_lane0 + 1, far_mid,
                                jnp.where(lane == far_lane0 + 2, far_lo, extra)))
    qaug_ref[:, :hd] = q
    qaug_ref[:, hd:] = extra
    q_aug = qaug_ref[...]

    def qk(j):
        return lax.dot_general(q_aug, kaug_ref[j], (((1,), (1,)), ((), ())),
                               preferred_element_type=F32)

    def consume(j, s, kind):
        if kind == "own":
            s = s + bias_ref[0, 0]
        elif kind == "previous":
            s = s + bias_ref[0, 1]
        _softmax_step(s, vaug_ref[j], m_ref, acc_ref)

    _softmax_init(m_ref, acc_ref)

    @pl.when(t == 0)
    def _():
        _pipelined_tiles(0, ["own"], qk, consume, s_ref, loop=False)

    @pl.when(t >= 1)
    def _():
        _pipelined_tiles(t - 1, ["previous", "own"], qk, consume, s_ref)

    acc = acc_ref[...]
    o_ref[...] = (acc[:, :hd] / acc[:, hd:]).astype(o_ref.dtype)


def _moba_attention(qkv, bias_tiles, rel_bias, n_heads):
    _, s, hd = qkv.shape
    t = bias_tiles.shape[2]
    assert s % t == 0 and t % MOBA_BLOCK == 0 and t >= REL_MAX_DIST and hd == LANES
    assert s // MOBA_BLOCK <= LANES - 3
    n_tiles = s // t
    nbp = -(-(s // MOBA_BLOCK) // SUBLANES) * SUBLANES
    kv = qkv.reshape(3 * n_heads, n_tiles, t, hd)
    far = jnp.broadcast_to(rel_bias[REL_BUCKETS - 1][:, None, None] * LOG2E, (n_heads, 1, LANES))
    return pl.pallas_call(
        _moba_kernel,
        out_shape=jax.ShapeDtypeStruct((s, n_heads * hd), MXU_DTYPE),
        grid=(n_heads, n_tiles),
        in_specs=[pl.BlockSpec((1, t, hd), lambda h, i: (h, i, 0)),
                  pl.BlockSpec((1, n_tiles, t, hd), lambda h, i: (n_heads + h, 0, 0, 0)),
                  pl.BlockSpec((1, n_tiles, t, hd), lambda h, i: (2 * n_heads + h, 0, 0, 0)),
                  pl.BlockSpec((1, 2, t, t), lambda h, i: (h, 0, 0, 0)),
                  pl.BlockSpec((1, 1, LANES), lambda h, i: (h, 0, 0))],
        out_specs=pl.BlockSpec((t, hd), lambda h, i: (i, h)),
        scratch_shapes=[pltpu.VMEM((n_tiles, t, 2 * hd), MXU_DTYPE),
                        pltpu.VMEM((n_tiles, t, 2 * hd), MXU_DTYPE),
                        pltpu.VMEM((t, 2 * hd), MXU_DTYPE),
                        pltpu.VMEM((nbp, hd), F32),
                        pltpu.VMEM((LANES, t), F32),
                        pltpu.VMEM((2, t, t), F32),
                        pltpu.VMEM((t, LANES), F32),
                        pltpu.VMEM((t, 2 * hd), F32)],
        compiler_params=_cparams(("arbitrary", "arbitrary")),
        name="moba_attention",
    )(qkv, kv, kv, bias_tiles, far)


def _ffn_kernel(te_ref, tr_ref, x_ref, w1_ref, w3_ref, w2_ref, o_ref, w1b_ref, w3b_ref, w2b_ref, g_ref):
    t = pl.program_id(0)
    f = pl.program_id(1)
    rows = tr_ref[t]
    tm = x_ref.shape[0]
    sub = min(FFN_SUB_ROWS, tm)

    def gate_up(r):
        x = x_ref[r, :]
        a = jnp.dot(x, w1b_ref[...], preferred_element_type=F32)
        b = jnp.dot(x, w3b_ref[...], preferred_element_type=F32)
        return (a * jax.nn.sigmoid(a) * b).astype(MXU_DTYPE)

    def down(g, r):
        o_ref[r, :] += jnp.dot(g, w2b_ref[...], preferred_element_type=F32)

    def cast_gate_up_weights():
        w1b_ref[...] = w1_ref[0].astype(w1b_ref.dtype)
        w3b_ref[...] = w3_ref[0].astype(w3b_ref.dtype)

    def cast_down_weights():
        w2b_ref[...] = w2_ref[0].astype(w2b_ref.dtype)

    @pl.when(f == 0)
    def _():
        o_ref[...] = jnp.zeros_like(o_ref)

    @pl.when(rows == tm)
    def _():
        cast_gate_up_weights()
        cast_down_weights()
        for c in range(tm // sub):
            r = pl.ds(c * sub, sub)
            down(gate_up(r), r)

    @pl.when((rows > 0) & (rows < tm))
    def _():
        def chunk(c):
            return pl.ds(pl.multiple_of(c * sub, sub), sub)

        n_chunks = (rows + sub - 1) // sub
        cast_gate_up_weights()
        g_ref[0] = gate_up(chunk(0))
        cast_down_weights()

        def body(c, carry):
            down(g_ref[(c - 1) % 2], chunk(c - 1))
            g_ref[c % 2] = gate_up(chunk(c))
            return carry

        lax.fori_loop(1, n_chunks, body, 0)
        down(g_ref[(n_chunks - 1) % 2], chunk(n_chunks - 1))


def _swiglu_ffn(xs, w1, w3, w2, tile_expert, tile_rows, tm):
    r, d = xs.shape
    ff = w1.shape[2]
    tf = _largest_divisor(ff, (FFN_F_TILE, 128))
    nf = ff // tf

    def f_eff(t, f, tr):
        return jnp.where(tr[t] > 0, f, nf - 1)

    return pl.pallas_call(
        _ffn_kernel,
        out_shape=jax.ShapeDtypeStruct((r, d), F32),
        grid_spec=pltpu.PrefetchScalarGridSpec(
            num_scalar_prefetch=2, grid=(r // tm, nf),
            in_specs=[pl.BlockSpec((tm, d), lambda t, f, te, tr: (t, 0)),
                      pl.BlockSpec((1, d, tf), lambda t, f, te, tr: (te[t], 0, f_eff(t, f, tr))),
                      pl.BlockSpec((1, d, tf), lambda t, f, te, tr: (te[t], 0, f_eff(t, f, tr))),
                      pl.BlockSpec((1, tf, d), lambda t, f, te, tr: (te[t], f_eff(t, f, tr), 0))],
            out_specs=pl.BlockSpec((tm, d), lambda t, f, te, tr: (t, 0)),
            scratch_shapes=[pltpu.VMEM((d, tf), MXU_DTYPE), pltpu.VMEM((d, tf), MXU_DTYPE),
                            pltpu.VMEM((tf, d), MXU_DTYPE),
                            pltpu.VMEM((2, min(FFN_SUB_ROWS, tm), tf), MXU_DTYPE)]),
        compiler_params=_cparams(("arbitrary", "arbitrary")),
        name="swiglu_ffn",
    )(tile_expert, tile_rows, xs, w1, w3, w2)


def _gather_kernel(tok_ref, nrows_ref, h_ref, o_ref, buf_ref, sem):
    t = pl.program_id(0)
    n_chunks = pl.num_programs(0)
    sub = o_ref.shape[0]
    slot = t % 2

    def request(chunk, into):
        def start(r, carry):
            tok = tok_ref[chunk * sub + r]
            pltpu.make_async_copy(h_ref.at[pl.ds(tok, 1)], buf_ref.at[into, pl.ds(r, 1)],
                                  sem.at[into]).start()
            return carry

        lax.fori_loop(0, sub, start, 0, unroll=DMA_ISSUE_UNROLL)

    @pl.when((t == 0) & (nrows_ref[0] > 0))
    def _():
        request(0, 0)

    nxt = jnp.minimum(t + 1, n_chunks - 1)

    @pl.when((t + 1 < n_chunks) & (nrows_ref[nxt] > 0))
    def _():
        request(nxt, 1 - slot)

    @pl.when(nrows_ref[t] > 0)
    def _():
        pltpu.make_async_copy(h_ref.at[pl.ds(0, sub)], buf_ref.at[slot], sem.at[slot]).wait()
        o_ref[...] = buf_ref[slot].astype(o_ref.dtype)

    @pl.when(nrows_ref[t] == 0)
    def _():
        o_ref[...] = jnp.zeros_like(o_ref)


def _gather_rows(h, row_token, sub_rows, sub):
    s, d = h.shape
    r = row_token.shape[0]
    return pl.pallas_call(
        _gather_kernel,
        out_shape=jax.ShapeDtypeStruct((r, d), MXU_DTYPE),
        grid_spec=pltpu.PrefetchScalarGridSpec(
            num_scalar_prefetch=2, grid=(r // sub,),
            in_specs=[pl.BlockSpec(memory_space=pl.ANY)],
            out_specs=pl.BlockSpec((sub, d), lambda t, tok, nr: (t, 0)),
            scratch_shapes=[pltpu.VMEM((2, sub, d), F32), pltpu.SemaphoreType.DMA((2,))]),
        compiler_params=_cparams(("arbitrary",)),
        name="moe_gather",
    )(row_token, sub_rows, h)


def _combine_kernel(pos_ref, x_ref, wts_ref, gt_ref, ys_ref, o_ref, buf_ref, sem):
    t = pl.program_id(0)
    n_tiles = pl.num_programs(0)
    tm = x_ref.shape[0]
    slot = t % 2

    def request(tile, into):
        def start(r, carry):
            for k in range(TOP_K):
                src = pos_ref[(tile * tm + r) * TOP_K + k]
                pltpu.make_async_copy(ys_ref.at[pl.ds(src, 1)], buf_ref.at[into, k, pl.ds(r, 1)],
                                      sem.at[into]).start()
            return carry

        lax.fori_loop(0, tm, start, 0, unroll=DMA_ISSUE_UNROLL)

    @pl.when(t == 0)
    def _():
        request(0, 0)

    @pl.when(t + 1 < n_tiles)
    def _():
        request(t + 1, 1 - slot)

    for k in range(TOP_K):
        pltpu.make_async_copy(ys_ref.at[pl.ds(0, tm)], buf_ref.at[slot, k], sem.at[slot]).wait()
    wts = wts_ref[...]
    y = wts[:, 0:1] * buf_ref[slot, 0]
    for k in range(1, TOP_K):
        y = y + wts[:, k:k + 1] * buf_ref[slot, k]
    o_ref[...] = x_ref[...] + gt_ref[...] * y


def _combine_residual(x, wts, gt, ys, pos):
    s, d = x.shape
    tm = min(COMBINE_TILE, s)
    return pl.pallas_call(
        _combine_kernel,
        out_shape=jax.ShapeDtypeStruct((s, d), F32),
        grid_spec=pltpu.PrefetchScalarGridSpec(
            num_scalar_prefetch=1, grid=(s // tm,),
            in_specs=[pl.BlockSpec((tm, d), lambda t, pos: (t, 0)),
                      pl.BlockSpec((tm, LANES), lambda t, pos: (t, 0)),
                      pl.BlockSpec((1, d), lambda t, pos: (0, 0)),
                      pl.BlockSpec(memory_space=pl.ANY)],
            out_specs=pl.BlockSpec((tm, d), lambda t, pos: (t, 0)),
            scratch_shapes=[pltpu.VMEM((2, TOP_K, tm, d), F32), pltpu.SemaphoreType.DMA((2,))]),
        compiler_params=_cparams(("arbitrary",)),
        name="moe_combine",
    )(pos, x, wts, gt, ys)


def _moe_plan(idx, n_experts, tm, sub):
    s = idx.shape[0]
    n_slots = s * TOP_K
    e_flat = idx.reshape(-1)
    onehot = (e_flat[:, None] == jnp.arange(n_experts, dtype=jnp.int32)[None, :]).astype(jnp.int32)
    cnt = jnp.sum(onehot, axis=0)
    rank = jnp.sum((jnp.cumsum(onehot, axis=0) - onehot) * onehot, axis=1)
    ntile_e = (cnt + tm - 1) // tm
    tile_end = jnp.cumsum(ntile_e)
    tile_start = tile_end - ntile_e
    pos = tile_start[e_flat] * tm + rank
    n_tiles = n_slots // tm + n_experts
    row_token = jnp.zeros((n_tiles * tm,), jnp.int32).at[pos].set(
        jnp.arange(n_slots, dtype=jnp.int32) // TOP_K)
    t_ids = jnp.arange(n_tiles, dtype=jnp.int32)
    te = jnp.sum((t_ids[:, None] >= tile_end[None, :]).astype(jnp.int32), axis=1)
    used = te < n_experts
    te_c = jnp.minimum(te, n_experts - 1)
    rows = jnp.where(used, jnp.clip(cnt[te_c] - (t_ids - tile_start[te_c]) * tm, 0, tm), 0)
    e_last = jnp.max(jnp.where(ntile_e > 0, jnp.arange(n_experts, dtype=jnp.int32), 0))
    tile_expert = jnp.where(used, te_c, e_last).astype(jnp.int32)
    per = tm // sub
    s_ids = jnp.arange(n_tiles * per, dtype=jnp.int32)
    sub_rows = jnp.clip(rows[s_ids // per] - (s_ids % per) * sub, 0, sub)
    return pos.astype(jnp.int32), row_token, tile_expert, rows.astype(jnp.int32), sub_rows.astype(jnp.int32)


def kernel(x, c, w_ada, b_ada, g_mix, g_ffn, g_final, rel_bias, w_qkv_fox, w_f_fox, b_f_fox, w_o_fox, w_qkv_moba, w_o_moba, w1_dense, w3_dense, w2_dense, w_router, w1_moe, w3_moe, w2_moe):
    b, s, d = x.shape
    assert b == 1 and d % HEAD_DIM == 0
    depth = w_ada.shape[0]
    n_heads = d // HEAD_DIM
    n_experts = w_router.shape[2]
    tm = min(ROW_TILE, s)
    sub = min(FFN_SUB_ROWS, tm)

    mod = _ada_modulation(c, w_ada, b_ada)
    bias_tiles = _moba_bias_tiles(rel_bias, min(ATTN_TILE, s))
    dense_rows = jnp.full((s // tm,), tm, jnp.int32)
    moe_w13_shape = (-1,) + w1_moe.shape[2:]
    moe_w2_shape = (-1,) + w2_moe.shape[2:]

    xc = x.reshape(s, d)
    pending = None
    for i in range(depth):
        j = i // 2
        sh1, sc1, gt1, sh2, sc2, gt2 = [mod[i, :, k * d:(k + 1) * d] for k in range(6)]
        g1 = g_mix[i].reshape(1, d)
        g2 = g_ffn[i].reshape(1, d)

        if i % 2 == 0:
            r = _norm_modulate(xc, g1, sc1, sh1, mode="fox", res=pending,
                               w_f=w_f_fox[j], b_f=b_f_fox[j])
        else:
            r = _norm_modulate(xc, g1, sc1, sh1, mode="attn", res=pending)
        xc = r.get("x", xc)
        pending = None
        if i % 2 == 0:
            qkv = _qkv_projection(r["h"], w_qkv_fox, j)
            nt = r["f"].shape[0]
            f_rows = r["f"].transpose(1, 0, 2).reshape(n_heads, nt, 1, r["f"].shape[2])
            o = _fox_attention(qkv, f_rows, n_heads)
            xc = _out_projection_residual(o, w_o_fox, j, xc, gt1)
        else:
            qkv = _qkv_projection(r["h"], w_qkv_moba, j)
            o = _moba_attention(qkv, bias_tiles, rel_bias, n_heads)
            xc = _out_projection_residual(o, w_o_moba, j, xc, gt1)

        if i % 2 == 0:
            r = _norm_modulate(xc, g2, sc2, sh2, mode="dense")
            y = _swiglu_ffn(r["h"], w1_dense, w3_dense, w2_dense,
                            jnp.full((s // tm,), j, jnp.int32), dense_rows, tm)
            pending = (y, gt2)
        else:
            r = _norm_modulate(xc, g2, sc2, sh2, mode="moe", w_router=w_router[j])
            pos, row_token, tile_expert, tile_rows, sub_rows = _moe_plan(
                r["idx"][:, :TOP_K], n_experts, tm, sub)
            xs = _gather_rows(r["h"], row_token, sub_rows, sub)
            ys = _swiglu_ffn(xs, w1_moe.reshape(moe_w13_shape), w3_moe.reshape(moe_w13_shape),
                             w2_moe.reshape(moe_w2_shape), tile_expert + j * n_experts, tile_rows, tm)
            xc = _combine_residual(xc, r["wts"], gt2, ys, pos)

    zero = jnp.zeros((1, d), F32)
    r = _norm_modulate(xc, g_final.reshape(1, d), zero, zero, mode="final", res=pending)
    return r["h"].reshape(b, s, d)
```

```python
import functools
import math

import numpy as np
import jax
import jax.numpy as jnp
from jax import lax
from jax.experimental import pallas as pl
from jax.experimental.pallas import tpu as pltpu

HEAD_DIM = 128
MOBA_BLOCK = 256
MOBA_TOPK = 3
REL_BUCKETS = 32
REL_MAX_DIST = 128
TOP_K = 2
RMS_EPS = 1e-6
NEG_INF = -1e30
BELOW_NEG_INF = -3e38
LOG2E = math.log2(math.e)
QK_SCALE_LOG2 = HEAD_DIM ** -0.5 * LOG2E

LANES = 128
SUBLANES = 8
VMEM_LIMIT_BYTES = 56 * 1024 * 1024

MXU_DTYPE = jnp.bfloat16
F32 = jnp.float32

ROW_TILE = 1024
FFN_F_TILE = 256
FFN_SUB_ROWS = 256
ATTN_TILE = 512
NORM_TILE = 512
COMBINE_TILE = 256
DMA_ISSUE_UNROLL = 8


def _cparams(semantics):
    return pltpu.CompilerParams(dimension_semantics=semantics,
                                vmem_limit_bytes=VMEM_LIMIT_BYTES)


def _largest_divisor(n, candidates):
    for c in candidates:
        if n % c == 0:
            return c
    raise ValueError(f"no tile in {candidates} divides {n}")


def _ada_kernel(c_ref, w_ref, b_ref, o_ref, ca_ref):
    c = c_ref[...]
    ca_ref[...] = c * jax.nn.sigmoid(c)
    d, tn = w_ref.shape[1], w_ref.shape[2]
    ch = min(d, 256)

    def body(r, acc):
        rows = pl.ds(pl.multiple_of(r * ch, ch), ch)
        prod = w_ref[0, rows, :] * ca_ref[rows, :]
        return acc + jnp.sum(prod.reshape(ch // SUBLANES, SUBLANES, tn), axis=0)

    acc = lax.fori_loop(0, d // ch, body, jnp.zeros((SUBLANES, tn), F32))
    o_ref[0] = jnp.sum(acc, axis=0, keepdims=True) + b_ref[0]


def _ada_modulation(c, w_ada, b_ada):
    depth, d, n = w_ada.shape
    tn = _largest_divisor(n, (1024, 512, 256, 128))
    return pl.pallas_call(
        _ada_kernel,
        out_shape=jax.ShapeDtypeStruct((depth, 1, n), F32),
        grid=(depth, n // tn),
        in_specs=[pl.BlockSpec((d, 1), lambda i, j: (0, 0)),
                  pl.BlockSpec((1, d, tn), lambda i, j: (i, 0, j)),
                  pl.BlockSpec((1, 1, tn), lambda i, j: (i, 0, j))],
        out_specs=pl.BlockSpec((1, 1, tn), lambda i, j: (i, 0, j)),
        scratch_shapes=[pltpu.VMEM((d, 1), F32)],
        compiler_params=_cparams(("arbitrary", "arbitrary")),
        name="ada_modulation",
    )(c.reshape(d, 1), w_ada, b_ada.reshape(depth, 1, n))


def _norm_kernel(*refs, has_res, mode, n_experts):
    refs = list(refs)
    x_ref = refs.pop(0)
    if has_res:
        y_ref, gt_ref = refs.pop(0), refs.pop(0)
    g_ref, sc_ref, sh_ref = refs.pop(0), refs.pop(0), refs.pop(0)
    if mode == "moe":
        wr_ref = refs.pop(0)
    if mode == "fox":
        wf_ref, bf_ref = refs.pop(0), refs.pop(0)
    if has_res:
        xo_ref = refs.pop(0)
    h_ref = refs.pop(0)

    x = x_ref[...]
    if has_res:
        x = x + gt_ref[...] * y_ref[...]
        xo_ref[...] = x
    ms = jnp.mean(x * x, axis=-1, keepdims=True)
    h = (x * lax.rsqrt(ms + RMS_EPS) * g_ref[...]) * (1.0 + sc_ref[...]) + sh_ref[...]
    h_ref[...] = h.astype(h_ref.dtype)

    if mode == "moe":
        idx_ref, wts_ref = refs.pop(0), refs.pop(0)
        logits = jnp.dot(h.astype(MXU_DTYPE), wr_ref[...], preferred_element_type=F32)
        lane = lax.broadcasted_iota(jnp.int32, logits.shape, 1)
        lg = jnp.where(lane < n_experts, logits, BELOW_NEG_INF)
        v0 = jnp.max(lg, axis=-1, keepdims=True)
        i0 = jnp.min(jnp.where(lg == v0, lane, LANES), axis=-1, keepdims=True)
        lg = jnp.where(lane == i0, BELOW_NEG_INF, lg)
        v1 = jnp.max(lg, axis=-1, keepdims=True)
        i1 = jnp.min(jnp.where(lg == v1, lane, LANES), axis=-1, keepdims=True)
        e1 = jnp.exp(v1 - v0)
        den = 1.0 + e1
        idx_ref[...] = jnp.where(lane == 0, i0, jnp.where(lane == 1, i1, 0))
        wts_ref[...] = jnp.where(lane == 0, 1.0 / den, jnp.where(lane == 1, e1 / den, 0.0))

    if mode == "fox":
        f_ref, carry_ref = refs.pop(0), refs.pop(0)

        @pl.when(pl.program_id(0) == 0)
        def _():
            carry_ref[...] = jnp.zeros_like(carry_ref)

        tm = x.shape[0]
        z = lax.dot_general(wf_ref[...], h.astype(MXU_DTYPE), (((1,), (1,)), ((), ())),
                            preferred_element_type=F32) + bf_ref[...]
        log_f = jnp.minimum(z, 0.0) - jnp.log1p(jnp.exp(-jnp.abs(z)))
        upper = (lax.broadcasted_iota(jnp.int32, (tm, tm), 0)
                 <= lax.broadcasted_iota(jnp.int32, (tm, tm), 1)).astype(F32)
        cum = jnp.dot(log_f, upper, preferred_element_type=F32,
                      precision=lax.Precision.HIGHEST) + carry_ref[...]
        f_ref[0] = cum * LOG2E
        carry_ref[...] = cum[:, tm - 1:tm]


def _norm_modulate(x, g, sc, sh, *, mode, res=None, w_router=None, w_f=None, b_f=None):
    s, d = x.shape
    tm = min(NORM_TILE, s)
    row = pl.BlockSpec((tm, d), lambda i: (i, 0))
    vec = pl.BlockSpec((1, d), lambda i: (0, 0))
    args, in_specs, out_shape, out_specs, scratch = [x], [row], [], [], []
    if res is not None:
        y, gt = res
        args += [y, gt]
        in_specs += [row, vec]
        out_shape.append(jax.ShapeDtypeStruct((s, d), F32))
        out_specs.append(row)
    args += [g, sc, sh]
    in_specs += [vec, vec, vec]
    h_dtype = {"attn": MXU_DTYPE, "fox": MXU_DTYPE, "dense": MXU_DTYPE,
               "moe": F32, "final": F32}[mode]
    out_shape.append(jax.ShapeDtypeStruct((s, d), h_dtype))
    out_specs.append(row)
    n_experts = 0
    if mode == "moe":
        n_experts = w_router.shape[1]
        wr = jnp.zeros((d, LANES), MXU_DTYPE).at[:, :n_experts].set(w_router.astype(MXU_DTYPE))
        args.append(wr)
        in_specs.append(pl.BlockSpec((d, LANES), lambda i: (0, 0)))
        lane_blk = pl.BlockSpec((tm, LANES), lambda i: (i, 0))
        out_shape += [jax.ShapeDtypeStruct((s, LANES), jnp.int32),
                      jax.ShapeDtypeStruct((s, LANES), F32)]
        out_specs += [lane_blk, lane_blk]
    if mode == "fox":
        nh = w_f.shape[1]
        args += [w_f.T.astype(MXU_DTYPE), b_f.reshape(nh, 1)]
        in_specs += [pl.BlockSpec((nh, d), lambda i: (0, 0)),
                     pl.BlockSpec((nh, 1), lambda i: (0, 0))]
        out_shape.append(jax.ShapeDtypeStruct((s // tm, nh, tm), F32))
        out_specs.append(pl.BlockSpec((1, nh, tm), lambda i: (i, 0, 0)))
        scratch.append(pltpu.VMEM((nh, 1), F32))
    outs = pl.pallas_call(
        functools.partial(_norm_kernel, has_res=res is not None, mode=mode, n_experts=n_experts),
        out_shape=out_shape,
        grid=(s // tm,),
        in_specs=in_specs,
        out_specs=out_specs,
        scratch_shapes=scratch,
        compiler_params=_cparams(("arbitrary",)),
        name=f"norm_{mode}",
    )(*args)
    outs = list(outs)
    result = {}
    if res is not None:
        result["x"] = outs.pop(0)
    result["h"] = outs.pop(0)
    if mode == "moe":
        result["idx"], result["wts"] = outs.pop(0), outs.pop(0)
    if mode == "fox":
        result["f"] = outs.pop(0)
    return result


def _qkv_kernel(a_ref, w_ref, cs_ref, o_ref, wb_ref):
    @pl.when(pl.program_id(1) == 0)
    def _():
        wb_ref[...] = w_ref[0].astype(wb_ref.dtype)

    acc = jnp.dot(a_ref[...], wb_ref[...], preferred_element_type=F32) * cs_ref[...]
    for j in range(o_ref.shape[0]):
        o_ref[j] = acc[:, j * HEAD_DIM:(j + 1) * HEAD_DIM].astype(o_ref.dtype)


def _qkv_projection(h, w, layer):
    s, d = h.shape
    n = w.shape[2]
    n_heads3 = n // HEAD_DIM
    tm = min(ROW_TILE, s)
    nh = _largest_divisor(n_heads3, (6, 4, 3, 2, 1))
    tn = nh * HEAD_DIM
    col_scale = jnp.where(jnp.arange(n) < n // 3, QK_SCALE_LOG2, 1.0).astype(F32).reshape(1, n)
    return pl.pallas_call(
        _qkv_kernel,
        out_shape=jax.ShapeDtypeStruct((n_heads3, s, HEAD_DIM), MXU_DTYPE),
        grid=(n // tn, s // tm),
        in_specs=[pl.BlockSpec((tm, d), lambda j, i: (i, 0)),
                  pl.BlockSpec((1, d, tn), lambda j, i: (layer, 0, j)),
                  pl.BlockSpec((1, tn), lambda j, i: (0, j))],
        out_specs=pl.BlockSpec((nh, tm, HEAD_DIM), lambda j, i: (j, i, 0)),
        scratch_shapes=[pltpu.VMEM((d, tn), MXU_DTYPE)],
        compiler_params=_cparams(("arbitrary", "arbitrary")),
        name="qkv_projection",
    )(h, w, col_scale)


def _out_proj_kernel(a_ref, w_ref, x_ref, gt_ref, o_ref, wb_ref):
    @pl.when(pl.program_id(1) == 0)
    def _():
        wb_ref[...] = w_ref[0].astype(wb_ref.dtype)

    acc = jnp.dot(a_ref[...], wb_ref[...], preferred_element_type=F32)
    o_ref[...] = x_ref[...] + gt_ref[...] * acc


def _out_projection_residual(a, w, layer, x, gt):
    s, k = a.shape
    n = w.shape[2]
    tm = min(ROW_TILE, s)
    tn = _largest_divisor(n, (1024, 512, 256, 128))
    return pl.pallas_call(
        _out_proj_kernel,
        out_shape=jax.ShapeDtypeStruct((s, n), F32),
        grid=(n // tn, s // tm),
        in_specs=[pl.BlockSpec((tm, k), lambda j, i: (i, 0)),
                  pl.BlockSpec((1, k, tn), lambda j, i: (layer, 0, j)),
                  pl.BlockSpec((tm, tn), lambda j, i: (i, j)),
                  pl.BlockSpec((1, tn), lambda j, i: (0, j))],
        out_specs=pl.BlockSpec((tm, tn), lambda j, i: (i, j)),
        scratch_shapes=[pltpu.VMEM((k, tn), MXU_DTYPE)],
        compiler_params=_cparams(("arbitrary", "arbitrary")),
        name="out_projection",
    )(a, w, x, gt)


def _lane_tiles(x):
    return [x[:, c * LANES:(c + 1) * LANES] for c in range(x.shape[1] // LANES)]


def _softmax_init(m_ref, acc_ref):
    m_ref[...] = jnp.full(m_ref.shape, BELOW_NEG_INF, F32)
    acc_ref[...] = jnp.zeros_like(acc_ref)


def _softmax_step(s, v_aug, m_ref, acc_ref):
    tiles = _lane_tiles(s)
    m_cur = jnp.max(functools.reduce(jnp.maximum, tiles), axis=-1, keepdims=True)
    m_old = m_ref[...]
    m_new = jnp.maximum(m_old, m_cur)
    p = jnp.concatenate([jnp.exp2(t - m_new) for t in tiles], axis=1).astype(MXU_DTYPE)
    pv = jnp.dot(p, v_aug, preferred_element_type=F32)
    alpha = jnp.exp2(m_old - m_new)
    acc_ref[...] = jnp.concatenate([alpha] * (acc_ref.shape[1] // LANES), axis=1) * acc_ref[...] + pv
    m_ref[...] = m_new


def _pipelined_tiles(n_plain, tail, qk, consume, s_ref, loop=True):
    s_ref[0] = qk(0)

    def step(j, cur, kind, prefetch=True):
        s = s_ref[cur]
        if prefetch:
            s_ref[1 - cur] = qk(j + 1)
        consume(j, s, kind)

    def pair(i, carry):
        step(2 * i, 0, None)
        step(2 * i + 1, 1, None)
        return carry

    def run_tail(cur):
        for k, kind in enumerate(tail):
            step(n_plain + k, cur, kind, prefetch=k + 1 < len(tail))
            cur = 1 - cur

    if not loop:
        run_tail(0)
        return
    lax.fori_loop(0, n_plain // 2, pair, 0)

    @pl.when(n_plain % 2 == 0)
    def _():
        run_tail(0)

    @pl.when(n_plain % 2 == 1)
    def _():
        step(n_plain - 1, 0, None)
        run_tail(1)


def _augment_values(v_ref, vaug_ref):
    hd = v_ref.shape[-1]
    for j in range(v_ref.shape[1]):
        vaug_ref[j, :, :hd] = v_ref[0, j]
        vaug_ref[j, :, hd:] = jnp.ones((v_ref.shape[2], hd), vaug_ref.dtype)


def _fox_kernel(q_ref, k_ref, v_ref, f_ref, o_ref, vaug_ref, s_ref, m_ref, acc_ref):
    i = pl.program_id(1)
    hd = q_ref.shape[2]

    @pl.when(i == 0)
    def _():
        _augment_values(v_ref, vaug_ref)

    q = q_ref[0]
    tq = q.shape[0]

    def qk(j):
        return lax.dot_general(q, k_ref[0, j], (((1,), (1,)), ((), ())),
                               preferred_element_type=F32)

    def consume(j, s, kind):
        s = s - f_ref[0, j]
        if kind == "diagonal":
            causal = (lax.broadcasted_iota(jnp.int32, (tq, tq), 1)
                      <= lax.broadcasted_iota(jnp.int32, (tq, tq), 0))
            s = jnp.where(causal, s, NEG_INF)
        _softmax_step(s, vaug_ref[j], m_ref, acc_ref)

    _softmax_init(m_ref, acc_ref)
    _pipelined_tiles(i, ["diagonal"], qk, consume, s_ref)

    acc = acc_ref[...]
    o_ref[...] = (acc[:, :hd] / acc[:, hd:]).astype(o_ref.dtype)


def _fox_attention(qkv, f_rows, n_heads):
    _, s, hd = qkv.shape
    t = min(ATTN_TILE, s)
    nb = s // t
    kv = qkv.reshape(3 * n_heads, nb, t, hd)
    return pl.pallas_call(
        _fox_kernel,
        out_shape=jax.ShapeDtypeStruct((s, n_heads * hd), MXU_DTYPE),
        grid=(n_heads, nb),
        in_specs=[pl.BlockSpec((1, t, hd), lambda h, i: (h, i, 0)),
                  pl.BlockSpec((1, nb, t, hd), lambda h, i: (n_heads + h, 0, 0, 0)),
                  pl.BlockSpec((1, nb, t, hd), lambda h, i: (2 * n_heads + h, 0, 0, 0)),
                  pl.BlockSpec((1, nb, 1, t), lambda h, i: (h, 0, 0, 0))],
        out_specs=pl.BlockSpec((t, hd), lambda h, i: (i, h)),
        scratch_shapes=[pltpu.VMEM((nb, t, 2 * hd), MXU_DTYPE),
                        pltpu.VMEM((2, t, t), F32),
                        pltpu.VMEM((t, LANES), F32),
                        pltpu.VMEM((t, 2 * hd), F32)],
        compiler_params=_cparams(("arbitrary", "arbitrary")),
        name="fox_attention",
    )(qkv, kv, kv, f_rows)


def _rel_bucket_np(dist):
    n = np.maximum(dist, 0)
    max_exact = REL_BUCKETS // 2
    nf = np.maximum(n, 1).astype(np.float32)
    large = max_exact + (np.log(nf / np.float32(max_exact))
                         / np.float32(math.log(REL_MAX_DIST / max_exact))
                         * np.float32(REL_BUCKETS - max_exact)).astype(np.int32)
    large = np.minimum(large, REL_BUCKETS - 1)
    return np.where(n < max_exact, n, large).astype(np.int32)


def _bias_table_kernel(rb_ref, idx_ref, o_ref, *, n_heads):
    h = pl.program_id(0)
    idx = idx_ref[0]
    bias = jnp.zeros(idx.shape, F32)
    far = rb_ref[(REL_BUCKETS - 1) * n_heads + h]
    for b in range(REL_BUCKETS):
        bias = jnp.where(idx == b, (rb_ref[b * n_heads + h] - far) * LOG2E, bias)
    o_ref[0, 0] = jnp.where(idx < 0, NEG_INF, bias)


def _moba_bias_tiles(rel_bias, t):
    n_heads = rel_bias.shape[1]
    dist = np.arange(2)[:, None, None] * t + np.arange(t)[None, :, None] - np.arange(t)[None, None, :]
    idx = jnp.asarray(np.where(dist < 0, -1, _rel_bucket_np(dist)).astype(np.int32))
    return pl.pallas_call(
        functools.partial(_bias_table_kernel, n_heads=n_heads),
        out_shape=jax.ShapeDtypeStruct((n_heads, 2, t, t), F32),
        grid_spec=pltpu.PrefetchScalarGridSpec(
            num_scalar_prefetch=1, grid=(n_heads, 2),
            in_specs=[pl.BlockSpec((1, t, t), lambda h, dt, rb: (dt, 0, 0))],
            out_specs=pl.BlockSpec((1, 1, t, t), lambda h, dt, rb: (h, dt, 0, 0))),
        compiler_params=_cparams(("arbitrary", "arbitrary")),
        name="moba_bias_tiles",
    )(rel_bias.reshape(-1), idx)


def _split3(x):
    hi = x.astype(MXU_DTYPE)
    r1 = x - hi.astype(F32)
    mid = r1.astype(MXU_DTYPE)
    lo = (r1 - mid.astype(F32)).astype(MXU_DTYPE)
    return hi, mid, lo


def _moba_kernel(q_ref, k_ref, v_ref, bias_ref, far_ref, o_ref,
                 vaug_ref, kaug_ref, qaug_ref, kmean_ref, selt_ref, s_ref, m_ref, acc_ref):
    t = pl.program_id(1)
    n_tiles, tq, hd = k_ref.shape[1], k_ref.shape[2], k_ref.shape[3]
    bpt = tq // MOBA_BLOCK
    n_blocks = n_tiles * bpt
    nbp = kmean_ref.shape[0]
    far_lane0 = LANES - 3

    @pl.when(t == 0)
    def _():
        _augment_values(v_ref, vaug_ref)
        kmean_ref[...] = jnp.zeros_like(kmean_ref)
        selt_ref[...] = jnp.zeros_like(selt_ref)
        lane = lax.broadcasted_iota(jnp.int32, (tq, LANES), 1)
        row_blk = lax.broadcasted_iota(jnp.int32, (tq, LANES), 0) // MOBA_BLOCK
        for j in range(n_tiles):
            kaug_ref[j, :, :hd] = k_ref[0, j]
            kaug_ref[j, :, hd:] = ((lane == j * bpt + row_blk) | (lane >= far_lane0)).astype(kaug_ref.dtype)
        for n in range(n_blocks):
            rows = slice((n % bpt) * MOBA_BLOCK, (n % bpt + 1) * MOBA_BLOCK)
            kmean_ref[n:n + 1, :] = jnp.mean(k_ref[0, n // bpt, rows, :].astype(F32), axis=0,
                                             keepdims=True)

    q = q_ref[0]

    gate = lax.dot_general(kmean_ref[...].astype(MXU_DTYPE), q, (((1,), (1,)), ((), ())),
                           preferred_element_type=F32)
    blk = lax.broadcasted_iota(jnp.int32, gate.shape, 0)
    own = t * bpt + lax.broadcasted_iota(jnp.int32, gate.shape, 1) // MOBA_BLOCK
    past = blk < own
    g = jnp.where(past, gate, NEG_INF)
    sel = jnp.where(blk == own, 0.0, NEG_INF)
    for _ in range(MOBA_TOPK):
        mx = jnp.max(g, axis=0, keepdims=True)
        pick = blk == jnp.min(jnp.where(g == mx, blk, nbp), axis=0, keepdims=True)
        sel = jnp.where(pick & past, 0.0, sel)
        g = jnp.where(pick, BELOW_NEG_INF, g)
    selt_ref[0:nbp, :] = sel
    far_hi, far_mid, far_lo = _split3(far_ref[0])
    lane = lax.broadcasted_iota(jnp.int32, (tq, LANES), 1)
    extra = selt_ref[...].T.astype(MXU_DTYPE)
    extra = jnp.where(lane == far_lane0, far_hi,
                      jnp.where(lane == far_lane0 + 1, far_mid,
                                jnp.where(lane == far_lane0 + 2, far_lo, extra)))
    qaug_ref[:, :hd] = q
    qaug_ref[:, hd:] = extra
    q_aug = qaug_ref[...]

    def qk(j):
        return lax.dot_general(q_aug, kaug_ref[j], (((1,), (1,)), ((), ())),
                               preferred_element_type=F32)

    def consume(j, s, kind):
        if kind == "own":
            s = s + bias_ref[0, 0]
        elif kind == "previous":
            s = s + bias_ref[0, 1]
        _softmax_step(s, vaug_ref[j], m_ref, acc_ref)

    _softmax_init(m_ref, acc_ref)

    @pl.when(t == 0)
    def _():
        _pipelined_tiles(0, ["own"], qk, consume, s_ref, loop=False)

    @pl.when(t >= 1)
    def _():
        _pipelined_tiles(t - 1, ["previous", "own"], qk, consume, s_ref)

    acc = acc_ref[...]
    o_ref[...] = (acc[:, :hd] / acc[:, hd:]).astype(o_ref.dtype)


def _moba_attention(qkv, bias_tiles, rel_bias, n_heads):
    _, s, hd = qkv.shape
    t = bias_tiles.shape[2]
    assert s % t == 0 and t % MOBA_BLOCK == 0 and t >= REL_MAX_DIST and hd == LANES
    assert s // MOBA_BLOCK <= LANES - 3
    n_tiles = s // t
    nbp = -(-(s // MOBA_BLOCK) // SUBLANES) * SUBLANES
    kv = qkv.reshape(3 * n_heads, n_tiles, t, hd)
    far = jnp.broadcast_to(rel_bias[REL_BUCKETS - 1][:, None, None] * LOG2E, (n_heads, 1, LANES))
    return pl.pallas_call(
        _moba_kernel,
        out_shape=jax.ShapeDtypeStruct((s, n_heads * hd), MXU_DTYPE),
        grid=(n_heads, n_tiles),
        in_specs=[pl.BlockSpec((1, t, hd), lambda h, i: (h, i, 0)),
                  pl.BlockSpec((1, n_tiles, t, hd), lambda h, i: (n_heads + h, 0, 0, 0)),
                  pl.BlockSpec((1, n_tiles, t, hd), lambda h, i: (2 * n_heads + h, 0, 0, 0)),
                  pl.BlockSpec((1, 2, t, t), lambda h, i: (h, 0, 0, 0)),
                  pl.BlockSpec((1, 1, LANES), lambda h, i: (h, 0, 0))],
        out_specs=pl.BlockSpec((t, hd), lambda h, i: (i, h)),
        scratch_shapes=[pltpu.VMEM((n_tiles, t, 2 * hd), MXU_DTYPE),
                        pltpu.VMEM((n_tiles, t, 2 * hd), MXU_DTYPE),
                        pltpu.VMEM((t, 2 * hd), MXU_DTYPE),
                        pltpu.VMEM((nbp, hd), F32),
                        pltpu.VMEM((LANES, t), F32),
                        pltpu.VMEM((2, t, t), F32),
                        pltpu.VMEM((t, LANES), F32),
                        pltpu.VMEM((t, 2 * hd), F32)],
        compiler_params=_cparams(("arbitrary", "arbitrary")),
        name="moba_attention",
    )(qkv, kv, kv, bias_tiles, far)


def _ffn_kernel(te_ref, tr_ref, x_ref, w1_ref, w3_ref, w2_ref, o_ref, w1b_ref, w3b_ref, w2b_ref):
    t = pl.program_id(0)
    f = pl.program_id(1)
    rows = tr_ref[t]
    tm = x_ref.shape[0]
    sub = min(FFN_SUB_ROWS, tm)

    def gate_up(r):
        x = x_ref[r, :]
        a = jnp.dot(x, w1b_ref[...], preferred_element_type=F32)
        b = jnp.dot(x, w3b_ref[...], preferred_element_type=F32)
        return (a * jax.nn.sigmoid(a) * b).astype(MXU_DTYPE)

    def down(g, r):
        o_ref[r, :] += jnp.dot(g, w2b_ref[...], preferred_element_type=F32)

    @pl.when(f == 0)
    def _():
        o_ref[...] = jnp.zeros_like(o_ref)

    n_chunks = (rows + sub - 1) // sub
    for nc in range(1, tm // sub + 1):
        @pl.when(n_chunks == nc)
        def _(nc=nc):
            w1b_ref[...] = w1_ref[0].astype(w1b_ref.dtype)
            w3b_ref[...] = w3_ref[0].astype(w3b_ref.dtype)
            w2b_ref[...] = w2_ref[0].astype(w2b_ref.dtype)
            for c in range(nc):
                r = pl.ds(c * sub, sub)
                down(gate_up(r), r)


def _swiglu_ffn(xs, w1, w3, w2, tile_expert, tile_rows, tm):
    r, d = xs.shape
    ff = w1.shape[2]
    tf = _largest_divisor(ff, (FFN_F_TILE, 128))
    nf = ff // tf

    def f_eff(t, f, tr):
        return jnp.where(tr[t] > 0, f, nf - 1)

    return pl.pallas_call(
        _ffn_kernel,
        out_shape=jax.ShapeDtypeStruct((r, d), F32),
        grid_spec=pltpu.PrefetchScalarGridSpec(
            num_scalar_prefetch=2, grid=(r // tm, nf),
            in_specs=[pl.BlockSpec((tm, d), lambda t, f, te, tr: (t, 0)),
                      pl.BlockSpec((1, d, tf), lambda t, f, te, tr: (te[t], 0, f_eff(t, f, tr))),
                      pl.BlockSpec((1, d, tf), lambda t, f, te, tr: (te[t], 0, f_eff(t, f, tr))),
                      pl.BlockSpec((1, tf, d), lambda t, f, te, tr: (te[t], f_eff(t, f, tr), 0))],
            out_specs=pl.BlockSpec((tm, d), lambda t, f, te, tr: (t, 0)),
            scratch_shapes=[pltpu.VMEM((d, tf), MXU_DTYPE), pltpu.VMEM((d, tf), MXU_DTYPE),
                            pltpu.VMEM((tf, d), MXU_DTYPE)]),
        compiler_params=_cparams(("arbitrary", "arbitrary")),
        name="swiglu_ffn",
    )(tile_expert, tile_rows, xs, w1, w3, w2)


def _gather_kernel(tok_ref, nrows_ref, h_ref, o_ref, buf_ref, sem):
    t = pl.program_id(0)
    n_chunks = pl.num_programs(0)
    sub = o_ref.shape[0]
    slot = t % 2

    def request(chunk, into):
        def start(r, carry):
            tok = tok_ref[chunk * sub + r]
            pltpu.make_async_copy(h_ref.at[pl.ds(tok, 1)], buf_ref.at[into, pl.ds(r, 1)],
                                  sem.at[into]).start()
            return carry

        lax.fori_loop(0, sub, start, 0, unroll=DMA_ISSUE_UNROLL)

    @pl.when((t == 0) & (nrows_ref[0] > 0))
    def _():
        request(0, 0)

    nxt = jnp.minimum(t + 1, n_chunks - 1)

    @pl.when((t + 1 < n_chunks) & (nrows_ref[nxt] > 0))
    def _():
        request(nxt, 1 - slot)

    @pl.when(nrows_ref[t] > 0)
    def _():
        pltpu.make_async_copy(h_ref.at[pl.ds(0, sub)], buf_ref.at[slot], sem.at[slot]).wait()
        o_ref[...] = buf_ref[slot].astype(o_ref.dtype)

    @pl.when(nrows_ref[t] == 0)
    def _():
        o_ref[...] = jnp.zeros_like(o_ref)


def _gather_rows(h, row_token, sub_rows, sub):
    s, d = h.shape
    r = row_token.shape[0]
    return pl.pallas_call(
        _gather_kernel,
        out_shape=jax.ShapeDtypeStruct((r, d), MXU_DTYPE),
        grid_spec=pltpu.PrefetchScalarGridSpec(
            num_scalar_prefetch=2, grid=(r // sub,),
            in_specs=[pl.BlockSpec(memory_space=pl.ANY)],
            out_specs=pl.BlockSpec((sub, d), lambda t, tok, nr: (t, 0)),
            scratch_shapes=[pltpu.VMEM((2, sub, d), F32), pltpu.SemaphoreType.DMA((2,))]),
        compiler_params=_cparams(("arbitrary",)),
        name="moe_gather",
    )(row_token, sub_rows, h)


def _combine_kernel(pos_ref, x_ref, wts_ref, gt_ref, ys_ref, o_ref, buf_ref, sem):
    t = pl.program_id(0)
    n_tiles = pl.num_programs(0)
    tm = x_ref.shape[0]
    slot = t % 2

    def request(tile, into):
        def start(r, carry):
            for k in range(TOP_K):
                src = pos_ref[(tile * tm + r) * TOP_K + k]
                pltpu.make_async_copy(ys_ref.at[pl.ds(src, 1)], buf_ref.at[into, k, pl.ds(r, 1)],
                                      sem.at[into]).start()
            return carry

        lax.fori_loop(0, tm, start, 0, unroll=DMA_ISSUE_UNROLL)

    @pl.when(t == 0)
    def _():
        request(0, 0)

    @pl.when(t + 1 < n_tiles)
    def _():
        request(t + 1, 1 - slot)

    for k in range(TOP_K):
        pltpu.make_async_copy(ys_ref.at[pl.ds(0, tm)], buf_ref.at[slot, k], sem.at[slot]).wait()
    wts = wts_ref[...]
    y = wts[:, 0:1] * buf_ref[slot, 0]
    for k in range(1, TOP_K):
        y = y + wts[:, k:k + 1] * buf_ref[slot, k]
    o_ref[...] = x_ref[...] + gt_ref[...] * y


def _combine_residual(x, wts, gt, ys, pos):
    s, d = x.shape
    tm = min(COMBINE_TILE, s)
    return pl.pallas_call(
        _combine_kernel,
        out_shape=jax.ShapeDtypeStruct((s, d), F32),
        grid_spec=pltpu.PrefetchScalarGridSpec(
            num_scalar_prefetch=1, grid=(s // tm,),
            in_specs=[pl.BlockSpec((tm, d), lambda t, pos: (t, 0)),
                      pl.BlockSpec((tm, LANES), lambda t, pos: (t, 0)),
                      pl.BlockSpec((1, d), lambda t, pos: (0, 0)),
                      pl.BlockSpec(memory_space=pl.ANY)],
            out_specs=pl.BlockSpec((tm, d), lambda t, pos: (t, 0)),
            scratch_shapes=[pltpu.VMEM((2, TOP_K, tm, d), F32), pltpu.SemaphoreType.DMA((2,))]),
        compiler_params=_cparams(("arbitrary",)),
        name="moe_combine",
    )(pos, x, wts, gt, ys)


def _moe_plan(idx, n_experts, tm, sub):
    s = idx.shape[0]
    n_slots = s * TOP_K
    e_flat = idx.reshape(-1)
    onehot = (e_flat[:, None] == jnp.arange(n_experts, dtype=jnp.int32)[None, :]).astype(jnp.int32)
    cnt = jnp.sum(onehot, axis=0)
    rank = jnp.sum((jnp.cumsum(onehot, axis=0) - onehot) * onehot, axis=1)
    ntile_e = (cnt + tm - 1) // tm
    per_e = (cnt + ntile_e * sub - 1) // jnp.maximum(ntile_e * sub, 1) * sub
    per_e = jnp.maximum(per_e, sub)
    tile_end = jnp.cumsum(ntile_e)
    tile_start = tile_end - ntile_e
    per_slot = per_e[e_flat]
    pos = (tile_start[e_flat] + rank // per_slot) * tm + rank % per_slot
    n_tiles = n_slots // tm + n_experts
    row_token = jnp.zeros((n_tiles * tm,), jnp.int32).at[pos].set(
        jnp.arange(n_slots, dtype=jnp.int32) // TOP_K)
    t_ids = jnp.arange(n_tiles, dtype=jnp.int32)
    te = jnp.sum((t_ids[:, None] >= tile_end[None, :]).astype(jnp.int32), axis=1)
    used = te < n_experts
    te_c = jnp.minimum(te, n_experts - 1)
    rows = jnp.where(used, jnp.clip(cnt[te_c] - (t_ids - tile_start[te_c]) * per_e[te_c], 0, per_e[te_c]), 0)
    e_last = jnp.max(jnp.where(ntile_e > 0, jnp.arange(n_experts, dtype=jnp.int32), 0))
    tile_expert = jnp.where(used, te_c, e_last).astype(jnp.int32)
    per = tm // sub
    s_ids = jnp.arange(n_tiles * per, dtype=jnp.int32)
    sub_rows = jnp.clip(rows[s_ids // per] - (s_ids % per) * sub, 0, sub)
    return pos.astype(jnp.int32), row_token, tile_expert, rows.astype(jnp.int32), sub_rows.astype(jnp.int32)


def kernel(x, c, w_ada, b_ada, g_mix, g_ffn, g_final, rel_bias, w_qkv_fox, w_f_fox, b_f_fox, w_o_fox, w_qkv_moba, w_o_moba, w1_dense, w3_dense, w2_dense, w_router, w1_moe, w3_moe, w2_moe):
    b, s, d = x.shape
    assert b == 1 and d % HEAD_DIM == 0
    depth = w_ada.shape[0]
    n_heads = d // HEAD_DIM
    n_experts = w_router.shape[2]
    tm = min(ROW_TILE, s)
    sub = min(FFN_SUB_ROWS, tm)

    mod = _ada_modulation(c, w_ada, b_ada)
    bias_tiles = _moba_bias_tiles(rel_bias, min(ATTN_TILE, s))
    dense_rows = jnp.full((s // tm,), tm, jnp.int32)
    moe_w13_shape = (-1,) + w1_moe.shape[2:]
    moe_w2_shape = (-1,) + w2_moe.shape[2:]

    xc = x.reshape(s, d)
    pending = None
    for i in range(depth):
        j = i // 2
        sh1, sc1, gt1, sh2, sc2, gt2 = [mod[i, :, k * d:(k + 1) * d] for k in range(6)]
        g1 = g_mix[i].reshape(1, d)
        g2 = g_ffn[i].reshape(1, d)

        if i % 2 == 0:
            r = _norm_modulate(xc, g1, sc1, sh1, mode="fox", res=pending,
                               w_f=w_f_fox[j], b_f=b_f_fox[j])
        else:
            r = _norm_modulate(xc, g1, sc1, sh1, mode="attn", res=pending)
        xc = r.get("x", xc)
        pending = None
        if i % 2 == 0:
            qkv = _qkv_projection(r["h"], w_qkv_fox, j)
            nt = r["f"].shape[0]
            f_rows = r["f"].transpose(1, 0, 2).reshape(n_heads, nt, 1, r["f"].shape[2])
            o = _fox_attention(qkv, f_rows, n_heads)
            xc = _out_projection_residual(o, w_o_fox, j, xc, gt1)
        else:
            qkv = _qkv_projection(r["h"], w_qkv_moba, j)
            o = _moba_attention(qkv, bias_tiles, rel_bias, n_heads)
            xc = _out_projection_residual(o, w_o_moba, j, xc, gt1)

        if i % 2 == 0:
            r = _norm_modulate(xc, g2, sc2, sh2, mode="dense")
            y = _swiglu_ffn(r["h"], w1_dense, w3_dense, w2_dense,
                            jnp.full((s // tm,), j, jnp.int32), dense_rows, tm)
            pending = (y, gt2)
        else:
            r = _norm_modulate(xc, g2, sc2, sh2, mode="moe", w_router=w_router[j])
            pos, row_token, tile_expert, tile_rows, sub_rows = _moe_plan(
                r["idx"][:, :TOP_K], n_experts, tm, sub)
            xs = _gather_rows(r["h"], row_token, sub_rows, sub)
            ys = _swiglu_ffn(xs, w1_moe.reshape(moe_w13_shape), w3_moe.reshape(moe_w13_shape),
                             w2_moe.reshape(moe_w2_shape), tile_expert + j * n_experts, tile_rows, tm)
            xc = _combine_residual(xc, r["wts"], gt2, ys, pos)

    zero = jnp.zeros((1, d), F32)
    r = _norm_modulate(xc, g_final.reshape(1, d), zero, zero, mode="final", res=pending)
    return r["h"].reshape(b, s, d)
```

```python
import functools
import math

import numpy as np
import jax
import jax.numpy as jnp
from jax import lax
from jax.experimental import pallas as pl
from jax.experimental.pallas import tpu as pltpu

HEAD_DIM = 128
MOBA_BLOCK = 256
MOBA_TOPK = 3
REL_BUCKETS = 32
REL_MAX_DIST = 128
TOP_K = 2
RMS_EPS = 1e-6
NEG_INF = -1e30
BELOW_NEG_INF = -3e38
LOG2E = math.log2(math.e)
QK_SCALE_LOG2 = HEAD_DIM ** -0.5 * LOG2E

LANES = 128
SUBLANES = 8
VMEM_LIMIT_BYTES = 56 * 1024 * 1024

MXU_DTYPE = jnp.bfloat16
F32 = jnp.float32

ROW_TILE = 1024
FFN_F_TILE = 512
FFN_SUB_ROWS = 256
ATTN_TILE = 512
NORM_TILE = 512
COMBINE_TILE = 256
DMA_ISSUE_UNROLL = 8


def _cparams(semantics):
    return pltpu.CompilerParams(dimension_semantics=semantics,
                                vmem_limit_bytes=VMEM_LIMIT_BYTES)


def _largest_divisor(n, candidates):
    for c in candidates:
        if n % c == 0:
            return c
    raise ValueError(f"no tile in {candidates} divides {n}")


def _ada_kernel(c_ref, w_ref, b_ref, o_ref, ca_ref):
    c = c_ref[...]
    ca_ref[...] = c * jax.nn.sigmoid(c)
    d, tn = w_ref.shape[1], w_ref.shape[2]
    ch = min(d, 256)

    def body(r, acc):
        rows = pl.ds(pl.multiple_of(r * ch, ch), ch)
        prod = w_ref[0, rows, :] * ca_ref[rows, :]
        return acc + jnp.sum(prod.reshape(ch // SUBLANES, SUBLANES, tn), axis=0)

    acc = lax.fori_loop(0, d // ch, body, jnp.zeros((SUBLANES, tn), F32))
    o_ref[0] = jnp.sum(acc, axis=0, keepdims=True) + b_ref[0]


def _ada_modulation(c, w_ada, b_ada):
    depth, d, n = w_ada.shape
    tn = _largest_divisor(n, (1024, 512, 256, 128))
    return pl.pallas_call(
        _ada_kernel,
        out_shape=jax.ShapeDtypeStruct((depth, 1, n), F32),
        grid=(depth, n // tn),
        in_specs=[pl.BlockSpec((d, 1), lambda i, j: (0, 0)),
                  pl.BlockSpec((1, d, tn), lambda i, j: (i, 0, j)),
                  pl.BlockSpec((1, 1, tn), lambda i, j: (i, 0, j))],
        out_specs=pl.BlockSpec((1, 1, tn), lambda i, j: (i, 0, j)),
        scratch_shapes=[pltpu.VMEM((d, 1), F32)],
        compiler_params=_cparams(("arbitrary", "arbitrary")),
        name="ada_modulation",
    )(c.reshape(d, 1), w_ada, b_ada.reshape(depth, 1, n))


def _norm_kernel(*refs, has_res, mode, n_experts):
    refs = list(refs)
    x_ref = refs.pop(0)
    if has_res:
        y_ref, gt_ref = refs.pop(0), refs.pop(0)
    g_ref, sc_ref, sh_ref = refs.pop(0), refs.pop(0), refs.pop(0)
    if mode == "moe":
        wr_ref = refs.pop(0)
    if mode == "fox":
        wf_ref, bf_ref = refs.pop(0), refs.pop(0)
    if has_res:
        xo_ref = refs.pop(0)
    h_ref = refs.pop(0)

    x = x_ref[...]
    if has_res:
        x = x + gt_ref[...] * y_ref[...]
        xo_ref[...] = x
    ms = jnp.mean(x * x, axis=-1, keepdims=True)
    h = (x * lax.rsqrt(ms + RMS_EPS) * g_ref[...]) * (1.0 + sc_ref[...]) + sh_ref[...]
    h_ref[...] = h.astype(h_ref.dtype)

    if mode == "moe":
        idx_ref, wts_ref = refs.pop(0), refs.pop(0)
        logits = jnp.dot(h.astype(MXU_DTYPE), wr_ref[...], preferred_element_type=F32)
        lane = lax.broadcasted_iota(jnp.int32, logits.shape, 1)
        lg = jnp.where(lane < n_experts, logits, BELOW_NEG_INF)
        v0 = jnp.max(lg, axis=-1, keepdims=True)
        i0 = jnp.min(jnp.where(lg == v0, lane, LANES), axis=-1, keepdims=True)
        lg = jnp.where(lane == i0, BELOW_NEG_INF, lg)
        v1 = jnp.max(lg, axis=-1, keepdims=True)
        i1 = jnp.min(jnp.where(lg == v1, lane, LANES), axis=-1, keepdims=True)
        e1 = jnp.exp(v1 - v0)
        den = 1.0 + e1
        idx_ref[...] = jnp.where(lane == 0, i0, jnp.where(lane == 1, i1, 0))
        wts_ref[...] = jnp.where(lane == 0, 1.0 / den, jnp.where(lane == 1, e1 / den, 0.0))

    if mode == "fox":
        f_ref, carry_ref = refs.pop(0), refs.pop(0)

        @pl.when(pl.program_id(0) == 0)
        def _():
            carry_ref[...] = jnp.zeros_like(carry_ref)

        tm = x.shape[0]
        z = lax.dot_general(wf_ref[...], h.astype(MXU_DTYPE), (((1,), (1,)), ((), ())),
                            preferred_element_type=F32) + bf_ref[...]
        log_f = jnp.minimum(z, 0.0) - jnp.log1p(jnp.exp(-jnp.abs(z)))
        upper = (lax.broadcasted_iota(jnp.int32, (tm, tm), 0)
                 <= lax.broadcasted_iota(jnp.int32, (tm, tm), 1)).astype(F32)
        cum = jnp.dot(log_f, upper, preferred_element_type=F32,
                      precision=lax.Precision.HIGHEST) + carry_ref[...]
        f_ref[0] = cum * LOG2E
        carry_ref[...] = cum[:, tm - 1:tm]


def _norm_modulate(x, g, sc, sh, *, mode, res=None, w_router=None, w_f=None, b_f=None):
    s, d = x.shape
    tm = min(NORM_TILE, s)
    row = pl.BlockSpec((tm, d), lambda i: (i, 0))
    vec = pl.BlockSpec((1, d), lambda i: (0, 0))
    args, in_specs, out_shape, out_specs, scratch = [x], [row], [], [], []
    if res is not None:
        y, gt = res
        args += [y, gt]
        in_specs += [row, vec]
        out_shape.append(jax.ShapeDtypeStruct((s, d), F32))
        out_specs.append(row)
    args += [g, sc, sh]
    in_specs += [vec, vec, vec]
    h_dtype = {"attn": MXU_DTYPE, "fox": MXU_DTYPE, "dense": MXU_DTYPE,
               "moe": F32, "final": F32}[mode]
    out_shape.append(jax.ShapeDtypeStruct((s, d), h_dtype))
    out_specs.append(row)
    n_experts = 0
    if mode == "moe":
        n_experts = w_router.shape[1]
        wr = jnp.zeros((d, LANES), MXU_DTYPE).at[:, :n_experts].set(w_router.astype(MXU_DTYPE))
        args.append(wr)
        in_specs.append(pl.BlockSpec((d, LANES), lambda i: (0, 0)))
        lane_blk = pl.BlockSpec((tm, LANES), lambda i: (i, 0))
        out_shape += [jax.ShapeDtypeStruct((s, LANES), jnp.int32),
                      jax.ShapeDtypeStruct((s, LANES), F32)]
        out_specs += [lane_blk, lane_blk]
    if mode == "fox":
        nh = w_f.shape[1]
        args += [w_f.T.astype(MXU_DTYPE), b_f.reshape(nh, 1)]
        in_specs += [pl.BlockSpec((nh, d), lambda i: (0, 0)),
                     pl.BlockSpec((nh, 1), lambda i: (0, 0))]
        out_shape.append(jax.ShapeDtypeStruct((s // tm, nh, tm), F32))
        out_specs.append(pl.BlockSpec((1, nh, tm), lambda i: (i, 0, 0)))
        scratch.append(pltpu.VMEM((nh, 1), F32))
    outs = pl.pallas_call(
        functools.partial(_norm_kernel, has_res=res is not None, mode=mode, n_experts=n_experts),
        out_shape=out_shape,
        grid=(s // tm,),
        in_specs=in_specs,
        out_specs=out_specs,
        scratch_shapes=scratch,
        compiler_params=_cparams(("arbitrary",)),
        name=f"norm_{mode}",
    )(*args)
    outs = list(outs)
    result = {}
    if res is not None:
        result["x"] = outs.pop(0)
    result["h"] = outs.pop(0)
    if mode == "moe":
        result["idx"], result["wts"] = outs.pop(0), outs.pop(0)
    if mode == "fox":
        result["f"] = outs.pop(0)
    return result


def _qkv_kernel(a_ref, w_ref, cs_ref, o_ref, wb_ref):
    @pl.when(pl.program_id(1) == 0)
    def _():
        wb_ref[...] = w_ref[0].astype(wb_ref.dtype)

    acc = jnp.dot(a_ref[...], wb_ref[...], preferred_element_type=F32) * cs_ref[...]
    for j in range(o_ref.shape[0]):
        o_ref[j] = acc[:, j * HEAD_DIM:(j + 1) * HEAD_DIM].astype(o_ref.dtype)


def _qkv_projection(h, w, layer):
    s, d = h.shape
    n = w.shape[2]
    n_heads3 = n // HEAD_DIM
    tm = min(ROW_TILE, s)
    nh = _largest_divisor(n_heads3, (12, 6, 4, 3, 2, 1))
    tn = nh * HEAD_DIM
    col_scale = jnp.where(jnp.arange(n) < n // 3, QK_SCALE_LOG2, 1.0).astype(F32).reshape(1, n)
    return pl.pallas_call(
        _qkv_kernel,
        out_shape=jax.ShapeDtypeStruct((n_heads3, s, HEAD_DIM), MXU_DTYPE),
        grid=(n // tn, s // tm),
        in_specs=[pl.BlockSpec((tm, d), lambda j, i: (i, 0)),
                  pl.BlockSpec((1, d, tn), lambda j, i: (layer, 0, j)),
                  pl.BlockSpec((1, tn), lambda j, i: (0, j))],
        out_specs=pl.BlockSpec((nh, tm, HEAD_DIM), lambda j, i: (j, i, 0)),
        scratch_shapes=[pltpu.VMEM((d, tn), MXU_DTYPE)],
        compiler_params=_cparams(("arbitrary", "arbitrary")),
        name="qkv_projection",
    )(h, w, col_scale)


def _out_proj_kernel(a_ref, w_ref, x_ref, gt_ref, o_ref, wb_ref):
    @pl.when(pl.program_id(1) == 0)
    def _():
        wb_ref[...] = w_ref[0].astype(wb_ref.dtype)

    acc = jnp.dot(a_ref[...], wb_ref[...], preferred_element_type=F32)
    o_ref[...] = x_ref[...] + gt_ref[...] * acc


def _out_projection_residual(a, w, layer, x, gt):
    s, k = a.shape
    n = w.shape[2]
    tm = min(ROW_TILE, s)
    tn = _largest_divisor(n, (1024, 512, 256, 128))
    return pl.pallas_call(
        _out_proj_kernel,
        out_shape=jax.ShapeDtypeStruct((s, n), F32),
        grid=(n // tn, s // tm),
        in_specs=[pl.BlockSpec((tm, k), lambda j, i: (i, 0)),
                  pl.BlockSpec((1, k, tn), lambda j, i: (layer, 0, j)),
                  pl.BlockSpec((tm, tn), lambda j, i: (i, j)),
                  pl.BlockSpec((1, tn), lambda j, i: (0, j))],
        out_specs=pl.BlockSpec((tm, tn), lambda j, i: (i, j)),
        scratch_shapes=[pltpu.VMEM((k, tn), MXU_DTYPE)],
        compiler_params=_cparams(("arbitrary", "arbitrary")),
        name="out_projection",
    )(a, w, x, gt)


def _lane_tiles(x):
    return [x[:, c * LANES:(c + 1) * LANES] for c in range(x.shape[1] // LANES)]


def _softmax_init(m_ref, acc_ref):
    m_ref[...] = jnp.full(m_ref.shape, BELOW_NEG_INF, F32)
    acc_ref[...] = jnp.zeros_like(acc_ref)


def _softmax_step(s, v_aug, m_ref, acc_ref):
    tiles = _lane_tiles(s)
    m_cur = jnp.max(functools.reduce(jnp.maximum, tiles), axis=-1, keepdims=True)
    m_old = m_ref[...]
    m_new = jnp.maximum(m_old, m_cur)
    p = jnp.concatenate([jnp.exp2(t - m_new) for t in tiles], axis=1).astype(MXU_DTYPE)
    pv = jnp.dot(p, v_aug, preferred_element_type=F32)
    alpha = jnp.exp2(m_old - m_new)
    acc_ref[...] = jnp.concatenate([alpha] * (acc_ref.shape[1] // LANES), axis=1) * acc_ref[...] + pv
    m_ref[...] = m_new


def _pipelined_tiles(n_plain, tail, qk, consume, s_ref, loop=True):
    s_ref[0] = qk(0)

    def step(j, cur, kind, prefetch=True):
        s = s_ref[cur]
        if prefetch:
            s_ref[1 - cur] = qk(j + 1)
        consume(j, s, kind)

    def pair(i, carry):
        step(2 * i, 0, None)
        step(2 * i + 1, 1, None)
        return carry

    def run_tail(cur):
        for k, kind in enumerate(tail):
            step(n_plain + k, cur, kind, prefetch=k + 1 < len(tail))
            cur = 1 - cur

    if not loop:
        run_tail(0)
        return
    lax.fori_loop(0, n_plain // 2, pair, 0)

    @pl.when(n_plain % 2 == 0)
    def _():
        run_tail(0)

    @pl.when(n_plain % 2 == 1)
    def _():
        step(n_plain - 1, 0, None)
        run_tail(1)


def _augment_values(v_ref, vaug_ref):
    hd = v_ref.shape[-1]
    for j in range(v_ref.shape[1]):
        vaug_ref[j, :, :hd] = v_ref[0, j]
        vaug_ref[j, :, hd:] = jnp.ones((v_ref.shape[2], hd), vaug_ref.dtype)


def _fox_kernel(q_ref, k_ref, v_ref, f_ref, o_ref, vaug_ref, s_ref, m_ref, acc_ref):
    i = pl.program_id(1)
    hd = q_ref.shape[2]

    @pl.when(i == 0)
    def _():
        _augment_values(v_ref, vaug_ref)

    q = q_ref[0]
    tq = q.shape[0]

    def qk(j):
        return lax.dot_general(q, k_ref[0, j], (((1,), (1,)), ((), ())),
                               preferred_element_type=F32)

    def consume(j, s, kind):
        s = s - f_ref[0, j]
        if kind == "diagonal":
            causal = (lax.broadcasted_iota(jnp.int32, (tq, tq), 1)
                      <= lax.broadcasted_iota(jnp.int32, (tq, tq), 0))
            s = jnp.where(causal, s, NEG_INF)
        _softmax_step(s, vaug_ref[j], m_ref, acc_ref)

    _softmax_init(m_ref, acc_ref)
    _pipelined_tiles(i, ["diagonal"], qk, consume, s_ref)

    acc = acc_ref[...]
    o_ref[...] = (acc[:, :hd] / acc[:, hd:]).astype(o_ref.dtype)


def _fox_attention(qkv, f_rows, n_heads):
    _, s, hd = qkv.shape
    t = min(ATTN_TILE, s)
    nb = s // t
    kv = qkv.reshape(3 * n_heads, nb, t, hd)
    return pl.pallas_call(
        _fox_kernel,
        out_shape=jax.ShapeDtypeStruct((s, n_heads * hd), MXU_DTYPE),
        grid=(n_heads, nb),
        in_specs=[pl.BlockSpec((1, t, hd), lambda h, i: (h, i, 0)),
                  pl.BlockSpec((1, nb, t, hd), lambda h, i: (n_heads + h, 0, 0, 0)),
                  pl.BlockSpec((1, nb, t, hd), lambda h, i: (2 * n_heads + h, 0, 0, 0)),
                  pl.BlockSpec((1, nb, 1, t), lambda h, i: (h, 0, 0, 0))],
        out_specs=pl.BlockSpec((t, hd), lambda h, i: (i, h)),
        scratch_shapes=[pltpu.VMEM((nb, t, 2 * hd), MXU_DTYPE),
                        pltpu.VMEM((2, t, t), F32),
                        pltpu.VMEM((t, LANES), F32),
                        pltpu.VMEM((t, 2 * hd), F32)],
        compiler_params=_cparams(("arbitrary", "arbitrary")),
        name="fox_attention",
    )(qkv, kv, kv, f_rows)


def _rel_bucket_np(dist):
    n = np.maximum(dist, 0)
    max_exact = REL_BUCKETS // 2
    nf = np.maximum(n, 1).astype(np.float32)
    large = max_exact + (np.log(nf / np.float32(max_exact))
                         / np.float32(math.log(REL_MAX_DIST / max_exact))
                         * np.float32(REL_BUCKETS - max_exact)).astype(np.int32)
    large = np.minimum(large, REL_BUCKETS - 1)
    return np.where(n < max_exact, n, large).astype(np.int32)


def _bias_table_kernel(rb_ref, idx_ref, o_ref, *, n_heads):
    h = pl.program_id(0)
    idx = idx_ref[0]
    bias = jnp.zeros(idx.shape, F32)
    far = rb_ref[(REL_BUCKETS - 1) * n_heads + h]
    for b in range(REL_BUCKETS):
        bias = jnp.where(idx == b, (rb_ref[b * n_heads + h] - far) * LOG2E, bias)
    o_ref[0, 0] = jnp.where(idx < 0, NEG_INF, bias)


def _moba_bias_tiles(rel_bias, t):
    n_heads = rel_bias.shape[1]
    dist = np.arange(2)[:, None, None] * t + np.arange(t)[None, :, None] - np.arange(t)[None, None, :]
    idx = jnp.asarray(np.where(dist < 0, -1, _rel_bucket_np(dist)).astype(np.int32))
    return pl.pallas_call(
        functools.partial(_bias_table_kernel, n_heads=n_heads),
        out_shape=jax.ShapeDtypeStruct((n_heads, 2, t, t), F32),
        grid_spec=pltpu.PrefetchScalarGridSpec(
            num_scalar_prefetch=1, grid=(n_heads, 2),
            in_specs=[pl.BlockSpec((1, t, t), lambda h, dt, rb: (dt, 0, 0))],
            out_specs=pl.BlockSpec((1, 1, t, t), lambda h, dt, rb: (h, dt, 0, 0))),
        compiler_params=_cparams(("arbitrary", "arbitrary")),
        name="moba_bias_tiles",
    )(rel_bias.reshape(-1), idx)


def _split3(x):
    hi = x.astype(MXU_DTYPE)
    r1 = x - hi.astype(F32)
    mid = r1.astype(MXU_DTYPE)
    lo = (r1 - mid.astype(F32)).astype(MXU_DTYPE)
    return hi, mid, lo


def _moba_kernel(q_ref, k_ref, v_ref, bias_ref, far_ref, o_ref,
                 vaug_ref, kaug_ref, qaug_ref, kmean_ref, selt_ref, s_ref, m_ref, acc_ref):
    t = pl.program_id(1)
    n_tiles, tq, hd = k_ref.shape[1], k_ref.shape[2], k_ref.shape[3]
    bpt = tq // MOBA_BLOCK
    n_blocks = n_tiles * bpt
    nbp = kmean_ref.shape[0]
    far_lane0 = LANES - 3

    @pl.when(t == 0)
    def _():
        _augment_values(v_ref, vaug_ref)
        kmean_ref[...] = jnp.zeros_like(kmean_ref)
        selt_ref[...] = jnp.zeros_like(selt_ref)
        lane = lax.broadcasted_iota(jnp.int32, (tq, LANES), 1)
        row_blk = lax.broadcasted_iota(jnp.int32, (tq, LANES), 0) // MOBA_BLOCK
        for j in range(n_tiles):
            kaug_ref[j, :, :hd] = k_ref[0, j]
            kaug_ref[j, :, hd:] = ((lane == j * bpt + row_blk) | (lane >= far_lane0)).astype(kaug_ref.dtype)
        for n in range(n_blocks):
            rows = slice((n % bpt) * MOBA_BLOCK, (n % bpt + 1) * MOBA_BLOCK)
            kmean_ref[n:n + 1, :] = jnp.mean(k_ref[0, n // bpt, rows, :].astype(F32), axis=0,
                                             keepdims=True)

    q = q_ref[0]

    gate = lax.dot_general(kmean_ref[...].astype(MXU_DTYPE), q, (((1,), (1,)), ((), ())),
                           preferred_element_type=F32)
    blk = lax.broadcasted_iota(jnp.int32, gate.shape, 0)
    own = t * bpt + lax.broadcasted_iota(jnp.int32, gate.shape, 1) // MOBA_BLOCK
    past = blk < own
    g = jnp.where(past, gate, NEG_INF)
    sel = jnp.where(blk == own, 0.0, NEG_INF)
    for _ in range(MOBA_TOPK):
        mx = jnp.max(g, axis=0, keepdims=True)
        pick = blk == jnp.min(jnp.where(g == mx, blk, nbp), axis=0, keepdims=True)
        sel = jnp.where(pick & past, 0.0, sel)
        g = jnp.where(pick, BELOW_NEG_INF, g)
    selt_ref[0:nbp, :] = sel
    far_hi, far_mid, far_lo = _split3(far_ref[0])
    lane = lax.broadcasted_iota(jnp.int32, (tq, LANES), 1)
    extra = selt_ref[...].T.astype(MXU_DTYPE)
    extra = jnp.where(lane == far_lane0, far_hi,
                      jnp.where(lane == far_lane0 + 1, far_mid,
                                jnp.where(lane == far_lane0 + 2, far_lo, extra)))
    qaug_ref[:, :hd] = q
    qaug_ref[:, hd:] = extra
    q_aug = qaug_ref[...]

    def qk(j):
        return lax.dot_general(q_aug, kaug_ref[j], (((1,), (1,)), ((), ())),
                               preferred_element_type=F32)

    def consume(j, s, kind):
        if kind == "own":
            s = s + bias_ref[0, 0]
        elif kind == "previous":
            s = s + bias_ref[0, 1]
        _softmax_step(s, vaug_ref[j], m_ref, acc_ref)

    _softmax_init(m_ref, acc_ref)

    @pl.when(t == 0)
    def _():
        _pipelined_tiles(0, ["own"], qk, consume, s_ref, loop=False)

    @pl.when(t >= 1)
    def _():
        _pipelined_tiles(t - 1, ["previous", "own"], qk, consume, s_ref)

    acc = acc_ref[...]
    o_ref[...] = (acc[:, :hd] / acc[:, hd:]).astype(o_ref.dtype)


def _moba_attention(qkv, bias_tiles, rel_bias, n_heads):
    _, s, hd = qkv.shape
    t = bias_tiles.shape[2]
    assert s % t == 0 and t % MOBA_BLOCK == 0 and t >= REL_MAX_DIST and hd == LANES
    assert s // MOBA_BLOCK <= LANES - 3
    n_tiles = s // t
    nbp = -(-(s // MOBA_BLOCK) // SUBLANES) * SUBLANES
    kv = qkv.reshape(3 * n_heads, n_tiles, t, hd)
    far = jnp.broadcast_to(rel_bias[REL_BUCKETS - 1][:, None, None] * LOG2E, (n_heads, 1, LANES))
    return pl.pallas_call(
        _moba_kernel,
        out_shape=jax.ShapeDtypeStruct((s, n_heads * hd), MXU_DTYPE),
        grid=(n_heads, n_tiles),
        in_specs=[pl.BlockSpec((1, t, hd), lambda h, i: (h, i, 0)),
                  pl.BlockSpec((1, n_tiles, t, hd), lambda h, i: (n_heads + h, 0, 0, 0)),
                  pl.BlockSpec((1, n_tiles, t, hd), lambda h, i: (2 * n_heads + h, 0, 0, 0)),
                  pl.BlockSpec((1, 2, t, t), lambda h, i: (h, 0, 0, 0)),
                  pl.BlockSpec((1, 1, LANES), lambda h, i: (h, 0, 0))],
        out_specs=pl.BlockSpec((t, hd), lambda h, i: (i, h)),
        scratch_shapes=[pltpu.VMEM((n_tiles, t, 2 * hd), MXU_DTYPE),
                        pltpu.VMEM((n_tiles, t, 2 * hd), MXU_DTYPE),
                        pltpu.VMEM((t, 2 * hd), MXU_DTYPE),
                        pltpu.VMEM((nbp, hd), F32),
                        pltpu.VMEM((LANES, t), F32),
                        pltpu.VMEM((2, t, t), F32),
                        pltpu.VMEM((t, LANES), F32),
                        pltpu.VMEM((t, 2 * hd), F32)],
        compiler_params=_cparams(("arbitrary", "arbitrary")),
        name="moba_attention",
    )(qkv, kv, kv, bias_tiles, far)


def _ffn_kernel(te_ref, tr_ref, x_ref, w1_ref, w3_ref, w2_ref, o_ref, w1b_ref, w3b_ref, w2b_ref):
    t = pl.program_id(0)
    f = pl.program_id(1)
    rows = tr_ref[t]
    tm = x_ref.shape[0]
    sub = min(FFN_SUB_ROWS, tm)

    def gate_up(r):
        x = x_ref[r, :]
        a = jnp.dot(x, w1b_ref[...], preferred_element_type=F32)
        b = jnp.dot(x, w3b_ref[...], preferred_element_type=F32)
        return (a * jax.nn.sigmoid(a) * b).astype(MXU_DTYPE)

    def down(g, r):
        o_ref[r, :] += jnp.dot(g, w2b_ref[...], preferred_element_type=F32)

    @pl.when(f == 0)
    def _():
        o_ref[...] = jnp.zeros_like(o_ref)

    n_chunks = (rows + sub - 1) // sub
    for nc in range(1, tm // sub + 1):
        @pl.when(n_chunks == nc)
        def _(nc=nc):
            w1b_ref[...] = w1_ref[0].astype(w1b_ref.dtype)
            w3b_ref[...] = w3_ref[0].astype(w3b_ref.dtype)
            w2b_ref[...] = w2_ref[0].astype(w2b_ref.dtype)
            for c in range(nc):
                r = pl.ds(c * sub, sub)
                down(gate_up(r), r)


def _swiglu_ffn(xs, w1, w3, w2, tile_expert, tile_rows, tm):
    r, d = xs.shape
    ff = w1.shape[2]
    tf = _largest_divisor(ff, (FFN_F_TILE, 128))
    nf = ff // tf

    def f_eff(t, f, tr):
        return jnp.where(tr[t] > 0, f, nf - 1)

    return pl.pallas_call(
        _ffn_kernel,
        out_shape=jax.ShapeDtypeStruct((r, d), F32),
        grid_spec=pltpu.PrefetchScalarGridSpec(
            num_scalar_prefetch=2, grid=(r // tm, nf),
            in_specs=[pl.BlockSpec((tm, d), lambda t, f, te, tr: (t, 0), pipeline_mode=pl.Buffered(1)),
                      pl.BlockSpec((1, d, tf), lambda t, f, te, tr: (te[t], 0, f_eff(t, f, tr))),
                      pl.BlockSpec((1, d, tf), lambda t, f, te, tr: (te[t], 0, f_eff(t, f, tr))),
                      pl.BlockSpec((1, tf, d), lambda t, f, te, tr: (te[t], f_eff(t, f, tr), 0))],
            out_specs=pl.BlockSpec((tm, d), lambda t, f, te, tr: (t, 0), pipeline_mode=pl.Buffered(1)),
            scratch_shapes=[pltpu.VMEM((d, tf), MXU_DTYPE), pltpu.VMEM((d, tf), MXU_DTYPE),
                            pltpu.VMEM((tf, d), MXU_DTYPE)]),
        compiler_params=_cparams(("arbitrary", "arbitrary")),
        name="swiglu_ffn",
    )(tile_expert, tile_rows, xs, w1, w3, w2)


def _gather_kernel(tok_ref, nrows_ref, h_ref, o_ref, buf_ref, sem):
    t = pl.program_id(0)
    n_chunks = pl.num_programs(0)
    sub = o_ref.shape[0]
    slot = t % 2

    def request(chunk, into):
        def start(r, carry):
            tok = tok_ref[chunk * sub + r]
            pltpu.make_async_copy(h_ref.at[pl.ds(tok, 1)], buf_ref.at[into, pl.ds(r, 1)],
                                  sem.at[into]).start()
            return carry

        lax.fori_loop(0, sub, start, 0, unroll=DMA_ISSUE_UNROLL)

    @pl.when((t == 0) & (nrows_ref[0] > 0))
    def _():
        request(0, 0)

    nxt = jnp.minimum(t + 1, n_chunks - 1)

    @pl.when((t + 1 < n_chunks) & (nrows_ref[nxt] > 0))
    def _():
        request(nxt, 1 - slot)

    @pl.when(nrows_ref[t] > 0)
    def _():
        pltpu.make_async_copy(h_ref.at[pl.ds(0, sub)], buf_ref.at[slot], sem.at[slot]).wait()
        o_ref[...] = buf_ref[slot].astype(o_ref.dtype)

    @pl.when(nrows_ref[t] == 0)
    def _():
        o_ref[...] = jnp.zeros_like(o_ref)


def _gather_rows(h, row_token, sub_rows, sub):
    s, d = h.shape
    r = row_token.shape[0]
    return pl.pallas_call(
        _gather_kernel,
        out_shape=jax.ShapeDtypeStruct((r, d), MXU_DTYPE),
        grid_spec=pltpu.PrefetchScalarGridSpec(
            num_scalar_prefetch=2, grid=(r // sub,),
            in_specs=[pl.BlockSpec(memory_space=pl.ANY)],
            out_specs=pl.BlockSpec((sub, d), lambda t, tok, nr: (t, 0)),
            scratch_shapes=[pltpu.VMEM((2, sub, d), F32), pltpu.SemaphoreType.DMA((2,))]),
        compiler_params=_cparams(("arbitrary",)),
        name="moe_gather",
    )(row_token, sub_rows, h)


def _combine_kernel(pos_ref, x_ref, wts_ref, gt_ref, ys_ref, o_ref, buf_ref, sem):
    t = pl.program_id(0)
    n_tiles = pl.num_programs(0)
    tm = x_ref.shape[0]
    slot = t % 2

    def request(tile, into):
        def start(r, carry):
            for k in range(TOP_K):
                src = pos_ref[(tile * tm + r) * TOP_K + k]
                pltpu.make_async_copy(ys_ref.at[pl.ds(src, 1)], buf_ref.at[into, k, pl.ds(r, 1)],
                                      sem.at[into]).start()
            return carry

        lax.fori_loop(0, tm, start, 0, unroll=DMA_ISSUE_UNROLL)

    @pl.when(t == 0)
    def _():
        request(0, 0)

    @pl.when(t + 1 < n_tiles)
    def _():
        request(t + 1, 1 - slot)

    for k in range(TOP_K):
        pltpu.make_async_copy(ys_ref.at[pl.ds(0, tm)], buf_ref.at[slot, k], sem.at[slot]).wait()
    wts = wts_ref[...]
    y = wts[:, 0:1] * buf_ref[slot, 0]
    for k in range(1, TOP_K):
        y = y + wts[:, k:k + 1] * buf_ref[slot, k]
    o_ref[...] = x_ref[...] + gt_ref[...] * y


def _combine_residual(x, wts, gt, ys, pos):
    s, d = x.shape
    tm = min(COMBINE_TILE, s)
    return pl.pallas_call(
        _combine_kernel,
        out_shape=jax.ShapeDtypeStruct((s, d), F32),
        grid_spec=pltpu.PrefetchScalarGridSpec(
            num_scalar_prefetch=1, grid=(s // tm,),
            in_specs=[pl.BlockSpec((tm, d), lambda t, pos: (t, 0)),
                      pl.BlockSpec((tm, LANES), lambda t, pos: (t, 0)),
                      pl.BlockSpec((1, d), lambda t, pos: (0, 0)),
                      pl.BlockSpec(memory_space=pl.ANY)],
            out_specs=pl.BlockSpec((tm, d), lambda t, pos: (t, 0)),
            scratch_shapes=[pltpu.VMEM((2, TOP_K, tm, d), F32), pltpu.SemaphoreType.DMA((2,))]),
        compiler_params=_cparams(("arbitrary",)),
        name="moe_combine",
    )(pos, x, wts, gt, ys)


def _moe_plan(idx, n_experts, tm, sub):
    s = idx.shape[0]
    n_slots = s * TOP_K
    e_flat = idx.reshape(-1)
    onehot = (e_flat[:, None] == jnp.arange(n_experts, dtype=jnp.int32)[None, :]).astype(jnp.int32)
    cnt = jnp.sum(onehot, axis=0)
    rank = jnp.sum((jnp.cumsum(onehot, axis=0) - onehot) * onehot, axis=1)
    ntile_e = (cnt + tm - 1) // tm
    per_e = (cnt + ntile_e * sub - 1) // jnp.maximum(ntile_e * sub, 1) * sub
    per_e = jnp.maximum(per_e, sub)
    tile_end = jnp.cumsum(ntile_e)
    tile_start = tile_end - ntile_e
    per_slot = per_e[e_flat]
    pos = (tile_start[e_flat] + rank // per_slot) * tm + rank % per_slot
    n_tiles = n_slots // tm + n_experts
    row_token = jnp.zeros((n_tiles * tm,), jnp.int32).at[pos].set(
        jnp.arange(n_slots, dtype=jnp.int32) // TOP_K)
    t_ids = jnp.arange(n_tiles, dtype=jnp.int32)
    te = jnp.sum((t_ids[:, None] >= tile_end[None, :]).astype(jnp.int32), axis=1)
    used = te < n_experts
    te_c = jnp.minimum(te, n_experts - 1)
    rows = jnp.where(used, jnp.clip(cnt[te_c] - (t_ids - tile_start[te_c]) * per_e[te_c], 0, per_e[te_c]), 0)
    e_last = jnp.max(jnp.where(ntile_e > 0, jnp.arange(n_experts, dtype=jnp.int32), 0))
    tile_expert = jnp.where(used, te_c, e_last).astype(jnp.int32)
    per = tm // sub
    s_ids = jnp.arange(n_tiles * per, dtype=jnp.int32)
    sub_rows = jnp.clip(rows[s_ids // per] - (s_ids % per) * sub, 0, sub)
    return pos.astype(jnp.int32), row_token, tile_expert, rows.astype(jnp.int32), sub_rows.astype(jnp.int32)


def kernel(x, c, w_ada, b_ada, g_mix, g_ffn, g_final, rel_bias, w_qkv_fox, w_f_fox, b_f_fox, w_o_fox, w_qkv_moba, w_o_moba, w1_dense, w3_dense, w2_dense, w_router, w1_moe, w3_moe, w2_moe):
    b, s, d = x.shape
    assert b == 1 and d % HEAD_DIM == 0
    depth = w_ada.shape[0]
    n_heads = d // HEAD_DIM
    n_experts = w_router.shape[2]
    tm = min(ROW_TILE, s)
    sub = min(FFN_SUB_ROWS, tm)

    mod = _ada_modulation(c, w_ada, b_ada)
    bias_tiles = _moba_bias_tiles(rel_bias, min(ATTN_TILE, s))
    dense_rows = jnp.full((s // tm,), tm, jnp.int32)
    moe_w13_shape = (-1,) + w1_moe.shape[2:]
    moe_w2_shape = (-1,) + w2_moe.shape[2:]

    xc = x.reshape(s, d)
    pending = None
    for i in range(depth):
        j = i // 2
        sh1, sc1, gt1, sh2, sc2, gt2 = [mod[i, :, k * d:(k + 1) * d] for k in range(6)]
        g1 = g_mix[i].reshape(1, d)
        g2 = g_ffn[i].reshape(1, d)

        if i % 2 == 0:
            r = _norm_modulate(xc, g1, sc1, sh1, mode="fox", res=pending,
                               w_f=w_f_fox[j], b_f=b_f_fox[j])
        else:
            r = _norm_modulate(xc, g1, sc1, sh1, mode="attn", res=pending)
        xc = r.get("x", xc)
        pending = None
        if i % 2 == 0:
            qkv = _qkv_projection(r["h"], w_qkv_fox, j)
            nt = r["f"].shape[0]
            f_rows = r["f"].transpose(1, 0, 2).reshape(n_heads, nt, 1, r["f"].shape[2])
            o = _fox_attention(qkv, f_rows, n_heads)
            xc = _out_projection_residual(o, w_o_fox, j, xc, gt1)
        else:
            qkv = _qkv_projection(r["h"], w_qkv_moba, j)
            o = _moba_attention(qkv, bias_tiles, rel_bias, n_heads)
            xc = _out_projection_residual(o, w_o_moba, j, xc, gt1)

        if i % 2 == 0:
            r = _norm_modulate(xc, g2, sc2, sh2, mode="dense")
            y = _swiglu_ffn(r["h"], w1_dense, w3_dense, w2_dense,
                            jnp.full((s // tm,), j, jnp.int32), dense_rows, tm)
            pending = (y, gt2)
        else:
            r = _norm_modulate(xc, g2, sc2, sh2, mode="moe", w_router=w_router[j])
            pos, row_token, tile_expert, tile_rows, sub_rows = _moe_plan(
                r["idx"][:, :TOP_K], n_experts, tm, sub)
            xs = _gather_rows(r["h"], row_token, sub_rows, sub)
            ys = _swiglu_ffn(xs, w1_moe.reshape(moe_w13_shape), w3_moe.reshape(moe_w13_shape),
                             w2_moe.reshape(moe_w2_shape), tile_expert + j * n_experts, tile_rows, tm)
            xc = _combine_residual(xc, r["wts"], gt2, ys, pos)

    zero = jnp.zeros((1, d), F32)
    r = _norm_modulate(xc, g_final.reshape(1, d), zero, zero, mode="final", res=pending)
    return r["h"].reshape(b, s, d)
```

```python
import functools
import math

import numpy as np
import jax
import jax.numpy as jnp
from jax import lax
from jax.experimental import pallas as pl
from jax.experimental.pallas import tpu as pltpu

HEAD_DIM = 128
MOBA_BLOCK = 256
MOBA_TOPK = 3
REL_BUCKETS = 32
REL_MAX_DIST = 128
TOP_K = 2
RMS_EPS = 1e-6
NEG_INF = -1e30
BELOW_NEG_INF = -3e38
LOG2E = math.log2(math.e)
QK_SCALE_LOG2 = HEAD_DIM ** -0.5 * LOG2E

LANES = 128
SUBLANES = 8
VMEM_LIMIT_BYTES = 56 * 1024 * 1024

MXU_DTYPE = jnp.bfloat16
F32 = jnp.float32

ROW_TILE = 1024
FFN_ROW_TILE = 2048
FFN_F_TILE = 256
FFN_SUB_ROWS = 256
FFN_CHUNK_GROUP = 4
ATTN_TILE = 512
NORM_TILE = 512
COMBINE_TILE = 256
DMA_ISSUE_UNROLL = 8


def _cparams(semantics):
    return pltpu.CompilerParams(dimension_semantics=semantics,
                                vmem_limit_bytes=VMEM_LIMIT_BYTES)


def _largest_divisor(n, candidates):
    for c in candidates:
        if n % c == 0:
            return c
    raise ValueError(f"no tile in {candidates} divides {n}")


def _ada_kernel(c_ref, w_ref, b_ref, o_ref, ca_ref):
    c = c_ref[...]
    ca_ref[...] = c * jax.nn.sigmoid(c)
    d, tn = w_ref.shape[1], w_ref.shape[2]
    ch = min(d, 256)

    def body(r, acc):
        rows = pl.ds(pl.multiple_of(r * ch, ch), ch)
        prod = w_ref[0, rows, :] * ca_ref[rows, :]
        return acc + jnp.sum(prod.reshape(ch // SUBLANES, SUBLANES, tn), axis=0)

    acc = lax.fori_loop(0, d // ch, body, jnp.zeros((SUBLANES, tn), F32))
    o_ref[0] = jnp.sum(acc, axis=0, keepdims=True) + b_ref[0]


def _ada_modulation(c, w_ada, b_ada):
    depth, d, n = w_ada.shape
    tn = _largest_divisor(n, (1024, 512, 256, 128))
    return pl.pallas_call(
        _ada_kernel,
        out_shape=jax.ShapeDtypeStruct((depth, 1, n), F32),
        grid=(depth, n // tn),
        in_specs=[pl.BlockSpec((d, 1), lambda i, j: (0, 0)),
                  pl.BlockSpec((1, d, tn), lambda i, j: (i, 0, j)),
                  pl.BlockSpec((1, 1, tn), lambda i, j: (i, 0, j))],
        out_specs=pl.BlockSpec((1, 1, tn), lambda i, j: (i, 0, j)),
        scratch_shapes=[pltpu.VMEM((d, 1), F32)],
        compiler_params=_cparams(("arbitrary", "arbitrary")),
        name="ada_modulation",
    )(c.reshape(d, 1), w_ada, b_ada.reshape(depth, 1, n))


def _norm_kernel(*refs, has_res, mode, n_experts):
    refs = list(refs)
    x_ref = refs.pop(0)
    if has_res:
        y_ref, gt_ref = refs.pop(0), refs.pop(0)
    g_ref, sc_ref, sh_ref = refs.pop(0), refs.pop(0), refs.pop(0)
    if mode == "moe":
        wr_ref = refs.pop(0)
    if mode == "fox":
        wf_ref, bf_ref = refs.pop(0), refs.pop(0)
    if has_res:
        xo_ref = refs.pop(0)
    h_ref = refs.pop(0)

    x = x_ref[...]
    if has_res:
        x = x + gt_ref[...] * y_ref[...]
        xo_ref[...] = x
    ms = jnp.mean(x * x, axis=-1, keepdims=True)
    h = (x * lax.rsqrt(ms + RMS_EPS) * g_ref[...]) * (1.0 + sc_ref[...]) + sh_ref[...]
    h_ref[...] = h.astype(h_ref.dtype)

    if mode == "moe":
        idx_ref, wts_ref = refs.pop(0), refs.pop(0)
        logits = jnp.dot(h.astype(MXU_DTYPE), wr_ref[...], preferred_element_type=F32)
        lane = lax.broadcasted_iota(jnp.int32, logits.shape, 1)
        lg = jnp.where(lane < n_experts, logits, BELOW_NEG_INF)
        v0 = jnp.max(lg, axis=-1, keepdims=True)
        i0 = jnp.min(jnp.where(lg == v0, lane, LANES), axis=-1, keepdims=True)
        lg = jnp.where(lane == i0, BELOW_NEG_INF, lg)
        v1 = jnp.max(lg, axis=-1, keepdims=True)
        i1 = jnp.min(jnp.where(lg == v1, lane, LANES), axis=-1, keepdims=True)
        e1 = jnp.exp(v1 - v0)
        den = 1.0 + e1
        idx_ref[...] = jnp.where(lane == 0, i0, jnp.where(lane == 1, i1, 0))
        wts_ref[...] = jnp.where(lane == 0, 1.0 / den, jnp.where(lane == 1, e1 / den, 0.0))

    if mode == "fox":
        f_ref, carry_ref = refs.pop(0), refs.pop(0)

        @pl.when(pl.program_id(0) == 0)
        def _():
            carry_ref[...] = jnp.zeros_like(carry_ref)

        tm = x.shape[0]
        z = lax.dot_general(wf_ref[...], h.astype(MXU_DTYPE), (((1,), (1,)), ((), ())),
                            preferred_element_type=F32) + bf_ref[...]
        log_f = jnp.minimum(z, 0.0) - jnp.log1p(jnp.exp(-jnp.abs(z)))
        upper = (lax.broadcasted_iota(jnp.int32, (tm, tm), 0)
                 <= lax.broadcasted_iota(jnp.int32, (tm, tm), 1)).astype(F32)
        cum = jnp.dot(log_f, upper, preferred_element_type=F32,
                      precision=lax.Precision.HIGHEST) + carry_ref[...]
        f_ref[0] = cum * LOG2E
        carry_ref[...] = cum[:, tm - 1:tm]


def _norm_modulate(x, g, sc, sh, *, mode, res=None, w_router=None, w_f=None, b_f=None):
    s, d = x.shape
    tm = min(NORM_TILE, s)
    row = pl.BlockSpec((tm, d), lambda i: (i, 0))
    vec = pl.BlockSpec((1, d), lambda i: (0, 0))
    args, in_specs, out_shape, out_specs, scratch = [x], [row], [], [], []
    if res is not None:
        y, gt = res
        args += [y, gt]
        in_specs += [row, vec]
        out_shape.append(jax.ShapeDtypeStruct((s, d), F32))
        out_specs.append(row)
    args += [g, sc, sh]
    in_specs += [vec, vec, vec]
    h_dtype = {"attn": MXU_DTYPE, "fox": MXU_DTYPE, "dense": MXU_DTYPE,
               "moe": F32, "final": F32}[mode]
    out_shape.append(jax.ShapeDtypeStruct((s, d), h_dtype))
    out_specs.append(row)
    n_experts = 0
    if mode == "moe":
        n_experts = w_router.shape[1]
        wr = jnp.zeros((d, LANES), MXU_DTYPE).at[:, :n_experts].set(w_router.astype(MXU_DTYPE))
        args.append(wr)
        in_specs.append(pl.BlockSpec((d, LANES), lambda i: (0, 0)))
        lane_blk = pl.BlockSpec((tm, LANES), lambda i: (i, 0))
        out_shape += [jax.ShapeDtypeStruct((s, LANES), jnp.int32),
                      jax.ShapeDtypeStruct((s, LANES), F32)]
        out_specs += [lane_blk, lane_blk]
    if mode == "fox":
        nh = w_f.shape[1]
        args += [w_f.T.astype(MXU_DTYPE), b_f.reshape(nh, 1)]
        in_specs += [pl.BlockSpec((nh, d), lambda i: (0, 0)),
                     pl.BlockSpec((nh, 1), lambda i: (0, 0))]
        out_shape.append(jax.ShapeDtypeStruct((s // tm, nh, tm), F32))
        out_specs.append(pl.BlockSpec((1, nh, tm), lambda i: (i, 0, 0)))
        scratch.append(pltpu.VMEM((nh, 1), F32))
    outs = pl.pallas_call(
        functools.partial(_norm_kernel, has_res=res is not None, mode=mode, n_experts=n_experts),
        out_shape=out_shape,
        grid=(s // tm,),
        in_specs=in_specs,
        out_specs=out_specs,
        scratch_shapes=scratch,
        compiler_params=_cparams(("arbitrary",)),
        name=f"norm_{mode}",
    )(*args)
    outs = list(outs)
    result = {}
    if res is not None:
        result["x"] = outs.pop(0)
    result["h"] = outs.pop(0)
    if mode == "moe":
        result["idx"], result["wts"] = outs.pop(0), outs.pop(0)
    if mode == "fox":
        result["f"] = outs.pop(0)
    return result


def _qkv_kernel(a_ref, w_ref, cs_ref, o_ref, wb_ref):
    @pl.when(pl.program_id(1) == 0)
    def _():
        wb_ref[...] = w_ref[0].astype(wb_ref.dtype)

    acc = jnp.dot(a_ref[...], wb_ref[...], preferred_element_type=F32) * cs_ref[...]
    for j in range(o_ref.shape[0]):
        o_ref[j] = acc[:, j * HEAD_DIM:(j + 1) * HEAD_DIM].astype(o_ref.dtype)


def _qkv_projection(h, w, layer):
    s, d = h.shape
    n = w.shape[2]
    n_heads3 = n // HEAD_DIM
    tm = min(ROW_TILE, s)
    nh = _largest_divisor(n_heads3, (12, 6, 4, 3, 2, 1))
    tn = nh * HEAD_DIM
    col_scale = jnp.where(jnp.arange(n) < n // 3, QK_SCALE_LOG2, 1.0).astype(F32).reshape(1, n)
    return pl.pallas_call(
        _qkv_kernel,
        out_shape=jax.ShapeDtypeStruct((n_heads3, s, HEAD_DIM), MXU_DTYPE),
        grid=(n // tn, s // tm),
        in_specs=[pl.BlockSpec((tm, d), lambda j, i: (i, 0)),
                  pl.BlockSpec((1, d, tn), lambda j, i: (layer, 0, j)),
                  pl.BlockSpec((1, tn), lambda j, i: (0, j))],
        out_specs=pl.BlockSpec((nh, tm, HEAD_DIM), lambda j, i: (j, i, 0)),
        scratch_shapes=[pltpu.VMEM((d, tn), MXU_DTYPE)],
        compiler_params=_cparams(("arbitrary", "arbitrary")),
        name="qkv_projection",
    )(h, w, col_scale)


def _out_proj_kernel(a_ref, w_ref, x_ref, gt_ref, o_ref, wb_ref):
    @pl.when(pl.program_id(1) == 0)
    def _():
        wb_ref[...] = w_ref[0].astype(wb_ref.dtype)

    acc = jnp.dot(a_ref[...], wb_ref[...], preferred_element_type=F32)
    o_ref[...] = x_ref[...] + gt_ref[...] * acc


def _out_projection_residual(a, w, layer, x, gt):
    s, k = a.shape
    n = w.shape[2]
    tm = min(ROW_TILE, s)
    tn = _largest_divisor(n, (1024, 512, 256, 128))
    return pl.pallas_call(
        _out_proj_kernel,
        out_shape=jax.ShapeDtypeStruct((s, n), F32),
        grid=(n // tn, s // tm),
        in_specs=[pl.BlockSpec((tm, k), lambda j, i: (i, 0)),
                  pl.BlockSpec((1, k, tn), lambda j, i: (layer, 0, j)),
                  pl.BlockSpec((tm, tn), lambda j, i: (i, j)),
                  pl.BlockSpec((1, tn), lambda j, i: (0, j))],
        out_specs=pl.BlockSpec((tm, tn), lambda j, i: (i, j)),
        scratch_shapes=[pltpu.VMEM((k, tn), MXU_DTYPE)],
        compiler_params=_cparams(("arbitrary", "arbitrary")),
        name="out_projection",
    )(a, w, x, gt)


def _lane_tiles(x):
    return [x[:, c * LANES:(c + 1) * LANES] for c in range(x.shape[1] // LANES)]


def _softmax_init(m_ref, acc_ref):
    m_ref[...] = jnp.full(m_ref.shape, BELOW_NEG_INF, F32)
    acc_ref[...] = jnp.zeros_like(acc_ref)


def _softmax_step(s, v_aug, m_ref, acc_ref):
    tiles = _lane_tiles(s)
    m_cur = jnp.max(functools.reduce(jnp.maximum, tiles), axis=-1, keepdims=True)
    m_old = m_ref[...]
    m_new = jnp.maximum(m_old, m_cur)
    p = jnp.concatenate([jnp.exp2(t - m_new) for t in tiles], axis=1).astype(MXU_DTYPE)
    pv = jnp.dot(p, v_aug, preferred_element_type=F32)
    alpha = jnp.exp2(m_old - m_new)
    acc_ref[...] = jnp.concatenate([alpha] * (acc_ref.shape[1] // LANES), axis=1) * acc_ref[...] + pv
    m_ref[...] = m_new


def _pipelined_tiles(n_plain, tail, qk, consume, s_ref, loop=True):
    s_ref[0] = qk(0)

    def step(j, cur, kind, prefetch=True):
        s = s_ref[cur]
        if prefetch:
            s_ref[1 - cur] = qk(j + 1)
        consume(j, s, kind)

    def pair(i, carry):
        step(2 * i, 0, None)
        step(2 * i + 1, 1, None)
        return carry

    def run_tail(cur):
        for k, kind in enumerate(tail):
            step(n_plain + k, cur, kind, prefetch=k + 1 < len(tail))
            cur = 1 - cur

    if not loop:
        run_tail(0)
        return
    lax.fori_loop(0, n_plain // 2, pair, 0)

    @pl.when(n_plain % 2 == 0)
    def _():
        run_tail(0)

    @pl.when(n_plain % 2 == 1)
    def _():
        step(n_plain - 1, 0, None)
        run_tail(1)


def _augment_values(v_ref, vaug_ref):
    hd = v_ref.shape[-1]
    for j in range(v_ref.shape[1]):
        vaug_ref[j, :, :hd] = v_ref[0, j]
        vaug_ref[j, :, hd:] = jnp.ones((v_ref.shape[2], hd), vaug_ref.dtype)


def _fox_kernel(q_ref, k_ref, v_ref, f_ref, o_ref, vaug_ref, s_ref, m_ref, acc_ref):
    i = pl.program_id(1)
    hd = q_ref.shape[2]

    @pl.when(i == 0)
    def _():
        _augment_values(v_ref, vaug_ref)

    q = q_ref[0]
    tq = q.shape[0]

    def qk(j):
        return lax.dot_general(q, k_ref[0, j], (((1,), (1,)), ((), ())),
                               preferred_element_type=F32)

    def consume(j, s, kind):
        s = s - f_ref[0, j]
        if kind == "diagonal":
            causal = (lax.broadcasted_iota(jnp.int32, (tq, tq), 1)
                      <= lax.broadcasted_iota(jnp.int32, (tq, tq), 0))
            s = jnp.where(causal, s, NEG_INF)
        _softmax_step(s, vaug_ref[j], m_ref, acc_ref)

    _softmax_init(m_ref, acc_ref)
    _pipelined_tiles(i, ["diagonal"], qk, consume, s_ref)

    acc = acc_ref[...]
    o_ref[...] = (acc[:, :hd] / acc[:, hd:]).astype(o_ref.dtype)


def _fox_attention(qkv, f_rows, n_heads):
    _, s, hd = qkv.shape
    t = min(ATTN_TILE, s)
    nb = s // t
    kv = qkv.reshape(3 * n_heads, nb, t, hd)
    return pl.pallas_call(
        _fox_kernel,
        out_shape=jax.ShapeDtypeStruct((s, n_heads * hd), MXU_DTYPE),
        grid=(n_heads, nb),
        in_specs=[pl.BlockSpec((1, t, hd), lambda h, i: (h, i, 0)),
                  pl.BlockSpec((1, nb, t, hd), lambda h, i: (n_heads + h, 0, 0, 0)),
                  pl.BlockSpec((1, nb, t, hd), lambda h, i: (2 * n_heads + h, 0, 0, 0)),
                  pl.BlockSpec((1, nb, 1, t), lambda h, i: (h, 0, 0, 0))],
        out_specs=pl.BlockSpec((t, hd), lambda h, i: (i, h)),
        scratch_shapes=[pltpu.VMEM((nb, t, 2 * hd), MXU_DTYPE),
                        pltpu.VMEM((2, t, t), F32),
                        pltpu.VMEM((t, LANES), F32),
                        pltpu.VMEM((t, 2 * hd), F32)],
        compiler_params=_cparams(("arbitrary", "arbitrary")),
        name="fox_attention",
    )(qkv, kv, kv, f_rows)


def _rel_bucket_np(dist):
    n = np.maximum(dist, 0)
    max_exact = REL_BUCKETS // 2
    nf = np.maximum(n, 1).astype(np.float32)
    large = max_exact + (np.log(nf / np.float32(max_exact))
                         / np.float32(math.log(REL_MAX_DIST / max_exact))
                         * np.float32(REL_BUCKETS - max_exact)).astype(np.int32)
    large = np.minimum(large, REL_BUCKETS - 1)
    return np.where(n < max_exact, n, large).astype(np.int32)


def _bias_table_kernel(rb_ref, idx_ref, o_ref, *, n_heads):
    h = pl.program_id(0)
    idx = idx_ref[0]
    bias = jnp.zeros(idx.shape, F32)
    far = rb_ref[(REL_BUCKETS - 1) * n_heads + h]
    for b in range(REL_BUCKETS):
        bias = jnp.where(idx == b, (rb_ref[b * n_heads + h] - far) * LOG2E, bias)
    o_ref[0, 0] = jnp.where(idx < 0, NEG_INF, bias)


def _moba_bias_tiles(rel_bias, t):
    n_heads = rel_bias.shape[1]
    dist = np.arange(2)[:, None, None] * t + np.arange(t)[None, :, None] - np.arange(t)[None, None, :]
    idx = jnp.asarray(np.where(dist < 0, -1, _rel_bucket_np(dist)).astype(np.int32))
    return pl.pallas_call(
        functools.partial(_bias_table_kernel, n_heads=n_heads),
        out_shape=jax.ShapeDtypeStruct((n_heads, 2, t, t), F32),
        grid_spec=pltpu.PrefetchScalarGridSpec(
            num_scalar_prefetch=1, grid=(n_heads, 2),
            in_specs=[pl.BlockSpec((1, t, t), lambda h, dt, rb: (dt, 0, 0))],
            out_specs=pl.BlockSpec((1, 1, t, t), lambda h, dt, rb: (h, dt, 0, 0))),
        compiler_params=_cparams(("arbitrary", "arbitrary")),
        name="moba_bias_tiles",
    )(rel_bias.reshape(-1), idx)


def _split3(x):
    hi = x.astype(MXU_DTYPE)
    r1 = x - hi.astype(F32)
    mid = r1.astype(MXU_DTYPE)
    lo = (r1 - mid.astype(F32)).astype(MXU_DTYPE)
    return hi, mid, lo


def _moba_kernel(q_ref, k_ref, v_ref, bias_ref, far_ref, o_ref,
                 vaug_ref, kaug_ref, qaug_ref, kmean_ref, selt_ref, s_ref, m_ref, acc_ref):
    t = pl.program_id(1)
    n_tiles, tq, hd = k_ref.shape[1], k_ref.shape[2], k_ref.shape[3]
    bpt = tq // MOBA_BLOCK
    n_blocks = n_tiles * bpt
    nbp = kmean_ref.shape[0]
    far_lane0 = LANES - 3

    @pl.when(t == 0)
    def _():
        _augment_values(v_ref, vaug_ref)
        kmean_ref[...] = jnp.zeros_like(kmean_ref)
        selt_ref[...] = jnp.zeros_like(selt_ref)
        lane = lax.broadcasted_iota(jnp.int32, (tq, LANES), 1)
        row_blk = lax.broadcasted_iota(jnp.int32, (tq, LANES), 0) // MOBA_BLOCK
        for j in range(n_tiles):
            kaug_ref[j, :, :hd] = k_ref[0, j]
            kaug_ref[j, :, hd:] = ((lane == j * bpt + row_blk) | (lane >= far_lane0)).astype(kaug_ref.dtype)
        for n in range(n_blocks):
            rows = slice((n % bpt) * MOBA_BLOCK, (n % bpt + 1) * MOBA_BLOCK)
            kmean_ref[n:n + 1, :] = jnp.mean(k_ref[0, n // bpt, rows, :].astype(F32), axis=0,
                                             keepdims=True)

    q = q_ref[0]

    gate = lax.dot_general(kmean_ref[...].astype(MXU_DTYPE), q, (((1,), (1,)), ((), ())),
                           preferred_element_type=F32)
    blk = lax.broadcasted_iota(jnp.int32, gate.shape, 0)
    own = t * bpt + lax.broadcasted_iota(jnp.int32, gate.shape, 1) // MOBA_BLOCK
    past = blk < own
    g = jnp.where(past, gate, NEG_INF)
    sel = jnp.where(blk == own, 0.0, NEG_INF)
    for _ in range(MOBA_TOPK):
        mx = jnp.max(g, axis=0, keepdims=True)
        pick = blk == jnp.min(jnp.where(g == mx, blk, nbp), axis=0, keepdims=True)
        sel = jnp.where(pick & past, 0.0, sel)
        g = jnp.where(pick, BELOW_NEG_INF, g)
    selt_ref[0:nbp, :] = sel
    far_hi, far_mid, far_lo = _split3(far_ref[0])
    lane = lax.broadcasted_iota(jnp.int32, (tq, LANES), 1)
    extra = selt_ref[...].T.astype(MXU_DTYPE)
    extra = jnp.where(lane == far_lane0, far_hi,
                      jnp.where(lane == far_lane0 + 1, far_mid,
                                jnp.where(lane == far_lane0 + 2, far_lo, extra)))
    qaug_ref[:, :hd] = q
    qaug_ref[:, hd:] = extra
    q_aug = qaug_ref[...]

    def qk(j):
        return lax.dot_general(q_aug, kaug_ref[j], (((1,), (1,)), ((), ())),
                               preferred_element_type=F32)

    def consume(j, s, kind):
        if kind == "own":
            s = s + bias_ref[0, 0]
        elif kind == "previous":
            s = s + bias_ref[0, 1]
        _softmax_step(s, vaug_ref[j], m_ref, acc_ref)

    _softmax_init(m_ref, acc_ref)

    @pl.when(t == 0)
    def _():
        _pipelined_tiles(0, ["own"], qk, consume, s_ref, loop=False)

    @pl.when(t >= 1)
    def _():
        _pipelined_tiles(t - 1, ["previous", "own"], qk, consume, s_ref)

    acc = acc_ref[...]
    o_ref[...] = (acc[:, :hd] / acc[:, hd:]).astype(o_ref.dtype)


def _moba_attention(qkv, bias_tiles, rel_bias, n_heads):
    _, s, hd = qkv.shape
    t = bias_tiles.shape[2]
    assert s % t == 0 and t % MOBA_BLOCK == 0 and t >= REL_MAX_DIST and hd == LANES
    assert s // MOBA_BLOCK <= LANES - 3
    n_tiles = s // t
    nbp = -(-(s // MOBA_BLOCK) // SUBLANES) * SUBLANES
    kv = qkv.reshape(3 * n_heads, n_tiles, t, hd)
    far = jnp.broadcast_to(rel_bias[REL_BUCKETS - 1][:, None, None] * LOG2E, (n_heads, 1, LANES))
    return pl.pallas_call(
        _moba_kernel,
        out_shape=jax.ShapeDtypeStruct((s, n_heads * hd), MXU_DTYPE),
        grid=(n_heads, n_tiles),
        in_specs=[pl.BlockSpec((1, t, hd), lambda h, i: (h, i, 0)),
                  pl.BlockSpec((1, n_tiles, t, hd), lambda h, i: (n_heads + h, 0, 0, 0)),
                  pl.BlockSpec((1, n_tiles, t, hd), lambda h, i: (2 * n_heads + h, 0, 0, 0)),
                  pl.BlockSpec((1, 2, t, t), lambda h, i: (h, 0, 0, 0)),
                  pl.BlockSpec((1, 1, LANES), lambda h, i: (h, 0, 0))],
        out_specs=pl.BlockSpec((t, hd), lambda h, i: (i, h)),
        scratch_shapes=[pltpu.VMEM((n_tiles, t, 2 * hd), MXU_DTYPE),
                        pltpu.VMEM((n_tiles, t, 2 * hd), MXU_DTYPE),
                        pltpu.VMEM((t, 2 * hd), MXU_DTYPE),
                        pltpu.VMEM((nbp, hd), F32),
                        pltpu.VMEM((LANES, t), F32),
                        pltpu.VMEM((2, t, t), F32),
                        pltpu.VMEM((t, LANES), F32),
                        pltpu.VMEM((t, 2 * hd), F32)],
        compiler_params=_cparams(("arbitrary", "arbitrary")),
        name="moba_attention",
    )(qkv, kv, kv, bias_tiles, far)


def _ffn_kernel(te_ref, tr_ref, x_ref, w1a_ref, w1z_ref, w3a_ref, w3z_ref, w2a_ref, w2z_ref, o_ref,
                w1b_ref, w3b_ref, w2b_ref):
    t = pl.program_id(0)
    f = pl.program_id(1)
    rows = tr_ref[t]
    tm = x_ref.shape[0]
    sub = min(FFN_SUB_ROWS, tm)
    group = min(FFN_CHUNK_GROUP, tm // sub)

    def cast_weights():
        for halves, dst in (((w1a_ref, w1z_ref), w1b_ref), ((w3a_ref, w3z_ref), w3b_ref),
                            ((w2a_ref, w2z_ref), w2b_ref)):
            half = dst.shape[0] // 2
            dst[0:half, :] = halves[0][0].astype(dst.dtype)
            dst[half:, :] = halves[1][0].astype(dst.dtype)

    def gate_up(r):
        x = x_ref[r, :]
        a = jnp.dot(x, w1b_ref[...], preferred_element_type=F32)
        b = jnp.dot(x, w3b_ref[...], preferred_element_type=F32)
        return (a * jax.nn.sigmoid(a) * b).astype(MXU_DTYPE)

    def down(g, r):
        o_ref[r, :] += jnp.dot(g, w2b_ref[...], preferred_element_type=F32)

    @pl.when(f == 0)
    def _():
        o_ref[...] = jnp.zeros_like(o_ref)

    n_chunks = (rows + sub - 1) // sub
    n_groups = n_chunks // group

    def run_chunks(first, count):
        for c in range(count):
            r = pl.ds(pl.multiple_of((first + c) * sub, sub), sub)
            down(gate_up(r), r)

    @pl.when(n_chunks > 0)
    def _():
        cast_weights()

        def body(i, carry):
            run_chunks(i * group, group)
            return carry

        lax.fori_loop(0, n_groups, body, 0)

    for rem in range(1, group):
        @pl.when(n_chunks % group == rem)
        def _(rem=rem):
            run_chunks(n_groups * group, rem)


def _swiglu_ffn(xs, w1, w3, w2, tile_expert, tile_rows, tm):
    r, d = xs.shape
    ff = w1.shape[2]
    tf = _largest_divisor(ff, (FFN_F_TILE, 128))
    nf = ff // tf

    def f_eff(t, f, tr):
        return jnp.where(tr[t] > 0, f, nf - 1)

    def gate_up_half(k):
        return pl.BlockSpec((1, d // 2, tf), lambda t, f, te, tr: (te[t], k, f_eff(t, f, tr)))

    def down_half(k):
        return pl.BlockSpec((1, tf // 2, d), lambda t, f, te, tr: (te[t], 2 * f_eff(t, f, tr) + k, 0))

    return pl.pallas_call(
        _ffn_kernel,
        out_shape=jax.ShapeDtypeStruct((r, d), F32),
        grid_spec=pltpu.PrefetchScalarGridSpec(
            num_scalar_prefetch=2, grid=(r // tm, nf),
            in_specs=[pl.BlockSpec((tm, d), lambda t, f, te, tr: (t, 0), pipeline_mode=pl.Buffered(1)),
                      gate_up_half(0), gate_up_half(1), gate_up_half(0), gate_up_half(1),
                      down_half(0), down_half(1)],
            out_specs=pl.BlockSpec((tm, d), lambda t, f, te, tr: (t, 0), pipeline_mode=pl.Buffered(1)),
            scratch_shapes=[pltpu.VMEM((d, tf), MXU_DTYPE), pltpu.VMEM((d, tf), MXU_DTYPE),
                            pltpu.VMEM((tf, d), MXU_DTYPE)]),
        compiler_params=_cparams(("arbitrary", "arbitrary")),
        name="swiglu_ffn",
    )(tile_expert, tile_rows, xs, w1, w1, w3, w3, w2, w2)


def _gather_kernel(tok_ref, nrows_ref, h_ref, o_ref, buf_ref, sem):
    t = pl.program_id(0)
    n_chunks = pl.num_programs(0)
    sub = o_ref.shape[0]
    slot = t % 2

    def request(chunk, into):
        def start(r, carry):
            tok = tok_ref[chunk * sub + r]
            pltpu.make_async_copy(h_ref.at[pl.ds(tok, 1)], buf_ref.at[into, pl.ds(r, 1)],
                                  sem.at[into]).start()
            return carry

        lax.fori_loop(0, sub, start, 0, unroll=DMA_ISSUE_UNROLL)

    @pl.when((t == 0) & (nrows_ref[0] > 0))
    def _():
        request(0, 0)

    nxt = jnp.minimum(t + 1, n_chunks - 1)

    @pl.when((t + 1 < n_chunks) & (nrows_ref[nxt] > 0))
    def _():
        request(nxt, 1 - slot)

    @pl.when(nrows_ref[t] > 0)
    def _():
        pltpu.make_async_copy(h_ref.at[pl.ds(0, sub)], buf_ref.at[slot], sem.at[slot]).wait()
        o_ref[...] = buf_ref[slot].astype(o_ref.dtype)

    @pl.when(nrows_ref[t] == 0)
    def _():
        o_ref[...] = jnp.zeros_like(o_ref)


def _gather_rows(h, row_token, sub_rows, sub):
    s, d = h.shape
    r = row_token.shape[0]
    return pl.pallas_call(
        _gather_kernel,
        out_shape=jax.ShapeDtypeStruct((r, d), MXU_DTYPE),
        grid_spec=pltpu.PrefetchScalarGridSpec(
            num_scalar_prefetch=2, grid=(r // sub,),
            in_specs=[pl.BlockSpec(memory_space=pl.ANY)],
            out_specs=pl.BlockSpec((sub, d), lambda t, tok, nr: (t, 0)),
            scratch_shapes=[pltpu.VMEM((2, sub, d), F32), pltpu.SemaphoreType.DMA((2,))]),
        compiler_params=_cparams(("arbitrary",)),
        name="moe_gather",
    )(row_token, sub_rows, h)


def _combine_kernel(pos_ref, x_ref, wts_ref, gt_ref, ys_ref, o_ref, buf_ref, sem):
    t = pl.program_id(0)
    n_tiles = pl.num_programs(0)
    tm = x_ref.shape[0]
    slot = t % 2

    def request(tile, into):
        def start(r, carry):
            for k in range(TOP_K):
                src = pos_ref[(tile * tm + r) * TOP_K + k]
                pltpu.make_async_copy(ys_ref.at[pl.ds(src, 1)], buf_ref.at[into, k, pl.ds(r, 1)],
                                      sem.at[into]).start()
            return carry

        lax.fori_loop(0, tm, start, 0, unroll=DMA_ISSUE_UNROLL)

    @pl.when(t == 0)
    def _():
        request(0, 0)

    @pl.when(t + 1 < n_tiles)
    def _():
        request(t + 1, 1 - slot)

    for k in range(TOP_K):
        pltpu.make_async_copy(ys_ref.at[pl.ds(0, tm)], buf_ref.at[slot, k], sem.at[slot]).wait()
    wts = wts_ref[...]
    y = wts[:, 0:1] * buf_ref[slot, 0]
    for k in range(1, TOP_K):
        y = y + wts[:, k:k + 1] * buf_ref[slot, k]
    o_ref[...] = x_ref[...] + gt_ref[...] * y


def _combine_residual(x, wts, gt, ys, pos):
    s, d = x.shape
    tm = min(COMBINE_TILE, s)
    return pl.pallas_call(
        _combine_kernel,
        out_shape=jax.ShapeDtypeStruct((s, d), F32),
        grid_spec=pltpu.PrefetchScalarGridSpec(
            num_scalar_prefetch=1, grid=(s // tm,),
            in_specs=[pl.BlockSpec((tm, d), lambda t, pos: (t, 0)),
                      pl.BlockSpec((tm, LANES), lambda t, pos: (t, 0)),
                      pl.BlockSpec((1, d), lambda t, pos: (0, 0)),
                      pl.BlockSpec(memory_space=pl.ANY)],
            out_specs=pl.BlockSpec((tm, d), lambda t, pos: (t, 0)),
            scratch_shapes=[pltpu.VMEM((2, TOP_K, tm, d), F32), pltpu.SemaphoreType.DMA((2,))]),
        compiler_params=_cparams(("arbitrary",)),
        name="moe_combine",
    )(pos, x, wts, gt, ys)


def _moe_plan(idx, n_experts, tm, sub):
    s = idx.shape[0]
    n_slots = s * TOP_K
    e_flat = idx.reshape(-1)
    onehot = (e_flat[:, None] == jnp.arange(n_experts, dtype=jnp.int32)[None, :]).astype(jnp.int32)
    cnt = jnp.sum(onehot, axis=0)
    rank = jnp.sum((jnp.cumsum(onehot, axis=0) - onehot) * onehot, axis=1)
    ntile_e = (cnt + tm - 1) // tm
    per_e = (cnt + ntile_e * sub - 1) // jnp.maximum(ntile_e * sub, 1) * sub
    per_e = jnp.maximum(per_e, sub)
    tile_end = jnp.cumsum(ntile_e)
    tile_start = tile_end - ntile_e
    per_slot = per_e[e_flat]
    pos = (tile_start[e_flat] + rank // per_slot) * tm + rank % per_slot
    n_tiles = n_slots // tm + n_experts
    row_token = jnp.zeros((n_tiles * tm,), jnp.int32).at[pos].set(
        jnp.arange(n_slots, dtype=jnp.int32) // TOP_K)
    t_ids = jnp.arange(n_tiles, dtype=jnp.int32)
    te = jnp.sum((t_ids[:, None] >= tile_end[None, :]).astype(jnp.int32), axis=1)
    used = te < n_experts
    te_c = jnp.minimum(te, n_experts - 1)
    rows = jnp.where(used, jnp.clip(cnt[te_c] - (t_ids - tile_start[te_c]) * per_e[te_c], 0, per_e[te_c]), 0)
    e_last = jnp.max(jnp.where(ntile_e > 0, jnp.arange(n_experts, dtype=jnp.int32), 0))
    tile_expert = jnp.where(used, te_c, e_last).astype(jnp.int32)
    per = tm // sub
    s_ids = jnp.arange(n_tiles * per, dtype=jnp.int32)
    sub_rows = jnp.clip(rows[s_ids // per] - (s_ids % per) * sub, 0, sub)
    return pos.astype(jnp.int32), row_token, tile_expert, rows.astype(jnp.int32), sub_rows.astype(jnp.int32)


def kernel(x, c, w_ada, b_ada, g_mix, g_ffn, g_final, rel_bias, w_qkv_fox, w_f_fox, b_f_fox, w_o_fox, w_qkv_moba, w_o_moba, w1_dense, w3_dense, w2_dense, w_router, w1_moe, w3_moe, w2_moe):
    b, s, d = x.shape
    assert b == 1 and d % HEAD_DIM == 0
    depth = w_ada.shape[0]
    n_heads = d // HEAD_DIM
    n_experts = w_router.shape[2]
    tm = min(FFN_ROW_TILE, s)
    sub = min(FFN_SUB_ROWS, tm)

    mod = _ada_modulation(c, w_ada, b_ada)
    bias_tiles = _moba_bias_tiles(rel_bias, min(ATTN_TILE, s))
    dense_rows = jnp.full((s // tm,), tm, jnp.int32)
    moe_w13_shape = (-1,) + w1_moe.shape[2:]
    moe_w2_shape = (-1,) + w2_moe.shape[2:]

    xc = x.reshape(s, d)
    pending = None
    for i in range(depth):
        j = i // 2
        sh1, sc1, gt1, sh2, sc2, gt2 = [mod[i, :, k * d:(k + 1) * d] for k in range(6)]
        g1 = g_mix[i].reshape(1, d)
        g2 = g_ffn[i].reshape(1, d)

        if i % 2 == 0:
            r = _norm_modulate(xc, g1, sc1, sh1, mode="fox", res=pending,
                               w_f=w_f_fox[j], b_f=b_f_fox[j])
        else:
            r = _norm_modulate(xc, g1, sc1, sh1, mode="attn", res=pending)
        xc = r.get("x", xc)
        pending = None
        if i % 2 == 0:
            qkv = _qkv_projection(r["h"], w_qkv_fox, j)
            nt = r["f"].shape[0]
            f_rows = r["f"].transpose(1, 0, 2).reshape(n_heads, nt, 1, r["f"].shape[2])
            o = _fox_attention(qkv, f_rows, n_heads)
            xc = _out_projection_residual(o, w_o_fox, j, xc, gt1)
        else:
            qkv = _qkv_projection(r["h"], w_qkv_moba, j)
            o = _moba_attention(qkv, bias_tiles, rel_bias, n_heads)
            xc = _out_projection_residual(o, w_o_moba, j, xc, gt1)

        if i % 2 == 0:
            r = _norm_modulate(xc, g2, sc2, sh2, mode="dense")
            y = _swiglu_ffn(r["h"], w1_dense, w3_dense, w2_dense,
                            jnp.full((s // tm,), j, jnp.int32), dense_rows, tm)
            pending = (y, gt2)
        else:
            r = _norm_modulate(xc, g2, sc2, sh2, mode="moe", w_router=w_router[j])
            pos, row_token, tile_expert, tile_rows, sub_rows = _moe_plan(
                r["idx"][:, :TOP_K], n_experts, tm, sub)
            xs = _gather_rows(r["h"], row_token, sub_rows, sub)
            ys = _swiglu_ffn(xs, w1_moe.reshape(moe_w13_shape), w3_moe.reshape(moe_w13_shape),
                             w2_moe.reshape(moe_w2_shape), tile_expert + j * n_experts, tile_rows, tm)
            xc = _combine_residual(xc, r["wts"], gt2, ys, pos)

    zero = jnp.zeros((1, d), F32)
    r = _norm_modulate(xc, g_final.reshape(1, d), zero, zero, mode="final", res=pending)
    return r["h"].reshape(b, s, d)
```

```python
import functools
import math

import numpy as np
import jax
import jax.numpy as jnp
from jax import lax
from jax.experimental import pallas as pl
from jax.experimental.pallas import tpu as pltpu

HEAD_DIM = 128
MOBA_BLOCK = 256
MOBA_TOPK = 3
REL_BUCKETS = 32
REL_MAX_DIST = 128
TOP_K = 2
RMS_EPS = 1e-6
NEG_INF = -1e30
BELOW_NEG_INF = -3e38
LOG2E = math.log2(math.e)
QK_SCALE_LOG2 = HEAD_DIM ** -0.5 * LOG2E

LANES = 128
SUBLANES = 8
VMEM_LIMIT_BYTES = 56 * 1024 * 1024

MXU_DTYPE = jnp.bfloat16
F32 = jnp.float32

ROW_TILE = 1024
FFN_F_TILE = 512
FFN_SUB_ROWS = 256
ATTN_TILE = 512
FOX_Q_TILE = 1024
NORM_TILE = 512
COMBINE_TILE = 256
DMA_ISSUE_UNROLL = 8


def _cparams(semantics):
    return pltpu.CompilerParams(dimension_semantics=semantics,
                                vmem_limit_bytes=VMEM_LIMIT_BYTES)


def _largest_divisor(n, candidates):
    for c in candidates:
        if n % c == 0:
            return c
    raise ValueError(f"no tile in {candidates} divides {n}")


def _ada_kernel(c_ref, w_ref, b_ref, o_ref, ca_ref):
    c = c_ref[...]
    ca_ref[...] = c * jax.nn.sigmoid(c)
    d, tn = w_ref.shape[1], w_ref.shape[2]
    ch = min(d, 256)

    def body(r, acc):
        rows = pl.ds(pl.multiple_of(r * ch, ch), ch)
        prod = w_ref[0, rows, :] * ca_ref[rows, :]
        return acc + jnp.sum(prod.reshape(ch // SUBLANES, SUBLANES, tn), axis=0)

    acc = lax.fori_loop(0, d // ch, body, jnp.zeros((SUBLANES, tn), F32))
    o_ref[0] = jnp.sum(acc, axis=0, keepdims=True) + b_ref[0]


def _ada_modulation(c, w_ada, b_ada):
    depth, d, n = w_ada.shape
    tn = _largest_divisor(n, (1024, 512, 256, 128))
    return pl.pallas_call(
        _ada_kernel,
        out_shape=jax.ShapeDtypeStruct((depth, 1, n), F32),
        grid=(depth, n // tn),
        in_specs=[pl.BlockSpec((d, 1), lambda i, j: (0, 0)),
                  pl.BlockSpec((1, d, tn), lambda i, j: (i, 0, j)),
                  pl.BlockSpec((1, 1, tn), lambda i, j: (i, 0, j))],
        out_specs=pl.BlockSpec((1, 1, tn), lambda i, j: (i, 0, j)),
        scratch_shapes=[pltpu.VMEM((d, 1), F32)],
        compiler_params=_cparams(("arbitrary", "arbitrary")),
        name="ada_modulation",
    )(c.reshape(d, 1), w_ada, b_ada.reshape(depth, 1, n))


def _norm_kernel(*refs, has_res, mode, n_experts):
    refs = list(refs)
    x_ref = refs.pop(0)
    if has_res:
        y_ref, gt_ref = refs.pop(0), refs.pop(0)
    g_ref, sc_ref, sh_ref = refs.pop(0), refs.pop(0), refs.pop(0)
    if mode == "moe":
        wr_ref = refs.pop(0)
    if mode == "fox":
        wf_ref, bf_ref = refs.pop(0), refs.pop(0)
    if has_res:
        xo_ref = refs.pop(0)
    h_ref = refs.pop(0)

    x = x_ref[...]
    if has_res:
        x = x + gt_ref[...] * y_ref[...]
        xo_ref[...] = x
    ms = jnp.mean(x * x, axis=-1, keepdims=True)
    h = (x * lax.rsqrt(ms + RMS_EPS) * g_ref[...]) * (1.0 + sc_ref[...]) + sh_ref[...]
    h_ref[...] = h.astype(h_ref.dtype)

    if mode == "moe":
        idx_ref, wts_ref = refs.pop(0), refs.pop(0)
        logits = jnp.dot(h.astype(MXU_DTYPE), wr_ref[...], preferred_element_type=F32)
        lane = lax.broadcasted_iota(jnp.int32, logits.shape, 1)
        lg = jnp.where(lane < n_experts, logits, BELOW_NEG_INF)
        v0 = jnp.max(lg, axis=-1, keepdims=True)
        i0 = jnp.min(jnp.where(lg == v0, lane, LANES), axis=-1, keepdims=True)
        lg = jnp.where(lane == i0, BELOW_NEG_INF, lg)
        v1 = jnp.max(lg, axis=-1, keepdims=True)
        i1 = jnp.min(jnp.where(lg == v1, lane, LANES), axis=-1, keepdims=True)
        e1 = jnp.exp(v1 - v0)
        den = 1.0 + e1
        idx_ref[...] = jnp.where(lane == 0, i0, jnp.where(lane == 1, i1, 0))
        wts_ref[...] = jnp.where(lane == 0, 1.0 / den, jnp.where(lane == 1, e1 / den, 0.0))

    if mode == "fox":
        f_ref, carry_ref = refs.pop(0), refs.pop(0)

        @pl.when(pl.program_id(0) == 0)
        def _():
            carry_ref[...] = jnp.zeros_like(carry_ref)

        tm = x.shape[0]
        z = lax.dot_general(wf_ref[...], h.astype(MXU_DTYPE), (((1,), (1,)), ((), ())),
                            preferred_element_type=F32) + bf_ref[...]
        log_f = jnp.minimum(z, 0.0) - jnp.log1p(jnp.exp(-jnp.abs(z)))
        upper = (lax.broadcasted_iota(jnp.int32, (tm, tm), 0)
                 <= lax.broadcasted_iota(jnp.int32, (tm, tm), 1)).astype(F32)
        cum = jnp.dot(log_f, upper, preferred_element_type=F32,
                      precision=lax.Precision.HIGHEST) + carry_ref[...]
        f_ref[0] = cum * LOG2E
        carry_ref[...] = cum[:, tm - 1:tm]


def _norm_modulate(x, g, sc, sh, *, mode, res=None, w_router=None, w_f=None, b_f=None):
    s, d = x.shape
    tm = min(NORM_TILE, s)
    row = pl.BlockSpec((tm, d), lambda i: (i, 0))
    vec = pl.BlockSpec((1, d), lambda i: (0, 0))
    args, in_specs, out_shape, out_specs, scratch = [x], [row], [], [], []
    if res is not None:
        y, gt = res
        args += [y, gt]
        in_specs += [row, vec]
        out_shape.append(jax.ShapeDtypeStruct((s, d), F32))
        out_specs.append(row)
    args += [g, sc, sh]
    in_specs += [vec, vec, vec]
    h_dtype = {"attn": MXU_DTYPE, "fox": MXU_DTYPE, "dense": MXU_DTYPE,
               "moe": F32, "final": F32}[mode]
    out_shape.append(jax.ShapeDtypeStruct((s, d), h_dtype))
    out_specs.append(row)
    n_experts = 0
    if mode == "moe":
        n_experts = w_router.shape[1]
        wr = jnp.zeros((d, LANES), MXU_DTYPE).at[:, :n_experts].set(w_router.astype(MXU_DTYPE))
        args.append(wr)
        in_specs.append(pl.BlockSpec((d, LANES), lambda i: (0, 0)))
        lane_blk = pl.BlockSpec((tm, LANES), lambda i: (i, 0))
        out_shape += [jax.ShapeDtypeStruct((s, LANES), jnp.int32),
                      jax.ShapeDtypeStruct((s, LANES), F32)]
        out_specs += [lane_blk, lane_blk]
    if mode == "fox":
        nh = w_f.shape[1]
        args += [w_f.T.astype(MXU_DTYPE), b_f.reshape(nh, 1)]
        in_specs += [pl.BlockSpec((nh, d), lambda i: (0, 0)),
                     pl.BlockSpec((nh, 1), lambda i: (0, 0))]
        out_shape.append(jax.ShapeDtypeStruct((s // tm, nh, tm), F32))
        out_specs.append(pl.BlockSpec((1, nh, tm), lambda i: (i, 0, 0)))
        scratch.append(pltpu.VMEM((nh, 1), F32))
    outs = pl.pallas_call(
        functools.partial(_norm_kernel, has_res=res is not None, mode=mode, n_experts=n_experts),
        out_shape=out_shape,
        grid=(s // tm,),
        in_specs=in_specs,
        out_specs=out_specs,
        scratch_shapes=scratch,
        compiler_params=_cparams(("arbitrary",)),
        name=f"norm_{mode}",
    )(*args)
    outs = list(outs)
    result = {}
    if res is not None:
        result["x"] = outs.pop(0)
    result["h"] = outs.pop(0)
    if mode == "moe":
        result["idx"], result["wts"] = outs.pop(0), outs.pop(0)
    if mode == "fox":
        result["f"] = outs.pop(0)
    return result


def _qkv_kernel(a_ref, w_ref, cs_ref, o_ref, wb_ref):
    @pl.when(pl.program_id(1) == 0)
    def _():
        wb_ref[...] = w_ref[0].astype(wb_ref.dtype)

    acc = jnp.dot(a_ref[...], wb_ref[...], preferred_element_type=F32) * cs_ref[...]
    for j in range(o_ref.shape[0]):
        o_ref[j] = acc[:, j * HEAD_DIM:(j + 1) * HEAD_DIM].astype(o_ref.dtype)


def _qkv_projection(h, w, layer):
    s, d = h.shape
    n = w.shape[2]
    n_heads3 = n // HEAD_DIM
    tm = min(ROW_TILE, s)
    nh = _largest_divisor(n_heads3, (12, 6, 4, 3, 2, 1))
    tn = nh * HEAD_DIM
    col_scale = jnp.where(jnp.arange(n) < n // 3, QK_SCALE_LOG2, 1.0).astype(F32).reshape(1, n)
    return pl.pallas_call(
        _qkv_kernel,
        out_shape=jax.ShapeDtypeStruct((n_heads3, s, HEAD_DIM), MXU_DTYPE),
        grid=(n // tn, s // tm),
        in_specs=[pl.BlockSpec((tm, d), lambda j, i: (i, 0)),
                  pl.BlockSpec((1, d, tn), lambda j, i: (layer, 0, j)),
                  pl.BlockSpec((1, tn), lambda j, i: (0, j))],
        out_specs=pl.BlockSpec((nh, tm, HEAD_DIM), lambda j, i: (j, i, 0)),
        scratch_shapes=[pltpu.VMEM((d, tn), MXU_DTYPE)],
        compiler_params=_cparams(("arbitrary", "arbitrary")),
        name="qkv_projection",
    )(h, w, col_scale)


def _out_proj_kernel(a_ref, w_ref, x_ref, gt_ref, o_ref, wb_ref):
    @pl.when(pl.program_id(1) == 0)
    def _():
        wb_ref[...] = w_ref[0].astype(wb_ref.dtype)

    acc = jnp.dot(a_ref[...], wb_ref[...], preferred_element_type=F32)
    o_ref[...] = x_ref[...] + gt_ref[...] * acc


def _out_projection_residual(a, w, layer, x, gt):
    s, k = a.shape
    n = w.shape[2]
    tm = min(ROW_TILE, s)
    tn = _largest_divisor(n, (1024, 512, 256, 128))
    return pl.pallas_call(
        _out_proj_kernel,
        out_shape=jax.ShapeDtypeStruct((s, n), F32),
        grid=(n // tn, s // tm),
        in_specs=[pl.BlockSpec((tm, k), lambda j, i: (i, 0)),
                  pl.BlockSpec((1, k, tn), lambda j, i: (layer, 0, j)),
                  pl.BlockSpec((tm, tn), lambda j, i: (i, j)),
                  pl.BlockSpec((1, tn), lambda j, i: (0, j))],
        out_specs=pl.BlockSpec((tm, tn), lambda j, i: (i, j)),
        scratch_shapes=[pltpu.VMEM((k, tn), MXU_DTYPE)],
        compiler_params=_cparams(("arbitrary", "arbitrary")),
        name="out_projection",
    )(a, w, x, gt)


def _lane_tiles(x):
    return [x[:, c * LANES:(c + 1) * LANES] for c in range(x.shape[1] // LANES)]


def _softmax_init(m_ref, acc_ref):
    m_ref[...] = jnp.full(m_ref.shape, BELOW_NEG_INF, F32)
    acc_ref[...] = jnp.zeros_like(acc_ref)


def _softmax_step(s, v_aug, m_ref, acc_ref):
    tiles = _lane_tiles(s)
    m_cur = jnp.max(functools.reduce(jnp.maximum, tiles), axis=-1, keepdims=True)
    m_old = m_ref[...]
    m_new = jnp.maximum(m_old, m_cur)
    p = jnp.concatenate([jnp.exp2(t - m_new) for t in tiles], axis=1).astype(MXU_DTYPE)
    pv = jnp.dot(p, v_aug, preferred_element_type=F32)
    alpha = jnp.exp2(m_old - m_new)
    acc_ref[...] = jnp.concatenate([alpha] * (acc_ref.shape[1] // LANES), axis=1) * acc_ref[...] + pv
    m_ref[...] = m_new


def _pipelined_tiles(n_plain, tail, qk, consume, s_ref, loop=True):
    s_ref[0] = qk(0)

    def step(j, cur, kind, prefetch=True):
        s = s_ref[cur]
        if prefetch:
            s_ref[1 - cur] = qk(j + 1)
        consume(j, s, kind)

    def pair(i, carry):
        step(2 * i, 0, None)
        step(2 * i + 1, 1, None)
        return carry

    def run_tail(cur):
        for k, kind in enumerate(tail):
            step(n_plain + k, cur, kind, prefetch=k + 1 < len(tail))
            cur = 1 - cur

    if not loop:
        run_tail(0)
        return
    lax.fori_loop(0, n_plain // 2, pair, 0)

    @pl.when(n_plain % 2 == 0)
    def _():
        run_tail(0)

    @pl.when(n_plain % 2 == 1)
    def _():
        step(n_plain - 1, 0, None)
        run_tail(1)


def _augment_values(v_ref, vaug_ref):
    hd = v_ref.shape[-1]
    for j in range(v_ref.shape[1]):
        vaug_ref[j, :, :hd] = v_ref[0, j]
        vaug_ref[j, :, hd:] = jnp.ones((v_ref.shape[2], hd), vaug_ref.dtype)


def _fox_kernel(q_ref, k_ref, v_ref, f_ref, o_ref, vaug_ref, s_ref, m_ref, acc_ref):
    i = pl.program_id(1)
    hd = q_ref.shape[2]

    @pl.when(i == 0)
    def _():
        _augment_values(v_ref, vaug_ref)

    q = q_ref[0]
    tq, tk = q.shape[0], k_ref.shape[2]
    diagonal_tiles = tq // tk

    def qk(j):
        return lax.dot_general(q, k_ref[0, j], (((1,), (1,)), ((), ())),
                               preferred_element_type=F32)

    def consume(j, s, kind):
        s = s - f_ref[0, j]
        if kind is not None:
            causal = (lax.broadcasted_iota(jnp.int32, (tq, tk), 1) + kind * tk
                      <= lax.broadcasted_iota(jnp.int32, (tq, tk), 0))
            s = jnp.where(causal, s, NEG_INF)
        _softmax_step(s, vaug_ref[j], m_ref, acc_ref)

    _softmax_init(m_ref, acc_ref)
    _pipelined_tiles(i * diagonal_tiles, list(range(diagonal_tiles)), qk, consume, s_ref)

    acc = acc_ref[...]
    o_ref[...] = (acc[:, :hd] / acc[:, hd:]).astype(o_ref.dtype)


def _fox_attention(qkv, f_rows, n_heads):
    _, s, hd = qkv.shape
    t = min(ATTN_TILE, s)
    tq = min(FOX_Q_TILE, s)
    assert tq % t == 0 and s % tq == 0
    nb = s // t
    kv = qkv.reshape(3 * n_heads, nb, t, hd)
    return pl.pallas_call(
        _fox_kernel,
        out_shape=jax.ShapeDtypeStruct((s, n_heads * hd), MXU_DTYPE),
        grid=(n_heads, s // tq),
        in_specs=[pl.BlockSpec((1, tq, hd), lambda h, i: (h, i, 0)),
                  pl.BlockSpec((1, nb, t, hd), lambda h, i: (n_heads + h, 0, 0, 0)),
                  pl.BlockSpec((1, nb, t, hd), lambda h, i: (2 * n_heads + h, 0, 0, 0)),
                  pl.BlockSpec((1, nb, 1, t), lambda h, i: (h, 0, 0, 0))],
        out_specs=pl.BlockSpec((tq, hd), lambda h, i: (i, h)),
        scratch_shapes=[pltpu.VMEM((nb, t, 2 * hd), MXU_DTYPE),
                        pltpu.VMEM((2, tq, t), F32),
                        pltpu.VMEM((tq, LANES), F32),
                        pltpu.VMEM((tq, 2 * hd), F32)],
        compiler_params=_cparams(("arbitrary", "arbitrary")),
        name="fox_attention",
    )(qkv, kv, kv, f_rows)


def _rel_bucket_np(dist):
    n = np.maximum(dist, 0)
    max_exact = REL_BUCKETS // 2
    nf = np.maximum(n, 1).astype(np.float32)
    large = max_exact + (np.log(nf / np.float32(max_exact))
                         / np.float32(math.log(REL_MAX_DIST / max_exact))
                         * np.float32(REL_BUCKETS - max_exact)).astype(np.int32)
    large = np.minimum(large, REL_BUCKETS - 1)
    return np.where(n < max_exact, n, large).astype(np.int32)


def _bias_table_kernel(rb_ref, idx_ref, o_ref, *, n_heads):
    h = pl.program_id(0)
    idx = idx_ref[0]
    bias = jnp.zeros(idx.shape, F32)
    far = rb_ref[(REL_BUCKETS - 1) * n_heads + h]
    for b in range(REL_BUCKETS):
        bias = jnp.where(idx == b, (rb_ref[b * n_heads + h] - far) * LOG2E, bias)
    o_ref[0, 0] = jnp.where(idx < 0, NEG_INF, bias)


def _moba_bias_tiles(rel_bias, t):
    n_heads = rel_bias.shape[1]
    dist = np.arange(2)[:, None, None] * t + np.arange(t)[None, :, None] - np.arange(t)[None, None, :]
    idx = jnp.asarray(np.where(dist < 0, -1, _rel_bucket_np(dist)).astype(np.int32))
    return pl.pallas_call(
        functools.partial(_bias_table_kernel, n_heads=n_heads),
        out_shape=jax.ShapeDtypeStruct((n_heads, 2, t, t), F32),
        grid_spec=pltpu.PrefetchScalarGridSpec(
            num_scalar_prefetch=1, grid=(n_heads, 2),
            in_specs=[pl.BlockSpec((1, t, t), lambda h, dt, rb: (dt, 0, 0))],
            out_specs=pl.BlockSpec((1, 1, t, t), lambda h, dt, rb: (h, dt, 0, 0))),
        compiler_params=_cparams(("arbitrary", "arbitrary")),
        name="moba_bias_tiles",
    )(rel_bias.reshape(-1), idx)


def _split3(x):
    hi = x.astype(MXU_DTYPE)
    r1 = x - hi.astype(F32)
    mid = r1.astype(MXU_DTYPE)
    lo = (r1 - mid.astype(F32)).astype(MXU_DTYPE)
    return hi, mid, lo


def _moba_kernel(q_ref, k_ref, v_ref, bias_ref, far_ref, o_ref,
                 vaug_ref, kaug_ref, qaug_ref, kmean_ref, selt_ref, s_ref, m_ref, acc_ref):
    t = pl.program_id(1)
    n_tiles, tq, hd = k_ref.shape[1], k_ref.shape[2], k_ref.shape[3]
    bpt = tq // MOBA_BLOCK
    n_blocks = n_tiles * bpt
    nbp = kmean_ref.shape[0]
    far_lane0 = LANES - 3

    @pl.when(t == 0)
    def _():
        _augment_values(v_ref, vaug_ref)
        kmean_ref[...] = jnp.zeros_like(kmean_ref)
        selt_ref[...] = jnp.zeros_like(selt_ref)
        lane = lax.broadcasted_iota(jnp.int32, (tq, LANES), 1)
        row_blk = lax.broadcasted_iota(jnp.int32, (tq, LANES), 0) // MOBA_BLOCK
        for j in range(n_tiles):
            kaug_ref[j, :, :hd] = k_ref[0, j]
            kaug_ref[j, :, hd:] = ((lane == j * bpt + row_blk) | (lane >= far_lane0)).astype(kaug_ref.dtype)
        for n in range(n_blocks):
            rows = slice((n % bpt) * MOBA_BLOCK, (n % bpt + 1) * MOBA_BLOCK)
            kmean_ref[n:n + 1, :] = jnp.mean(k_ref[0, n // bpt, rows, :].astype(F32), axis=0,
                                             keepdims=True)

    q = q_ref[0]

    gate = lax.dot_general(kmean_ref[...].astype(MXU_DTYPE), q, (((1,), (1,)), ((), ())),
                           preferred_element_type=F32)
    blk = lax.broadcasted_iota(jnp.int32, gate.shape, 0)
    own = t * bpt + lax.broadcasted_iota(jnp.int32, gate.shape, 1) // MOBA_BLOCK
    past = blk < own
    g = jnp.where(past, gate, NEG_INF)
    sel = jnp.where(blk == own, 0.0, NEG_INF)
    for _ in range(MOBA_TOPK):
        mx = jnp.max(g, axis=0, keepdims=True)
        pick = blk == jnp.min(jnp.where(g == mx, blk, nbp), axis=0, keepdims=True)
        sel = jnp.where(pick & past, 0.0, sel)
        g = jnp.where(pick, BELOW_NEG_INF, g)
    selt_ref[0:nbp, :] = sel
    far_hi, far_mid, far_lo = _split3(far_ref[0])
    lane = lax.broadcasted_iota(jnp.int32, (tq, LANES), 1)
    extra = selt_ref[...].T.astype(MXU_DTYPE)
    extra = jnp.where(lane == far_lane0, far_hi,
                      jnp.where(lane == far_lane0 + 1, far_mid,
                                jnp.where(lane == far_lane0 + 2, far_lo, extra)))
    qaug_ref[:, :hd] = q
    qaug_ref[:, hd:] = extra
    q_aug = qaug_ref[...]

    def qk(j):
        return lax.dot_general(q_aug, kaug_ref[j], (((1,), (1,)), ((), ())),
                               preferred_element_type=F32)

    def consume(j, s, kind):
        if kind == "own":
            s = s + bias_ref[0, 0]
        elif kind == "previous":
            s = s + bias_ref[0, 1]
        _softmax_step(s, vaug_ref[j], m_ref, acc_ref)

    _softmax_init(m_ref, acc_ref)

    @pl.when(t == 0)
    def _():
        _pipelined_tiles(0, ["own"], qk, consume, s_ref, loop=False)

    @pl.when(t >= 1)
    def _():
        _pipelined_tiles(t - 1, ["previous", "own"], qk, consume, s_ref)

    acc = acc_ref[...]
    o_ref[...] = (acc[:, :hd] / acc[:, hd:]).astype(o_ref.dtype)


def _moba_attention(qkv, bias_tiles, rel_bias, n_heads):
    _, s, hd = qkv.shape
    t = bias_tiles.shape[2]
    assert s % t == 0 and t % MOBA_BLOCK == 0 and t >= REL_MAX_DIST and hd == LANES
    assert s // MOBA_BLOCK <= LANES - 3
    n_tiles = s // t
    nbp = -(-(s // MOBA_BLOCK) // SUBLANES) * SUBLANES
    kv = qkv.reshape(3 * n_heads, n_tiles, t, hd)
    far = jnp.broadcast_to(rel_bias[REL_BUCKETS - 1][:, None, None] * LOG2E, (n_heads, 1, LANES))
    return pl.pallas_call(
        _moba_kernel,
        out_shape=jax.ShapeDtypeStruct((s, n_heads * hd), MXU_DTYPE),
        grid=(n_heads, n_tiles),
        in_specs=[pl.BlockSpec((1, t, hd), lambda h, i: (h, i, 0)),
                  pl.BlockSpec((1, n_tiles, t, hd), lambda h, i: (n_heads + h, 0, 0, 0)),
                  pl.BlockSpec((1, n_tiles, t, hd), lambda h, i: (2 * n_heads + h, 0, 0, 0)),
                  pl.BlockSpec((1, 2, t, t), lambda h, i: (h, 0, 0, 0)),
                  pl.BlockSpec((1, 1, LANES), lambda h, i: (h, 0, 0))],
        out_specs=pl.BlockSpec((t, hd), lambda h, i: (i, h)),
        scratch_shapes=[pltpu.VMEM((n_tiles, t, 2 * hd), MXU_DTYPE),
                        pltpu.VMEM((n_tiles, t, 2 * hd), MXU_DTYPE),
                        pltpu.VMEM((t, 2 * hd), MXU_DTYPE),
                        pltpu.VMEM((nbp, hd), F32),
                        pltpu.VMEM((LANES, t), F32),
                        pltpu.VMEM((2, t, t), F32),
                        pltpu.VMEM((t, LANES), F32),
                        pltpu.VMEM((t, 2 * hd), F32)],
        compiler_params=_cparams(("arbitrary", "arbitrary")),
        name="moba_attention",
    )(qkv, kv, kv, bias_tiles, far)


def _ffn_kernel(te_ref, tr_ref, x_ref, w1_ref, w3_ref, w2_ref, o_ref, w1b_ref, w3b_ref, w2b_ref):
    t = pl.program_id(0)
    f = pl.program_id(1)
    rows = tr_ref[t]
    tm = x_ref.shape[0]
    sub = min(FFN_SUB_ROWS, tm)

    def gate_up(r):
        x = x_ref[r, :]
        a = jnp.dot(x, w1b_ref[...], preferred_element_type=F32)
        b = jnp.dot(x, w3b_ref[...], preferred_element_type=F32)
        return (a * jax.nn.sigmoid(a) * b).astype(MXU_DTYPE)

    def down(g, r):
        o_ref[r, :] += jnp.dot(g, w2b_ref[...], preferred_element_type=F32)

    @pl.when(f == 0)
    def _():
        o_ref[...] = jnp.zeros_like(o_ref)

    n_chunks = (rows + sub - 1) // sub
    for nc in range(1, tm // sub + 1):
        @pl.when(n_chunks == nc)
        def _(nc=nc):
            w1b_ref[...] = w1_ref[0].astype(w1b_ref.dtype)
            w3b_ref[...] = w3_ref[0].astype(w3b_ref.dtype)
            w2b_ref[...] = w2_ref[0].astype(w2b_ref.dtype)
            for c in range(nc):
                r = pl.ds(c * sub, sub)
                down(gate_up(r), r)


def _swiglu_ffn(xs, w1, w3, w2, tile_expert, tile_rows, tm):
    r, d = xs.shape
    ff = w1.shape[2]
    tf = _largest_divisor(ff, (FFN_F_TILE, 128))
    nf = ff // tf

    def f_eff(t, f, tr):
        return jnp.where(tr[t] > 0, f, nf - 1)

    return pl.pallas_call(
        _ffn_kernel,
        out_shape=jax.ShapeDtypeStruct((r, d), F32),
        grid_spec=pltpu.PrefetchScalarGridSpec(
            num_scalar_prefetch=2, grid=(r // tm, nf),
            in_specs=[pl.BlockSpec((tm, d), lambda t, f, te, tr: (t, 0), pipeline_mode=pl.Buffered(1)),
                      pl.BlockSpec((1, d, tf), lambda t, f, te, tr: (te[t], 0, f_eff(t, f, tr))),
                      pl.BlockSpec((1, d, tf), lambda t, f, te, tr: (te[t], 0, f_eff(t, f, tr))),
                      pl.BlockSpec((1, tf, d), lambda t, f, te, tr: (te[t], f_eff(t, f, tr), 0))],
            out_specs=pl.BlockSpec((tm, d), lambda t, f, te, tr: (t, 0)),
            scratch_shapes=[pltpu.VMEM((d, tf), MXU_DTYPE), pltpu.VMEM((d, tf), MXU_DTYPE),
                            pltpu.VMEM((tf, d), MXU_DTYPE)]),
        compiler_params=_cparams(("arbitrary", "arbitrary")),
        name="swiglu_ffn",
    )(tile_expert, tile_rows, xs, w1, w3, w2)


def _gather_kernel(tok_ref, nrows_ref, h_ref, o_ref, buf_ref, sem):
    t = pl.program_id(0)
    n_chunks = pl.num_programs(0)
    sub = o_ref.shape[0]
    slot = t % 2

    def request(chunk, into):
        def start(r, carry):
            tok = tok_ref[chunk * sub + r]
            pltpu.make_async_copy(h_ref.at[pl.ds(tok, 1)], buf_ref.at[into, pl.ds(r, 1)],
                                  sem.at[into]).start()
            return carry

        lax.fori_loop(0, sub, start, 0, unroll=DMA_ISSUE_UNROLL)

    @pl.when((t == 0) & (nrows_ref[0] > 0))
    def _():
        request(0, 0)

    nxt = jnp.minimum(t + 1, n_chunks - 1)

    @pl.when((t + 1 < n_chunks) & (nrows_ref[nxt] > 0))
    def _():
        request(nxt, 1 - slot)

    @pl.when(nrows_ref[t] > 0)
    def _():
        pltpu.make_async_copy(h_ref.at[pl.ds(0, sub)], buf_ref.at[slot], sem.at[slot]).wait()
        o_ref[...] = buf_ref[slot].astype(o_ref.dtype)

    @pl.when(nrows_ref[t] == 0)
    def _():
        o_ref[...] = jnp.zeros_like(o_ref)


def _gather_rows(h, row_token, sub_rows, sub):
    s, d = h.shape
    r = row_token.shape[0]
    return pl.pallas_call(
        _gather_kernel,
        out_shape=jax.ShapeDtypeStruct((r, d), MXU_DTYPE),
        grid_spec=pltpu.PrefetchScalarGridSpec(
            num_scalar_prefetch=2, grid=(r // sub,),
            in_specs=[pl.BlockSpec(memory_space=pl.ANY)],
            out_specs=pl.BlockSpec((sub, d), lambda t, tok, nr: (t, 0)),
            scratch_shapes=[pltpu.VMEM((2, sub, d), F32), pltpu.SemaphoreType.DMA((2,))]),
        compiler_params=_cparams(("arbitrary",)),
        name="moe_gather",
    )(row_token, sub_rows, h)


def _combine_kernel(pos_ref, x_ref, wts_ref, gt_ref, ys_ref, o_ref, buf_ref, sem):
    t = pl.program_id(0)
    n_tiles = pl.num_programs(0)
    tm = x_ref.shape[0]
    slot = t % 2

    def request(tile, into):
        def start(r, carry):
            for k in range(TOP_K):
                src = pos_ref[(tile * tm + r) * TOP_K + k]
                pltpu.make_async_copy(ys_ref.at[pl.ds(src, 1)], buf_ref.at[into, k, pl.ds(r, 1)],
                                      sem.at[into]).start()
            return carry

        lax.fori_loop(0, tm, start, 0, unroll=DMA_ISSUE_UNROLL)

    @pl.when(t == 0)
    def _():
        request(0, 0)

    @pl.when(t + 1 < n_tiles)
    def _():
        request(t + 1, 1 - slot)

    for k in range(TOP_K):
        pltpu.make_async_copy(ys_ref.at[pl.ds(0, tm)], buf_ref.at[slot, k], sem.at[slot]).wait()
    wts = wts_ref[...]
    y = wts[:, 0:1] * buf_ref[slot, 0]
    for k in range(1, TOP_K):
        y = y + wts[:, k:k + 1] * buf_ref[slot, k]
    o_ref[...] = x_ref[...] + gt_ref[...] * y


def _combine_residual(x, wts, gt, ys, pos):
    s, d = x.shape
    tm = min(COMBINE_TILE, s)
    return pl.pallas_call(
        _combine_kernel,
        out_shape=jax.ShapeDtypeStruct((s, d), F32),
        grid_spec=pltpu.PrefetchScalarGridSpec(
            num_scalar_prefetch=1, grid=(s // tm,),
            in_specs=[pl.BlockSpec((tm, d), lambda t, pos: (t, 0)),
                      pl.BlockSpec((tm, LANES), lambda t, pos: (t, 0)),
                      pl.BlockSpec((1, d), lambda t, pos: (0, 0)),
                      pl.BlockSpec(memory_space=pl.ANY)],
            out_specs=pl.BlockSpec((tm, d), lambda t, pos: (t, 0)),
            scratch_shapes=[pltpu.VMEM((2, TOP_K, tm, d), F32), pltpu.SemaphoreType.DMA((2,))]),
        compiler_params=_cparams(("arbitrary",)),
        name="moe_combine",
    )(pos, x, wts, gt, ys)


def _moe_plan(idx, n_experts, tm, sub):
    s = idx.shape[0]
    n_slots = s * TOP_K
    e_flat = idx.reshape(-1)
    onehot = (e_flat[:, None] == jnp.arange(n_experts, dtype=jnp.int32)[None, :]).astype(jnp.int32)
    cnt = jnp.sum(onehot, axis=0)
    rank = jnp.sum((jnp.cumsum(onehot, axis=0) - onehot) * onehot, axis=1)
    ntile_e = (cnt + tm - 1) // tm
    per_e = (cnt + ntile_e * sub - 1) // jnp.maximum(ntile_e * sub, 1) * sub
    per_e = jnp.maximum(per_e, sub)
    tile_end = jnp.cumsum(ntile_e)
    tile_start = tile_end - ntile_e
    per_slot = per_e[e_flat]
    pos = (tile_start[e_flat] + rank // per_slot) * tm + rank % per_slot
    n_tiles = n_slots // tm + n_experts
    row_token = jnp.zeros((n_tiles * tm,), jnp.int32).at[pos].set(
        jnp.arange(n_slots, dtype=jnp.int32) // TOP_K)
    t_ids = jnp.arange(n_tiles, dtype=jnp.int32)
    te = jnp.sum((t_ids[:, None] >= tile_end[None, :]).astype(jnp.int32), axis=1)
    used = te < n_experts
    te_c = jnp.minimum(te, n_experts - 1)
    rows = jnp.where(used, jnp.clip(cnt[te_c] - (t_ids - tile_start[te_c]) * per_e[te_c], 0, per_e[te_c]), 0)
    e_last = jnp.max(jnp.where(ntile_e > 0, jnp.arange(n_experts, dtype=jnp.int32), 0))
    tile_expert = jnp.where(used, te_c, e_last).astype(jnp.int32)
    per = tm // sub
    s_ids = jnp.arange(n_tiles * per, dtype=jnp.int32)
    sub_rows = jnp.clip(rows[s_ids // per] - (s_ids % per) * sub, 0, sub)
    return pos.astype(jnp.int32), row_token, tile_expert, rows.astype(jnp.int32), sub_rows.astype(jnp.int32)


def kernel(x, c, w_ada, b_ada, g_mix, g_ffn, g_final, rel_bias, w_qkv_fox, w_f_fox, b_f_fox, w_o_fox, w_qkv_moba, w_o_moba, w1_dense, w3_dense, w2_dense, w_router, w1_moe, w3_moe, w2_moe):
    b, s, d = x.shape
    assert b == 1 and d % HEAD_DIM == 0
    depth = w_ada.shape[0]
    n_heads = d // HEAD_DIM
    n_experts = w_router.shape[2]
    tm = min(ROW_TILE, s)
    sub = min(FFN_SUB_ROWS, tm)

    mod = _ada_modulation(c, w_ada, b_ada)
    bias_tiles = _moba_bias_tiles(rel_bias, min(ATTN_TILE, s))
    dense_rows = jnp.full((s // tm,), tm, jnp.int32)
    moe_w13_shape = (-1,) + w1_moe.shape[2:]
    moe_w2_shape = (-1,) + w2_moe.shape[2:]

    xc = x.reshape(s, d)
    pending = None
    for i in range(depth):
        j = i // 2
        sh1, sc1, gt1, sh2, sc2, gt2 = [mod[i, :, k * d:(k + 1) * d] for k in range(6)]
        g1 = g_mix[i].reshape(1, d)
        g2 = g_ffn[i].reshape(1, d)

        if i % 2 == 0:
            r = _norm_modulate(xc, g1, sc1, sh1, mode="fox", res=pending,
                               w_f=w_f_fox[j], b_f=b_f_fox[j])
        else:
            r = _norm_modulate(xc, g1, sc1, sh1, mode="attn", res=pending)
        xc = r.get("x", xc)
        pending = None
        if i % 2 == 0:
            qkv = _qkv_projection(r["h"], w_qkv_fox, j)
            nt = r["f"].shape[0]
            f_rows = r["f"].transpose(1, 0, 2).reshape(n_heads, nt, 1, r["f"].shape[2])
            o = _fox_attention(qkv, f_rows, n_heads)
            xc = _out_projection_residual(o, w_o_fox, j, xc, gt1)
        else:
            qkv = _qkv_projection(r["h"], w_qkv_moba, j)
            o = _moba_attention(qkv, bias_tiles, rel_bias, n_heads)
            xc = _out_projection_residual(o, w_o_moba, j, xc, gt1)

        if i % 2 == 0:
            r = _norm_modulate(xc, g2, sc2, sh2, mode="dense")
            y = _swiglu_ffn(r["h"], w1_dense, w3_dense, w2_dense,
                            jnp.full((s // tm,), j, jnp.int32), dense_rows, tm)
            pending = (y, gt2)
        else:
            r = _norm_modulate(xc, g2, sc2, sh2, mode="moe", w_router=w_router[j])
            pos, row_token, tile_expert, tile_rows, sub_rows = _moe_plan(
                r["idx"][:, :TOP_K], n_experts, tm, sub)
            xs = _gather_rows(r["h"], row_token, sub_rows, sub)
            ys = _swiglu_ffn(xs, w1_moe.reshape(moe_w13_shape), w3_moe.reshape(moe_w13_shape),
                             w2_moe.reshape(moe_w2_shape), tile_expert + j * n_experts, tile_rows, tm)
            xc = _combine_residual(xc, r["wts"], gt2, ys, pos)

    zero = jnp.zeros((1, d), F32)
    r = _norm_modulate(xc, g_final.reshape(1, d), zero, zero, mode="final", res=pending)
    return r["h"].reshape(b, s, d)
```

```python
import functools
import math

import numpy as np
import jax
import jax.numpy as jnp
from jax import lax
from jax.experimental import pallas as pl
from jax.experimental.pallas import tpu as pltpu

HEAD_DIM = 128
MOBA_BLOCK = 256
MOBA_TOPK = 3
REL_BUCKETS = 32
REL_MAX_DIST = 128
TOP_K = 2
RMS_EPS = 1e-6
NEG_INF = -1e30
BELOW_NEG_INF = -3e38
LOG2E = math.log2(math.e)
QK_SCALE_LOG2 = HEAD_DIM ** -0.5 * LOG2E

LANES = 128
SUBLANES = 8
VMEM_LIMIT_BYTES = 56 * 1024 * 1024

MXU_DTYPE = jnp.bfloat16
F32 = jnp.float32

ROW_TILE = 1024
FFN_F_TILE = 512
FFN_SUB_ROWS = 256
ATTN_TILE = 512
FOX_Q_TILE = 1024
MOBA_Q_TILE = 1024
NORM_TILE = 512
COMBINE_TILE = 256
DMA_ISSUE_UNROLL = 8


def _cparams(semantics):
    return pltpu.CompilerParams(dimension_semantics=semantics,
                                vmem_limit_bytes=VMEM_LIMIT_BYTES)


def _largest_divisor(n, candidates):
    for c in candidates:
        if n % c == 0:
            return c
    raise ValueError(f"no tile in {candidates} divides {n}")


def _ada_kernel(c_ref, w_ref, b_ref, o_ref, ca_ref):
    c = c_ref[...]
    ca_ref[...] = c * jax.nn.sigmoid(c)
    d, tn = w_ref.shape[1], w_ref.shape[2]
    ch = min(d, 256)

    def body(r, acc):
        rows = pl.ds(pl.multiple_of(r * ch, ch), ch)
        prod = w_ref[0, rows, :] * ca_ref[rows, :]
        return acc + jnp.sum(prod.reshape(ch // SUBLANES, SUBLANES, tn), axis=0)

    acc = lax.fori_loop(0, d // ch, body, jnp.zeros((SUBLANES, tn), F32))
    o_ref[0] = jnp.sum(acc, axis=0, keepdims=True) + b_ref[0]


def _ada_modulation(c, w_ada, b_ada):
    depth, d, n = w_ada.shape
    tn = _largest_divisor(n, (1024, 512, 256, 128))
    return pl.pallas_call(
        _ada_kernel,
        out_shape=jax.ShapeDtypeStruct((depth, 1, n), F32),
        grid=(depth, n // tn),
        in_specs=[pl.BlockSpec((d, 1), lambda i, j: (0, 0)),
                  pl.BlockSpec((1, d, tn), lambda i, j: (i, 0, j)),
                  pl.BlockSpec((1, 1, tn), lambda i, j: (i, 0, j))],
        out_specs=pl.BlockSpec((1, 1, tn), lambda i, j: (i, 0, j)),
        scratch_shapes=[pltpu.VMEM((d, 1), F32)],
        compiler_params=_cparams(("arbitrary", "arbitrary")),
        name="ada_modulation",
    )(c.reshape(d, 1), w_ada, b_ada.reshape(depth, 1, n))


def _norm_kernel(*refs, has_res, mode, n_experts):
    refs = list(refs)
    x_ref = refs.pop(0)
    if has_res:
        y_ref, gt_ref = refs.pop(0), refs.pop(0)
    g_ref, sc_ref, sh_ref = refs.pop(0), refs.pop(0), refs.pop(0)
    if mode == "moe":
        wr_ref = refs.pop(0)
    if mode == "fox":
        wf_ref, bf_ref = refs.pop(0), refs.pop(0)
    if has_res:
        xo_ref = refs.pop(0)
    h_ref = refs.pop(0)

    x = x_ref[...]
    if has_res:
        x = x + gt_ref[...] * y_ref[...]
        xo_ref[...] = x
    ms = jnp.mean(x * x, axis=-1, keepdims=True)
    h = (x * lax.rsqrt(ms + RMS_EPS) * g_ref[...]) * (1.0 + sc_ref[...]) + sh_ref[...]
    h_ref[...] = h.astype(h_ref.dtype)

    if mode == "moe":
        idx_ref, wts_ref = refs.pop(0), refs.pop(0)
        logits = jnp.dot(h.astype(MXU_DTYPE), wr_ref[...], preferred_element_type=F32)
        lane = lax.broadcasted_iota(jnp.int32, logits.shape, 1)
        lg = jnp.where(lane < n_experts, logits, BELOW_NEG_INF)
        v0 = jnp.max(lg, axis=-1, keepdims=True)
        i0 = jnp.min(jnp.where(lg == v0, lane, LANES), axis=-1, keepdims=True)
        lg = jnp.where(lane == i0, BELOW_NEG_INF, lg)
        v1 = jnp.max(lg, axis=-1, keepdims=True)
        i1 = jnp.min(jnp.where(lg == v1, lane, LANES), axis=-1, keepdims=True)
        e1 = jnp.exp(v1 - v0)
        den = 1.0 + e1
        idx_ref[...] = jnp.where(lane == 0, i0, jnp.where(lane == 1, i1, 0))
        wts_ref[...] = jnp.where(lane == 0, 1.0 / den, jnp.where(lane == 1, e1 / den, 0.0))

    if mode == "fox":
        f_ref, carry_ref = refs.pop(0), refs.pop(0)

        @pl.when(pl.program_id(0) == 0)
        def _():
            carry_ref[...] = jnp.zeros_like(carry_ref)

        tm = x.shape[0]
        z = lax.dot_general(wf_ref[...], h.astype(MXU_DTYPE), (((1,), (1,)), ((), ())),
                            preferred_element_type=F32) + bf_ref[...]
        log_f = jnp.minimum(z, 0.0) - jnp.log1p(jnp.exp(-jnp.abs(z)))
        upper = (lax.broadcasted_iota(jnp.int32, (tm, tm), 0)
                 <= lax.broadcasted_iota(jnp.int32, (tm, tm), 1)).astype(F32)
        cum = jnp.dot(log_f, upper, preferred_element_type=F32,
                      precision=lax.Precision.HIGHEST) + carry_ref[...]
        f_ref[0] = cum * LOG2E
        carry_ref[...] = cum[:, tm - 1:tm]


def _norm_modulate(x, g, sc, sh, *, mode, res=None, w_router=None, w_f=None, b_f=None):
    s, d = x.shape
    tm = min(NORM_TILE, s)
    row = pl.BlockSpec((tm, d), lambda i: (i, 0))
    vec = pl.BlockSpec((1, d), lambda i: (0, 0))
    args, in_specs, out_shape, out_specs, scratch = [x], [row], [], [], []
    if res is not None:
        y, gt = res
        args += [y, gt]
        in_specs += [row, vec]
        out_shape.append(jax.ShapeDtypeStruct((s, d), F32))
        out_specs.append(row)
    args += [g, sc, sh]
    in_specs += [vec, vec, vec]
    h_dtype = {"attn": MXU_DTYPE, "fox": MXU_DTYPE, "dense": MXU_DTYPE,
               "moe": F32, "final": F32}[mode]
    out_shape.append(jax.ShapeDtypeStruct((s, d), h_dtype))
    out_specs.append(row)
    n_experts = 0
    if mode == "moe":
        n_experts = w_router.shape[1]
        wr = jnp.zeros((d, LANES), MXU_DTYPE).at[:, :n_experts].set(w_router.astype(MXU_DTYPE))
        args.append(wr)
        in_specs.append(pl.BlockSpec((d, LANES), lambda i: (0, 0)))
        lane_blk = pl.BlockSpec((tm, LANES), lambda i: (i, 0))
        out_shape += [jax.ShapeDtypeStruct((s, LANES), jnp.int32),
                      jax.ShapeDtypeStruct((s, LANES), F32)]
        out_specs += [lane_blk, lane_blk]
    if mode == "fox":
        nh = w_f.shape[1]
        args += [w_f.T.astype(MXU_DTYPE), b_f.reshape(nh, 1)]
        in_specs += [pl.BlockSpec((nh, d), lambda i: (0, 0)),
                     pl.BlockSpec((nh, 1), lambda i: (0, 0))]
        out_shape.append(jax.ShapeDtypeStruct((s // tm, nh, tm), F32))
        out_specs.append(pl.BlockSpec((1, nh, tm), lambda i: (i, 0, 0)))
        scratch.append(pltpu.VMEM((nh, 1), F32))
    outs = pl.pallas_call(
        functools.partial(_norm_kernel, has_res=res is not None, mode=mode, n_experts=n_experts),
        out_shape=out_shape,
        grid=(s // tm,),
        in_specs=in_specs,
        out_specs=out_specs,
        scratch_shapes=scratch,
        compiler_params=_cparams(("arbitrary",)),
        name=f"norm_{mode}",
    )(*args)
    outs = list(outs)
    result = {}
    if res is not None:
        result["x"] = outs.pop(0)
    result["h"] = outs.pop(0)
    if mode == "moe":
        result["idx"], result["wts"] = outs.pop(0), outs.pop(0)
    if mode == "fox":
        result["f"] = outs.pop(0)
    return result


def _qkv_kernel(a_ref, w_ref, cs_ref, o_ref, wb_ref):
    @pl.when(pl.program_id(1) == 0)
    def _():
        wb_ref[...] = w_ref[0].astype(wb_ref.dtype)

    acc = jnp.dot(a_ref[...], wb_ref[...], preferred_element_type=F32) * cs_ref[...]
    for j in range(o_ref.shape[0]):
        o_ref[j] = acc[:, j * HEAD_DIM:(j + 1) * HEAD_DIM].astype(o_ref.dtype)


def _qkv_projection(h, w, layer):
    s, d = h.shape
    n = w.shape[2]
    n_heads3 = n // HEAD_DIM
    tm = min(ROW_TILE, s)
    nh = _largest_divisor(n_heads3, (12, 6, 4, 3, 2, 1))
    tn = nh * HEAD_DIM
    col_scale = jnp.where(jnp.arange(n) < n // 3, QK_SCALE_LOG2, 1.0).astype(F32).reshape(1, n)
    return pl.pallas_call(
        _qkv_kernel,
        out_shape=jax.ShapeDtypeStruct((n_heads3, s, HEAD_DIM), MXU_DTYPE),
        grid=(n // tn, s // tm),
        in_specs=[pl.BlockSpec((tm, d), lambda j, i: (i, 0)),
                  pl.BlockSpec((1, d, tn), lambda j, i: (layer, 0, j)),
                  pl.BlockSpec((1, tn), lambda j, i: (0, j))],
        out_specs=pl.BlockSpec((nh, tm, HEAD_DIM), lambda j, i: (j, i, 0)),
        scratch_shapes=[pltpu.VMEM((d, tn), MXU_DTYPE)],
        compiler_params=_cparams(("arbitrary", "arbitrary")),
        name="qkv_projection",
    )(h, w, col_scale)


def _out_proj_kernel(a_ref, w_ref, x_ref, gt_ref, o_ref, wb_ref):
    @pl.when(pl.program_id(1) == 0)
    def _():
        wb_ref[...] = w_ref[0].astype(wb_ref.dtype)

    acc = jnp.dot(a_ref[...], wb_ref[...], preferred_element_type=F32)
    o_ref[...] = x_ref[...] + gt_ref[...] * acc


def _out_projection_residual(a, w, layer, x, gt):
    s, k = a.shape
    n = w.shape[2]
    tm = min(ROW_TILE, s)
    tn = _largest_divisor(n, (1024, 512, 256, 128))
    return pl.pallas_call(
        _out_proj_kernel,
        out_shape=jax.ShapeDtypeStruct((s, n), F32),
        grid=(n // tn, s // tm),
        in_specs=[pl.BlockSpec((tm, k), lambda j, i: (i, 0)),
                  pl.BlockSpec((1, k, tn), lambda j, i: (layer, 0, j)),
                  pl.BlockSpec((tm, tn), lambda j, i: (i, j)),
                  pl.BlockSpec((1, tn), lambda j, i: (0, j))],
        out_specs=pl.BlockSpec((tm, tn), lambda j, i: (i, j)),
        scratch_shapes=[pltpu.VMEM((k, tn), MXU_DTYPE)],
        compiler_params=_cparams(("arbitrary", "arbitrary")),
        name="out_projection",
    )(a, w, x, gt)


def _lane_tiles(x):
    return [x[:, c * LANES:(c + 1) * LANES] for c in range(x.shape[1] // LANES)]


def _softmax_init(m_ref, acc_ref):
    m_ref[...] = jnp.full(m_ref.shape, BELOW_NEG_INF, F32)
    acc_ref[...] = jnp.zeros_like(acc_ref)


def _softmax_step(s, v_aug, m_ref, acc_ref):
    tiles = _lane_tiles(s)
    m_cur = jnp.max(functools.reduce(jnp.maximum, tiles), axis=-1, keepdims=True)
    m_old = m_ref[...]
    m_new = jnp.maximum(m_old, m_cur)
    p = jnp.concatenate([jnp.exp2(t - m_new) for t in tiles], axis=1).astype(MXU_DTYPE)
    pv = jnp.dot(p, v_aug, preferred_element_type=F32)
    alpha = jnp.exp2(m_old - m_new)
    acc_ref[...] = jnp.concatenate([alpha] * (acc_ref.shape[1] // LANES), axis=1) * acc_ref[...] + pv
    m_ref[...] = m_new


def _pipelined_tiles(n_plain, tail, qk, consume, s_ref, loop=True):
    s_ref[0] = qk(0)

    def step(j, cur, kind, prefetch=True):
        s = s_ref[cur]
        if prefetch:
            s_ref[1 - cur] = qk(j + 1)
        consume(j, s, kind)

    def pair(i, carry):
        step(2 * i, 0, None)
        step(2 * i + 1, 1, None)
        return carry

    def run_tail(cur):
        for k, kind in enumerate(tail):
            step(n_plain + k, cur, kind, prefetch=k + 1 < len(tail))
            cur = 1 - cur

    if not loop:
        run_tail(0)
        return
    lax.fori_loop(0, n_plain // 2, pair, 0)

    @pl.when(n_plain % 2 == 0)
    def _():
        run_tail(0)

    @pl.when(n_plain % 2 == 1)
    def _():
        step(n_plain - 1, 0, None)
        run_tail(1)


def _augment_values(v_ref, vaug_ref):
    hd = v_ref.shape[-1]
    for j in range(v_ref.shape[1]):
        vaug_ref[j, :, :hd] = v_ref[0, j]
        vaug_ref[j, :, hd:] = jnp.ones((v_ref.shape[2], hd), vaug_ref.dtype)


def _fox_kernel(q_ref, k_ref, v_ref, f_ref, o_ref, vaug_ref, s_ref, m_ref, acc_ref):
    i = pl.program_id(1)
    hd = q_ref.shape[2]

    @pl.when(i == 0)
    def _():
        _augment_values(v_ref, vaug_ref)

    q = q_ref[0]
    tq, tk = q.shape[0], k_ref.shape[2]
    diagonal_tiles = tq // tk

    def qk(j):
        return lax.dot_general(q, k_ref[0, j], (((1,), (1,)), ((), ())),
                               preferred_element_type=F32)

    def consume(j, s, kind):
        s = s - f_ref[0, j]
        if kind is not None:
            causal = (lax.broadcasted_iota(jnp.int32, (tq, tk), 1) + kind * tk
                      <= lax.broadcasted_iota(jnp.int32, (tq, tk), 0))
            s = jnp.where(causal, s, NEG_INF)
        _softmax_step(s, vaug_ref[j], m_ref, acc_ref)

    _softmax_init(m_ref, acc_ref)
    _pipelined_tiles(i * diagonal_tiles, list(range(diagonal_tiles)), qk, consume, s_ref)

    acc = acc_ref[...]
    o_ref[...] = (acc[:, :hd] / acc[:, hd:]).astype(o_ref.dtype)


def _fox_attention(qkv, f_rows, n_heads):
    _, s, hd = qkv.shape
    t = min(ATTN_TILE, s)
    tq = min(FOX_Q_TILE, s)
    assert tq % t == 0 and s % tq == 0
    nb = s // t
    kv = qkv.reshape(3 * n_heads, nb, t, hd)
    return pl.pallas_call(
        _fox_kernel,
        out_shape=jax.ShapeDtypeStruct((s, n_heads * hd), MXU_DTYPE),
        grid=(n_heads, s // tq),
        in_specs=[pl.BlockSpec((1, tq, hd), lambda h, i: (h, i, 0)),
                  pl.BlockSpec((1, nb, t, hd), lambda h, i: (n_heads + h, 0, 0, 0)),
                  pl.BlockSpec((1, nb, t, hd), lambda h, i: (2 * n_heads + h, 0, 0, 0)),
                  pl.BlockSpec((1, nb, 1, t), lambda h, i: (h, 0, 0, 0))],
        out_specs=pl.BlockSpec((tq, hd), lambda h, i: (i, h)),
        scratch_shapes=[pltpu.VMEM((nb, t, 2 * hd), MXU_DTYPE),
                        pltpu.VMEM((2, tq, t), F32),
                        pltpu.VMEM((tq, LANES), F32),
                        pltpu.VMEM((tq, 2 * hd), F32)],
        compiler_params=_cparams(("arbitrary", "arbitrary")),
        name="fox_attention",
    )(qkv, kv, kv, f_rows)


def _rel_bucket_np(dist):
    n = np.maximum(dist, 0)
    max_exact = REL_BUCKETS // 2
    nf = np.maximum(n, 1).astype(np.float32)
    large = max_exact + (np.log(nf / np.float32(max_exact))
                         / np.float32(math.log(REL_MAX_DIST / max_exact))
                         * np.float32(REL_BUCKETS - max_exact)).astype(np.int32)
    large = np.minimum(large, REL_BUCKETS - 1)
    return np.where(n < max_exact, n, large).astype(np.int32)


def _bias_table_kernel(rb_ref, idx_ref, o_ref, *, n_heads):
    h = pl.program_id(0)
    idx = idx_ref[0]
    bias = jnp.zeros(idx.shape, F32)
    far = rb_ref[(REL_BUCKETS - 1) * n_heads + h]
    for b in range(REL_BUCKETS):
        bias = jnp.where(idx == b, (rb_ref[b * n_heads + h] - far) * LOG2E, bias)
    o_ref[0, 0] = jnp.where(idx < 0, NEG_INF, bias)


def _moba_bias_tiles(rel_bias, tq, tk):
    n_heads = rel_bias.shape[1]
    n = 1 + tq // tk
    offset = tk - np.arange(n) * tk
    dist = offset[:, None, None] + np.arange(tq)[None, :, None] - np.arange(tk)[None, None, :]
    idx = jnp.asarray(np.where(dist < 0, -1, _rel_bucket_np(dist)).astype(np.int32))
    return pl.pallas_call(
        functools.partial(_bias_table_kernel, n_heads=n_heads),
        out_shape=jax.ShapeDtypeStruct((n_heads, n, tq, tk), F32),
        grid_spec=pltpu.PrefetchScalarGridSpec(
            num_scalar_prefetch=1, grid=(n_heads, n),
            in_specs=[pl.BlockSpec((1, tq, tk), lambda h, dt, rb: (dt, 0, 0))],
            out_specs=pl.BlockSpec((1, 1, tq, tk), lambda h, dt, rb: (h, dt, 0, 0))),
        compiler_params=_cparams(("arbitrary", "arbitrary")),
        name="moba_bias_tiles",
    )(rel_bias.reshape(-1), idx)


def _split3(x):
    hi = x.astype(MXU_DTYPE)
    r1 = x - hi.astype(F32)
    mid = r1.astype(MXU_DTYPE)
    lo = (r1 - mid.astype(F32)).astype(MXU_DTYPE)
    return hi, mid, lo


def _moba_kernel(q_ref, k_ref, v_ref, bias_ref, far_ref, o_ref,
                 vaug_ref, kaug_ref, qaug_ref, kmean_ref, selt_ref, s_ref, m_ref, acc_ref):
    t = pl.program_id(1)
    n_tiles, tk, hd = k_ref.shape[1], k_ref.shape[2], k_ref.shape[3]
    tq = q_ref.shape[1]
    own_tiles = tq // tk
    bpt = tk // MOBA_BLOCK
    n_blocks = n_tiles * bpt
    nbp = kmean_ref.shape[0]
    far_lane0 = LANES - 3

    @pl.when(t == 0)
    def _():
        _augment_values(v_ref, vaug_ref)
        kmean_ref[...] = jnp.zeros_like(kmean_ref)
        selt_ref[...] = jnp.zeros_like(selt_ref)
        lane = lax.broadcasted_iota(jnp.int32, (tk, LANES), 1)
        row_blk = lax.broadcasted_iota(jnp.int32, (tk, LANES), 0) // MOBA_BLOCK
        for j in range(n_tiles):
            kaug_ref[j, :, :hd] = k_ref[0, j]
            kaug_ref[j, :, hd:] = ((lane == j * bpt + row_blk) | (lane >= far_lane0)).astype(kaug_ref.dtype)
        for n in range(n_blocks):
            rows = slice((n % bpt) * MOBA_BLOCK, (n % bpt + 1) * MOBA_BLOCK)
            kmean_ref[n:n + 1, :] = jnp.mean(k_ref[0, n // bpt, rows, :].astype(F32), axis=0,
                                             keepdims=True)

    q = q_ref[0]

    gate = lax.dot_general(kmean_ref[...].astype(MXU_DTYPE), q, (((1,), (1,)), ((), ())),
                           preferred_element_type=F32)
    blk = lax.broadcasted_iota(jnp.int32, gate.shape, 0)
    own = t * (tq // MOBA_BLOCK) + lax.broadcasted_iota(jnp.int32, gate.shape, 1) // MOBA_BLOCK
    past = blk < own
    g = jnp.where(past, gate, NEG_INF)
    sel = jnp.where(blk == own, 0.0, NEG_INF)
    for _ in range(MOBA_TOPK):
        mx = jnp.max(g, axis=0, keepdims=True)
        pick = blk == jnp.min(jnp.where(g == mx, blk, nbp), axis=0, keepdims=True)
        sel = jnp.where(pick & past, 0.0, sel)
        g = jnp.where(pick, BELOW_NEG_INF, g)
    selt_ref[0:nbp, :] = sel
    far_hi, far_mid, far_lo = _split3(far_ref[0])
    lane = lax.broadcasted_iota(jnp.int32, (tq, LANES), 1)
    extra = selt_ref[...].T.astype(MXU_DTYPE)
    extra = jnp.where(lane == far_lane0, far_hi,
                      jnp.where(lane == far_lane0 + 1, far_mid,
                                jnp.where(lane == far_lane0 + 2, far_lo, extra)))
    qaug_ref[:, :hd] = q
    qaug_ref[:, hd:] = extra
    q_aug = qaug_ref[...]

    def qk(j):
        return lax.dot_general(q_aug, kaug_ref[j], (((1,), (1,)), ((), ())),
                               preferred_element_type=F32)

    def consume(j, s, kind):
        if kind is not None:
            s = s + bias_ref[0, kind]
        _softmax_step(s, vaug_ref[j], m_ref, acc_ref)

    _softmax_init(m_ref, acc_ref)
    own_kinds = list(range(1, 1 + own_tiles))

    @pl.when(t == 0)
    def _():
        _pipelined_tiles(0, own_kinds, qk, consume, s_ref, loop=False)

    @pl.when(t >= 1)
    def _():
        _pipelined_tiles(t * own_tiles - 1, [0] + own_kinds, qk, consume, s_ref)

    acc = acc_ref[...]
    o_ref[...] = (acc[:, :hd] / acc[:, hd:]).astype(o_ref.dtype)


def _moba_attention(qkv, bias_tiles, rel_bias, n_heads):
    _, s, hd = qkv.shape
    n_bias, tq, t = bias_tiles.shape[1:]
    assert s % tq == 0 and tq % t == 0 and t % MOBA_BLOCK == 0 and t >= REL_MAX_DIST and hd == LANES
    assert s // MOBA_BLOCK <= LANES - 3
    n_tiles = s // t
    nbp = -(-(s // MOBA_BLOCK) // SUBLANES) * SUBLANES
    kv = qkv.reshape(3 * n_heads, n_tiles, t, hd)
    far = jnp.broadcast_to(rel_bias[REL_BUCKETS - 1][:, None, None] * LOG2E, (n_heads, 1, LANES))
    return pl.pallas_call(
        _moba_kernel,
        out_shape=jax.ShapeDtypeStruct((s, n_heads * hd), MXU_DTYPE),
        grid=(n_heads, s // tq),
        in_specs=[pl.BlockSpec((1, tq, hd), lambda h, i: (h, i, 0)),
                  pl.BlockSpec((1, n_tiles, t, hd), lambda h, i: (n_heads + h, 0, 0, 0)),
                  pl.BlockSpec((1, n_tiles, t, hd), lambda h, i: (2 * n_heads + h, 0, 0, 0)),
                  pl.BlockSpec((1, n_bias, tq, t), lambda h, i: (h, 0, 0, 0)),
                  pl.BlockSpec((1, 1, LANES), lambda h, i: (h, 0, 0))],
        out_specs=pl.BlockSpec((tq, hd), lambda h, i: (i, h)),
        scratch_shapes=[pltpu.VMEM((n_tiles, t, 2 * hd), MXU_DTYPE),
                        pltpu.VMEM((n_tiles, t, 2 * hd), MXU_DTYPE),
                        pltpu.VMEM((tq, 2 * hd), MXU_DTYPE),
                        pltpu.VMEM((nbp, hd), F32),
                        pltpu.VMEM((LANES, tq), F32),
                        pltpu.VMEM((2, tq, t), F32),
                        pltpu.VMEM((tq, LANES), F32),
                        pltpu.VMEM((tq, 2 * hd), F32)],
        compiler_params=_cparams(("arbitrary", "arbitrary")),
        name="moba_attention",
    )(qkv, kv, kv, bias_tiles, far)


def _ffn_kernel(te_ref, tr_ref, x_ref, w1_ref, w3_ref, w2_ref, o_ref, w1b_ref, w3b_ref, w2b_ref):
    t = pl.program_id(0)
    f = pl.program_id(1)
    rows = tr_ref[t]
    tm = x_ref.shape[0]
    sub = min(FFN_SUB_ROWS, tm)

    def gate_up(r):
        x = x_ref[r, :]
        a = jnp.dot(x, w1b_ref[...], preferred_element_type=F32)
        b = jnp.dot(x, w3b_ref[...], preferred_element_type=F32)
        return (a * jax.nn.sigmoid(a) * b).astype(MXU_DTYPE)

    def down(g, r):
        o_ref[r, :] += jnp.dot(g, w2b_ref[...], preferred_element_type=F32)

    @pl.when(f == 0)
    def _():
        o_ref[...] = jnp.zeros_like(o_ref)

    n_chunks = (rows + sub - 1) // sub
    for nc in range(1, tm // sub + 1):
        @pl.when(n_chunks == nc)
        def _(nc=nc):
            w1b_ref[...] = w1_ref[0].astype(w1b_ref.dtype)
            w3b_ref[...] = w3_ref[0].astype(w3b_ref.dtype)
            w2b_ref[...] = w2_ref[0].astype(w2b_ref.dtype)
            for c in range(nc):
                r = pl.ds(c * sub, sub)
                down(gate_up(r), r)


def _swiglu_ffn(xs, w1, w3, w2, tile_expert, tile_rows, tm):
    r, d = xs.shape
    ff = w1.shape[2]
    tf = _largest_divisor(ff, (FFN_F_TILE, 128))
    nf = ff // tf

    def f_eff(t, f, tr):
        return jnp.where(tr[t] > 0, f, nf - 1)

    return pl.pallas_call(
        _ffn_kernel,
        out_shape=jax.ShapeDtypeStruct((r, d), F32),
        grid_spec=pltpu.PrefetchScalarGridSpec(
            num_scalar_prefetch=2, grid=(r // tm, nf),
            in_specs=[pl.BlockSpec((tm, d), lambda t, f, te, tr: (t, 0), pipeline_mode=pl.Buffered(1)),
                      pl.BlockSpec((1, d, tf), lambda t, f, te, tr: (te[t], 0, f_eff(t, f, tr))),
                      pl.BlockSpec((1, d, tf), lambda t, f, te, tr: (te[t], 0, f_eff(t, f, tr))),
                      pl.BlockSpec((1, tf, d), lambda t, f, te, tr: (te[t], f_eff(t, f, tr), 0))],
            out_specs=pl.BlockSpec((tm, d), lambda t, f, te, tr: (t, 0)),
            scratch_shapes=[pltpu.VMEM((d, tf), MXU_DTYPE), pltpu.VMEM((d, tf), MXU_DTYPE),
                            pltpu.VMEM((tf, d), MXU_DTYPE)]),
        compiler_params=_cparams(("arbitrary", "arbitrary")),
        name="swiglu_ffn",
    )(tile_expert, tile_rows, xs, w1, w3, w2)


def _gather_kernel(tok_ref, nrows_ref, h_ref, o_ref, buf_ref, sem):
    t = pl.program_id(0)
    n_chunks = pl.num_programs(0)
    sub = o_ref.shape[0]
    slot = t % 2

    def request(chunk, into):
        def start(r, carry):
            tok = tok_ref[chunk * sub + r]
            pltpu.make_async_copy(h_ref.at[pl.ds(tok, 1)], buf_ref.at[into, pl.ds(r, 1)],
                                  sem.at[into]).start()
            return carry

        lax.fori_loop(0, sub, start, 0, unroll=DMA_ISSUE_UNROLL)

    @pl.when((t == 0) & (nrows_ref[0] > 0))
    def _():
        request(0, 0)

    nxt = jnp.minimum(t + 1, n_chunks - 1)

    @pl.when((t + 1 < n_chunks) & (nrows_ref[nxt] > 0))
    def _():
        request(nxt, 1 - slot)

    @pl.when(nrows_ref[t] > 0)
    def _():
        pltpu.make_async_copy(h_ref.at[pl.ds(0, sub)], buf_ref.at[slot], sem.at[slot]).wait()
        o_ref[...] = buf_ref[slot].astype(o_ref.dtype)

    @pl.when(nrows_ref[t] == 0)
    def _():
        o_ref[...] = jnp.zeros_like(o_ref)


def _gather_rows(h, row_token, sub_rows, sub):
    s, d = h.shape
    r = row_token.shape[0]
    return pl.pallas_call(
        _gather_kernel,
        out_shape=jax.ShapeDtypeStruct((r, d), MXU_DTYPE),
        grid_spec=pltpu.PrefetchScalarGridSpec(
            num_scalar_prefetch=2, grid=(r // sub,),
            in_specs=[pl.BlockSpec(memory_space=pl.ANY)],
            out_specs=pl.BlockSpec((sub, d), lambda t, tok, nr: (t, 0)),
            scratch_shapes=[pltpu.VMEM((2, sub, d), F32), pltpu.SemaphoreType.DMA((2,))]),
        compiler_params=_cparams(("arbitrary",)),
        name="moe_gather",
    )(row_token, sub_rows, h)


def _combine_kernel(pos_ref, x_ref, wts_ref, gt_ref, ys_ref, o_ref, buf_ref, sem):
    t = pl.program_id(0)
    n_tiles = pl.num_programs(0)
    tm = x_ref.shape[0]
    slot = t % 2

    def request(tile, into):
        def start(r, carry):
            for k in range(TOP_K):
                src = pos_ref[(tile * tm + r) * TOP_K + k]
                pltpu.make_async_copy(ys_ref.at[pl.ds(src, 1)], buf_ref.at[into, k, pl.ds(r, 1)],
                                      sem.at[into]).start()
            return carry

        lax.fori_loop(0, tm, start, 0, unroll=DMA_ISSUE_UNROLL)

    @pl.when(t == 0)
    def _():
        request(0, 0)

    @pl.when(t + 1 < n_tiles)
    def _():
        request(t + 1, 1 - slot)

    for k in range(TOP_K):
        pltpu.make_async_copy(ys_ref.at[pl.ds(0, tm)], buf_ref.at[slot, k], sem.at[slot]).wait()
    wts = wts_ref[...]
    y = wts[:, 0:1] * buf_ref[slot, 0]
    for k in range(1, TOP_K):
        y = y + wts[:, k:k + 1] * buf_ref[slot, k]
    o_ref[...] = x_ref[...] + gt_ref[...] * y


def _combine_residual(x, wts, gt, ys, pos):
    s, d = x.shape
    tm = min(COMBINE_TILE, s)
    return pl.pallas_call(
        _combine_kernel,
        out_shape=jax.ShapeDtypeStruct((s, d), F32),
        grid_spec=pltpu.PrefetchScalarGridSpec(
            num_scalar_prefetch=1, grid=(s // tm,),
            in_specs=[pl.BlockSpec((tm, d), lambda t, pos: (t, 0)),
                      pl.BlockSpec((tm, LANES), lambda t, pos: (t, 0)),
                      pl.BlockSpec((1, d), lambda t, pos: (0, 0)),
                      pl.BlockSpec(memory_space=pl.ANY)],
            out_specs=pl.BlockSpec((tm, d), lambda t, pos: (t, 0)),
            scratch_shapes=[pltpu.VMEM((2, TOP_K, tm, d), F32), pltpu.SemaphoreType.DMA((2,))]),
        compiler_params=_cparams(("arbitrary",)),
        name="moe_combine",
    )(pos, x, wts, gt, ys)


def _moe_plan(idx, n_experts, tm, sub):
    s = idx.shape[0]
    n_slots = s * TOP_K
    e_flat = idx.reshape(-1)
    onehot = (e_flat[:, None] == jnp.arange(n_experts, dtype=jnp.int32)[None, :]).astype(jnp.int32)
    cnt = jnp.sum(onehot, axis=0)
    rank = jnp.sum((jnp.cumsum(onehot, axis=0) - onehot) * onehot, axis=1)
    ntile_e = (cnt + tm - 1) // tm
    per_e = (cnt + ntile_e * sub - 1) // jnp.maximum(ntile_e * sub, 1) * sub
    per_e = jnp.maximum(per_e, sub)
    tile_end = jnp.cumsum(ntile_e)
    tile_start = tile_end - ntile_e
    per_slot = per_e[e_flat]
    pos = (tile_start[e_flat] + rank // per_slot) * tm + rank % per_slot
    n_tiles = n_slots // tm + n_experts
    row_token = jnp.zeros((n_tiles * tm,), jnp.int32).at[pos].set(
        jnp.arange(n_slots, dtype=jnp.int32) // TOP_K)
    t_ids = jnp.arange(n_tiles, dtype=jnp.int32)
    te = jnp.sum((t_ids[:, None] >= tile_end[None, :]).astype(jnp.int32), axis=1)
    used = te < n_experts
    te_c = jnp.minimum(te, n_experts - 1)
    rows = jnp.where(used, jnp.clip(cnt[te_c] - (t_ids - tile_start[te_c]) * per_e[te_c], 0, per_e[te_c]), 0)
    e_last = jnp.max(jnp.where(ntile_e > 0, jnp.arange(n_experts, dtype=jnp.int32), 0))
    tile_expert = jnp.where(used, te_c, e_last).astype(jnp.int32)
    per = tm // sub
    s_ids = jnp.arange(n_tiles * per, dtype=jnp.int32)
    sub_rows = jnp.clip(rows[s_ids // per] - (s_ids % per) * sub, 0, sub)
    return pos.astype(jnp.int32), row_token, tile_expert, rows.astype(jnp.int32), sub_rows.astype(jnp.int32)


def kernel(x, c, w_ada, b_ada, g_mix, g_ffn, g_final, rel_bias, w_qkv_fox, w_f_fox, b_f_fox, w_o_fox, w_qkv_moba, w_o_moba, w1_dense, w3_dense, w2_dense, w_router, w1_moe, w3_moe, w2_moe):
    b, s, d = x.shape
    assert b == 1 and d % HEAD_DIM == 0
    depth = w_ada.shape[0]
    n_heads = d // HEAD_DIM
    n_experts = w_router.shape[2]
    tm = min(ROW_TILE, s)
    sub = min(FFN_SUB_ROWS, tm)

    mod = _ada_modulation(c, w_ada, b_ada)
    bias_tiles = _moba_bias_tiles(rel_bias, min(MOBA_Q_TILE, s), min(ATTN_TILE, s))
    dense_rows = jnp.full((s // tm,), tm, jnp.int32)
    moe_w13_shape = (-1,) + w1_moe.shape[2:]
    moe_w2_shape = (-1,) + w2_moe.shape[2:]

    xc = x.reshape(s, d)
    pending = None
    for i in range(depth):
        j = i // 2
        sh1, sc1, gt1, sh2, sc2, gt2 = [mod[i, :, k * d:(k + 1) * d] for k in range(6)]
        g1 = g_mix[i].reshape(1, d)
        g2 = g_ffn[i].reshape(1, d)

        if i % 2 == 0:
            r = _norm_modulate(xc, g1, sc1, sh1, mode="fox", res=pending,
                               w_f=w_f_fox[j], b_f=b_f_fox[j])
        else:
            r = _norm_modulate(xc, g1, sc1, sh1, mode="attn", res=pending)
        xc = r.get("x", xc)
        pending = None
        if i % 2 == 0:
            qkv = _qkv_projection(r["h"], w_qkv_fox, j)
            nt = r["f"].shape[0]
            f_rows = r["f"].transpose(1, 0, 2).reshape(n_heads, nt, 1, r["f"].shape[2])
            o = _fox_attention(qkv, f_rows, n_heads)
            xc = _out_projection_residual(o, w_o_fox, j, xc, gt1)
        else:
            qkv = _qkv_projection(r["h"], w_qkv_moba, j)
            o = _moba_attention(qkv, bias_tiles, rel_bias, n_heads)
            xc = _out_projection_residual(o, w_o_moba, j, xc, gt1)

        if i % 2 == 0:
            r = _norm_modulate(xc, g2, sc2, sh2, mode="dense")
            y = _swiglu_ffn(r["h"], w1_dense, w3_dense, w2_dense,
                            jnp.full((s // tm,), j, jnp.int32), dense_rows, tm)
            pending = (y, gt2)
        else:
            r = _norm_modulate(xc, g2, sc2, sh2, mode="moe", w_router=w_router[j])
            pos, row_token, tile_expert, tile_rows, sub_rows = _moe_plan(
                r["idx"][:, :TOP_K], n_experts, tm, sub)
            xs = _gather_rows(r["h"], row_token, sub_rows, sub)
            ys = _swiglu_ffn(xs, w1_moe.reshape(moe_w13_shape), w3_moe.reshape(moe_w13_shape),
                             w2_moe.reshape(moe_w2_shape), tile_expert + j * n_experts, tile_rows, tm)
            xc = _combine_residual(xc, r["wts"], gt2, ys, pos)

    zero = jnp.zeros((1, d), F32)
    r = _norm_modulate(xc, g_final.reshape(1, d), zero, zero, mode="final", res=pending)
    return r["h"].reshape(b, s, d)
```

```python
import functools
import math

import numpy as np
import jax
import jax.numpy as jnp
from jax import lax
from jax.experimental import pallas as pl
from jax.experimental.pallas import tpu as pltpu

HEAD_DIM = 128
MOBA_BLOCK = 256
MOBA_TOPK = 3
REL_BUCKETS = 32
REL_MAX_DIST = 128
TOP_K = 2
RMS_EPS = 1e-6
NEG_INF = -1e30
BELOW_NEG_INF = -3e38
LOG2E = math.log2(math.e)
QK_SCALE_LOG2 = HEAD_DIM ** -0.5 * LOG2E

LANES = 128
SUBLANES = 8
VMEM_LIMIT_BYTES = 56 * 1024 * 1024

MXU_DTYPE = jnp.bfloat16
F32 = jnp.float32

ROW_TILE = 1024
FFN_F_TILE = 512
FFN_SUB_ROWS = 256
FFN_FULL_TILE_CHUNK = 512
ATTN_TILE = 512
FOX_Q_TILE = 1024
MOBA_Q_TILE = 1024
NORM_TILE = 512
COMBINE_TILE = 256
DMA_ISSUE_UNROLL = 8


def _cparams(semantics):
    return pltpu.CompilerParams(dimension_semantics=semantics,
                                vmem_limit_bytes=VMEM_LIMIT_BYTES)


def _largest_divisor(n, candidates):
    for c in candidates:
        if n % c == 0:
            return c
    raise ValueError(f"no tile in {candidates} divides {n}")


def _ada_kernel(c_ref, w_ref, b_ref, o_ref, ca_ref):
    c = c_ref[...]
    ca_ref[...] = c * jax.nn.sigmoid(c)
    d, tn = w_ref.shape[1], w_ref.shape[2]
    ch = min(d, 256)

    def body(r, acc):
        rows = pl.ds(pl.multiple_of(r * ch, ch), ch)
        prod = w_ref[0, rows, :] * ca_ref[rows, :]
        return acc + jnp.sum(prod.reshape(ch // SUBLANES, SUBLANES, tn), axis=0)

    acc = lax.fori_loop(0, d // ch, body, jnp.zeros((SUBLANES, tn), F32))
    o_ref[0] = jnp.sum(acc, axis=0, keepdims=True) + b_ref[0]


def _ada_modulation(c, w_ada, b_ada):
    depth, d, n = w_ada.shape
    tn = _largest_divisor(n, (1024, 512, 256, 128))
    return pl.pallas_call(
        _ada_kernel,
        out_shape=jax.ShapeDtypeStruct((depth, 1, n), F32),
        grid=(depth, n // tn),
        in_specs=[pl.BlockSpec((d, 1), lambda i, j: (0, 0)),
                  pl.BlockSpec((1, d, tn), lambda i, j: (i, 0, j)),
                  pl.BlockSpec((1, 1, tn), lambda i, j: (i, 0, j))],
        out_specs=pl.BlockSpec((1, 1, tn), lambda i, j: (i, 0, j)),
        scratch_shapes=[pltpu.VMEM((d, 1), F32)],
        compiler_params=_cparams(("arbitrary", "arbitrary")),
        name="ada_modulation",
    )(c.reshape(d, 1), w_ada, b_ada.reshape(depth, 1, n))


def _norm_kernel(*refs, has_res, mode, n_experts):
    refs = list(refs)
    x_ref = refs.pop(0)
    if has_res:
        y_ref, gt_ref = refs.pop(0), refs.pop(0)
    g_ref, sc_ref, sh_ref = refs.pop(0), refs.pop(0), refs.pop(0)
    if mode == "moe":
        wr_ref = refs.pop(0)
    if mode == "fox":
        wf_ref, bf_ref = refs.pop(0), refs.pop(0)
    if has_res:
        xo_ref = refs.pop(0)
    h_ref = refs.pop(0)

    x = x_ref[...]
    if has_res:
        x = x + gt_ref[...] * y_ref[...]
        xo_ref[...] = x
    ms = jnp.mean(x * x, axis=-1, keepdims=True)
    h = (x * lax.rsqrt(ms + RMS_EPS) * g_ref[...]) * (1.0 + sc_ref[...]) + sh_ref[...]
    h_ref[...] = h.astype(h_ref.dtype)

    if mode == "moe":
        idx_ref, wts_ref = refs.pop(0), refs.pop(0)
        logits = jnp.dot(h.astype(MXU_DTYPE), wr_ref[...], preferred_element_type=F32)
        lane = lax.broadcasted_iota(jnp.int32, logits.shape, 1)
        lg = jnp.where(lane < n_experts, logits, BELOW_NEG_INF)
        v0 = jnp.max(lg, axis=-1, keepdims=True)
        i0 = jnp.min(jnp.where(lg == v0, lane, LANES), axis=-1, keepdims=True)
        lg = jnp.where(lane == i0, BELOW_NEG_INF, lg)
        v1 = jnp.max(lg, axis=-1, keepdims=True)
        i1 = jnp.min(jnp.where(lg == v1, lane, LANES), axis=-1, keepdims=True)
        e1 = jnp.exp(v1 - v0)
        den = 1.0 + e1
        idx_ref[...] = jnp.where(lane == 0, i0, jnp.where(lane == 1, i1, 0))
        wts_ref[...] = jnp.where(lane == 0, 1.0 / den, jnp.where(lane == 1, e1 / den, 0.0))

    if mode == "fox":
        f_ref, carry_ref = refs.pop(0), refs.pop(0)

        @pl.when(pl.program_id(0) == 0)
        def _():
            carry_ref[...] = jnp.zeros_like(carry_ref)

        tm = x.shape[0]
        z = lax.dot_general(wf_ref[...], h.astype(MXU_DTYPE), (((1,), (1,)), ((), ())),
                            preferred_element_type=F32) + bf_ref[...]
        log_f = jnp.minimum(z, 0.0) - jnp.log1p(jnp.exp(-jnp.abs(z)))
        upper = (lax.broadcasted_iota(jnp.int32, (tm, tm), 0)
                 <= lax.broadcasted_iota(jnp.int32, (tm, tm), 1)).astype(F32)
        cum = jnp.dot(log_f, upper, preferred_element_type=F32,
                      precision=lax.Precision.HIGHEST) + carry_ref[...]
        f_ref[0] = cum * LOG2E
        carry_ref[...] = cum[:, tm - 1:tm]


def _norm_modulate(x, g, sc, sh, *, mode, res=None, w_router=None, w_f=None, b_f=None):
    s, d = x.shape
    tm = min(NORM_TILE, s)
    row = pl.BlockSpec((tm, d), lambda i: (i, 0))
    vec = pl.BlockSpec((1, d), lambda i: (0, 0))
    args, in_specs, out_shape, out_specs, scratch = [x], [row], [], [], []
    if res is not None:
        y, gt = res
        args += [y, gt]
        in_specs += [row, vec]
        out_shape.append(jax.ShapeDtypeStruct((s, d), F32))
        out_specs.append(row)
    args += [g, sc, sh]
    in_specs += [vec, vec, vec]
    h_dtype = {"attn": MXU_DTYPE, "fox": MXU_DTYPE, "dense": MXU_DTYPE,
               "moe": F32, "final": F32}[mode]
    out_shape.append(jax.ShapeDtypeStruct((s, d), h_dtype))
    out_specs.append(row)
    n_experts = 0
    if mode == "moe":
        n_experts = w_router.shape[1]
        wr = jnp.zeros((d, LANES), MXU_DTYPE).at[:, :n_experts].set(w_router.astype(MXU_DTYPE))
        args.append(wr)
        in_specs.append(pl.BlockSpec((d, LANES), lambda i: (0, 0)))
        lane_blk = pl.BlockSpec((tm, LANES), lambda i: (i, 0))
        out_shape += [jax.ShapeDtypeStruct((s, LANES), jnp.int32),
                      jax.ShapeDtypeStruct((s, LANES), F32)]
        out_specs += [lane_blk, lane_blk]
    if mode == "fox":
        nh = w_f.shape[1]
        args += [w_f.T.astype(MXU_DTYPE), b_f.reshape(nh, 1)]
        in_specs += [pl.BlockSpec((nh, d), lambda i: (0, 0)),
                     pl.BlockSpec((nh, 1), lambda i: (0, 0))]
        out_shape.append(jax.ShapeDtypeStruct((s // tm, nh, tm), F32))
        out_specs.append(pl.BlockSpec((1, nh, tm), lambda i: (i, 0, 0)))
        scratch.append(pltpu.VMEM((nh, 1), F32))
    outs = pl.pallas_call(
        functools.partial(_norm_kernel, has_res=res is not None, mode=mode, n_experts=n_experts),
        out_shape=out_shape,
        grid=(s // tm,),
        in_specs=in_specs,
        out_specs=out_specs,
        scratch_shapes=scratch,
        compiler_params=_cparams(("arbitrary",)),
        name=f"norm_{mode}",
    )(*args)
    outs = list(outs)
    result = {}
    if res is not None:
        result["x"] = outs.pop(0)
    result["h"] = outs.pop(0)
    if mode == "moe":
        result["idx"], result["wts"] = outs.pop(0), outs.pop(0)
    if mode == "fox":
        result["f"] = outs.pop(0)
    return result


def _qkv_kernel(a_ref, w_ref, cs_ref, o_ref, wb_ref):
    @pl.when(pl.program_id(1) == 0)
    def _():
        wb_ref[...] = w_ref[0].astype(wb_ref.dtype)

    acc = jnp.dot(a_ref[...], wb_ref[...], preferred_element_type=F32) * cs_ref[...]
    for j in range(o_ref.shape[0]):
        o_ref[j] = acc[:, j * HEAD_DIM:(j + 1) * HEAD_DIM].astype(o_ref.dtype)


def _qkv_projection(h, w, layer):
    s, d = h.shape
    n = w.shape[2]
    n_heads3 = n // HEAD_DIM
    tm = min(ROW_TILE, s)
    nh = _largest_divisor(n_heads3, (12, 6, 4, 3, 2, 1))
    tn = nh * HEAD_DIM
    col_scale = jnp.where(jnp.arange(n) < n // 3, QK_SCALE_LOG2, 1.0).astype(F32).reshape(1, n)
    return pl.pallas_call(
        _qkv_kernel,
        out_shape=jax.ShapeDtypeStruct((n_heads3, s, HEAD_DIM), MXU_DTYPE),
        grid=(n // tn, s // tm),
        in_specs=[pl.BlockSpec((tm, d), lambda j, i: (i, 0)),
                  pl.BlockSpec((1, d, tn), lambda j, i: (layer, 0, j)),
                  pl.BlockSpec((1, tn), lambda j, i: (0, j))],
        out_specs=pl.BlockSpec((nh, tm, HEAD_DIM), lambda j, i: (j, i, 0)),
        scratch_shapes=[pltpu.VMEM((d, tn), MXU_DTYPE)],
        compiler_params=_cparams(("arbitrary", "arbitrary")),
        name="qkv_projection",
    )(h, w, col_scale)


def _out_proj_kernel(a_ref, w_ref, x_ref, gt_ref, o_ref, wb_ref):
    @pl.when(pl.program_id(1) == 0)
    def _():
        wb_ref[...] = w_ref[0].astype(wb_ref.dtype)

    acc = jnp.dot(a_ref[...], wb_ref[...], preferred_element_type=F32)
    o_ref[...] = x_ref[...] + gt_ref[...] * acc


def _out_projection_residual(a, w, layer, x, gt):
    s, k = a.shape
    n = w.shape[2]
    tm = min(ROW_TILE, s)
    tn = _largest_divisor(n, (1024, 512, 256, 128))
    return pl.pallas_call(
        _out_proj_kernel,
        out_shape=jax.ShapeDtypeStruct((s, n), F32),
        grid=(n // tn, s // tm),
        in_specs=[pl.BlockSpec((tm, k), lambda j, i: (i, 0)),
                  pl.BlockSpec((1, k, tn), lambda j, i: (layer, 0, j)),
                  pl.BlockSpec((tm, tn), lambda j, i: (i, j)),
                  pl.BlockSpec((1, tn), lambda j, i: (0, j))],
        out_specs=pl.BlockSpec((tm, tn), lambda j, i: (i, j)),
        scratch_shapes=[pltpu.VMEM((k, tn), MXU_DTYPE)],
        compiler_params=_cparams(("arbitrary", "arbitrary")),
        name="out_projection",
    )(a, w, x, gt)


def _lane_tiles(x):
    return [x[:, c * LANES:(c + 1) * LANES] for c in range(x.shape[1] // LANES)]


def _softmax_init(m_ref, acc_ref):
    m_ref[...] = jnp.full(m_ref.shape, BELOW_NEG_INF, F32)
    acc_ref[...] = jnp.zeros_like(acc_ref)


def _softmax_step(s, v_aug, m_ref, acc_ref):
    tiles = _lane_tiles(s)
    m_cur = jnp.max(functools.reduce(jnp.maximum, tiles), axis=-1, keepdims=True)
    m_old = m_ref[...]
    m_new = jnp.maximum(m_old, m_cur)
    p = jnp.concatenate([jnp.exp2(t - m_new) for t in tiles], axis=1).astype(MXU_DTYPE)
    pv = jnp.dot(p, v_aug, preferred_element_type=F32)
    alpha = jnp.exp2(m_old - m_new)
    acc_ref[...] = jnp.concatenate([alpha] * (acc_ref.shape[1] // LANES), axis=1) * acc_ref[...] + pv
    m_ref[...] = m_new


def _pipelined_tiles(n_plain, tail, qk, consume, s_ref, loop=True):
    s_ref[0] = qk(0)

    def step(j, cur, kind, prefetch=True):
        s = s_ref[cur]
        if prefetch:
            s_ref[1 - cur] = qk(j + 1)
        consume(j, s, kind)

    def pair(i, carry):
        step(2 * i, 0, None)
        step(2 * i + 1, 1, None)
        return carry

    def run_tail(cur):
        for k, kind in enumerate(tail):
            step(n_plain + k, cur, kind, prefetch=k + 1 < len(tail))
            cur = 1 - cur

    if not loop:
        run_tail(0)
        return
    lax.fori_loop(0, n_plain // 2, pair, 0)

    @pl.when(n_plain % 2 == 0)
    def _():
        run_tail(0)

    @pl.when(n_plain % 2 == 1)
    def _():
        step(n_plain - 1, 0, None)
        run_tail(1)


def _augment_values(v_ref, vaug_ref):
    hd = v_ref.shape[-1]
    for j in range(v_ref.shape[1]):
        vaug_ref[j, :, :hd] = v_ref[0, j]
        vaug_ref[j, :, hd:] = jnp.ones((v_ref.shape[2], hd), vaug_ref.dtype)


def _fox_kernel(q_ref, k_ref, v_ref, f_ref, o_ref, vaug_ref, s_ref, m_ref, acc_ref):
    i = pl.program_id(1)
    hd = q_ref.shape[2]

    @pl.when(i == 0)
    def _():
        _augment_values(v_ref, vaug_ref)

    q = q_ref[0]
    tq, tk = q.shape[0], k_ref.shape[2]
    diagonal_tiles = tq // tk

    def qk(j):
        return lax.dot_general(q, k_ref[0, j], (((1,), (1,)), ((), ())),
                               preferred_element_type=F32)

    def consume(j, s, kind):
        s = s - f_ref[0, j]
        if kind is not None:
            causal = (lax.broadcasted_iota(jnp.int32, (tq, tk), 1) + kind * tk
                      <= lax.broadcasted_iota(jnp.int32, (tq, tk), 0))
            s = jnp.where(causal, s, NEG_INF)
        _softmax_step(s, vaug_ref[j], m_ref, acc_ref)

    _softmax_init(m_ref, acc_ref)
    _pipelined_tiles(i * diagonal_tiles, list(range(diagonal_tiles)), qk, consume, s_ref)

    acc = acc_ref[...]
    o_ref[...] = (acc[:, :hd] / acc[:, hd:]).astype(o_ref.dtype)


def _fox_attention(qkv, f_rows, n_heads):
    _, s, hd = qkv.shape
    t = min(ATTN_TILE, s)
    tq = min(FOX_Q_TILE, s)
    assert tq % t == 0 and s % tq == 0
    nb = s // t
    kv = qkv.reshape(3 * n_heads, nb, t, hd)
    return pl.pallas_call(
        _fox_kernel,
        out_shape=jax.ShapeDtypeStruct((s, n_heads * hd), MXU_DTYPE),
        grid=(n_heads, s // tq),
        in_specs=[pl.BlockSpec((1, tq, hd), lambda h, i: (h, i, 0)),
                  pl.BlockSpec((1, nb, t, hd), lambda h, i: (n_heads + h, 0, 0, 0)),
                  pl.BlockSpec((1, nb, t, hd), lambda h, i: (2 * n_heads + h, 0, 0, 0)),
                  pl.BlockSpec((1, nb, 1, t), lambda h, i: (h, 0, 0, 0))],
        out_specs=pl.BlockSpec((tq, hd), lambda h, i: (i, h)),
        scratch_shapes=[pltpu.VMEM((nb, t, 2 * hd), MXU_DTYPE),
                        pltpu.VMEM((2, tq, t), F32),
                        pltpu.VMEM((tq, LANES), F32),
                        pltpu.VMEM((tq, 2 * hd), F32)],
        compiler_params=_cparams(("arbitrary", "arbitrary")),
        name="fox_attention",
    )(qkv, kv, kv, f_rows)


def _rel_bucket_np(dist):
    n = np.maximum(dist, 0)
    max_exact = REL_BUCKETS // 2
    nf = np.maximum(n, 1).astype(np.float32)
    large = max_exact + (np.log(nf / np.float32(max_exact))
                         / np.float32(math.log(REL_MAX_DIST / max_exact))
                         * np.float32(REL_BUCKETS - max_exact)).astype(np.int32)
    large = np.minimum(large, REL_BUCKETS - 1)
    return np.where(n < max_exact, n, large).astype(np.int32)


def _bias_table_kernel(rb_ref, idx_ref, o_ref, *, n_heads):
    h = pl.program_id(0)
    idx = idx_ref[0]
    bias = jnp.zeros(idx.shape, F32)
    far = rb_ref[(REL_BUCKETS - 1) * n_heads + h]
    for b in range(REL_BUCKETS):
        bias = jnp.where(idx == b, (rb_ref[b * n_heads + h] - far) * LOG2E, bias)
    o_ref[0, 0] = jnp.where(idx < 0, NEG_INF, bias)


def _moba_bias_tiles(rel_bias, tq, tk):
    n_heads = rel_bias.shape[1]
    n = 1 + tq // tk
    offset = tk - np.arange(n) * tk
    dist = offset[:, None, None] + np.arange(tq)[None, :, None] - np.arange(tk)[None, None, :]
    idx = jnp.asarray(np.where(dist < 0, -1, _rel_bucket_np(dist)).astype(np.int32))
    return pl.pallas_call(
        functools.partial(_bias_table_kernel, n_heads=n_heads),
        out_shape=jax.ShapeDtypeStruct((n_heads, n, tq, tk), F32),
        grid_spec=pltpu.PrefetchScalarGridSpec(
            num_scalar_prefetch=1, grid=(n_heads, n),
            in_specs=[pl.BlockSpec((1, tq, tk), lambda h, dt, rb: (dt, 0, 0))],
            out_specs=pl.BlockSpec((1, 1, tq, tk), lambda h, dt, rb: (h, dt, 0, 0))),
        compiler_params=_cparams(("arbitrary", "arbitrary")),
        name="moba_bias_tiles",
    )(rel_bias.reshape(-1), idx)


def _split3(x):
    hi = x.astype(MXU_DTYPE)
    r1 = x - hi.astype(F32)
    mid = r1.astype(MXU_DTYPE)
    lo = (r1 - mid.astype(F32)).astype(MXU_DTYPE)
    return hi, mid, lo


def _moba_kernel(q_ref, k_ref, v_ref, bias_ref, far_ref, o_ref,
                 vaug_ref, kaug_ref, qaug_ref, kmean_ref, selt_ref, s_ref, m_ref, acc_ref):
    t = pl.program_id(1)
    n_tiles, tk, hd = k_ref.shape[1], k_ref.shape[2], k_ref.shape[3]
    tq = q_ref.shape[1]
    own_tiles = tq // tk
    bpt = tk // MOBA_BLOCK
    n_blocks = n_tiles * bpt
    nbp = kmean_ref.shape[0]
    far_lane0 = LANES - 3

    @pl.when(t == 0)
    def _():
        _augment_values(v_ref, vaug_ref)
        kmean_ref[...] = jnp.zeros_like(kmean_ref)
        selt_ref[...] = jnp.zeros_like(selt_ref)
        lane = lax.broadcasted_iota(jnp.int32, (tk, LANES), 1)
        row_blk = lax.broadcasted_iota(jnp.int32, (tk, LANES), 0) // MOBA_BLOCK
        for j in range(n_tiles):
            kaug_ref[j, :, :hd] = k_ref[0, j]
            kaug_ref[j, :, hd:] = ((lane == j * bpt + row_blk) | (lane >= far_lane0)).astype(kaug_ref.dtype)
        for n in range(n_blocks):
            rows = slice((n % bpt) * MOBA_BLOCK, (n % bpt + 1) * MOBA_BLOCK)
            kmean_ref[n:n + 1, :] = jnp.mean(k_ref[0, n // bpt, rows, :].astype(F32), axis=0,
                                             keepdims=True)

    q = q_ref[0]

    gate = lax.dot_general(kmean_ref[...].astype(MXU_DTYPE), q, (((1,), (1,)), ((), ())),
                           preferred_element_type=F32)
    blk = lax.broadcasted_iota(jnp.int32, gate.shape, 0)
    own = t * (tq // MOBA_BLOCK) + lax.broadcasted_iota(jnp.int32, gate.shape, 1) // MOBA_BLOCK
    past = blk < own
    g = jnp.where(past, gate, NEG_INF)
    sel = jnp.where(blk == own, 0.0, NEG_INF)
    for _ in range(MOBA_TOPK):
        mx = jnp.max(g, axis=0, keepdims=True)
        pick = blk == jnp.min(jnp.where(g == mx, blk, nbp), axis=0, keepdims=True)
        sel = jnp.where(pick & past, 0.0, sel)
        g = jnp.where(pick, BELOW_NEG_INF, g)
    selt_ref[0:nbp, :] = sel
    far_hi, far_mid, far_lo = _split3(far_ref[0])
    lane = lax.broadcasted_iota(jnp.int32, (tq, LANES), 1)
    extra = selt_ref[...].T.astype(MXU_DTYPE)
    extra = jnp.where(lane == far_lane0, far_hi,
                      jnp.where(lane == far_lane0 + 1, far_mid,
                                jnp.where(lane == far_lane0 + 2, far_lo, extra)))
    qaug_ref[:, :hd] = q
    qaug_ref[:, hd:] = extra
    q_aug = qaug_ref[...]

    def qk(j):
        return lax.dot_general(q_aug, kaug_ref[j], (((1,), (1,)), ((), ())),
                               preferred_element_type=F32)

    def consume(j, s, kind):
        if kind is not None:
            s = s + bias_ref[0, kind]
        _softmax_step(s, vaug_ref[j], m_ref, acc_ref)

    _softmax_init(m_ref, acc_ref)
    own_kinds = list(range(1, 1 + own_tiles))

    @pl.when(t == 0)
    def _():
        _pipelined_tiles(0, own_kinds, qk, consume, s_ref, loop=False)

    @pl.when(t >= 1)
    def _():
        _pipelined_tiles(t * own_tiles - 1, [0] + own_kinds, qk, consume, s_ref)

    acc = acc_ref[...]
    o_ref[...] = (acc[:, :hd] / acc[:, hd:]).astype(o_ref.dtype)


def _moba_attention(qkv, bias_tiles, rel_bias, n_heads):
    _, s, hd = qkv.shape
    n_bias, tq, t = bias_tiles.shape[1:]
    assert s % tq == 0 and tq % t == 0 and t % MOBA_BLOCK == 0 and t >= REL_MAX_DIST and hd == LANES
    assert s // MOBA_BLOCK <= LANES - 3
    n_tiles = s // t
    nbp = -(-(s // MOBA_BLOCK) // SUBLANES) * SUBLANES
    kv = qkv.reshape(3 * n_heads, n_tiles, t, hd)
    far = jnp.broadcast_to(rel_bias[REL_BUCKETS - 1][:, None, None] * LOG2E, (n_heads, 1, LANES))
    return pl.pallas_call(
        _moba_kernel,
        out_shape=jax.ShapeDtypeStruct((s, n_heads * hd), MXU_DTYPE),
        grid=(n_heads, s // tq),
        in_specs=[pl.BlockSpec((1, tq, hd), lambda h, i: (h, i, 0)),
                  pl.BlockSpec((1, n_tiles, t, hd), lambda h, i: (n_heads + h, 0, 0, 0)),
                  pl.BlockSpec((1, n_tiles, t, hd), lambda h, i: (2 * n_heads + h, 0, 0, 0)),
                  pl.BlockSpec((1, n_bias, tq, t), lambda h, i: (h, 0, 0, 0)),
                  pl.BlockSpec((1, 1, LANES), lambda h, i: (h, 0, 0))],
        out_specs=pl.BlockSpec((tq, hd), lambda h, i: (i, h)),
        scratch_shapes=[pltpu.VMEM((n_tiles, t, 2 * hd), MXU_DTYPE),
                        pltpu.VMEM((n_tiles, t, 2 * hd), MXU_DTYPE),
                        pltpu.VMEM((tq, 2 * hd), MXU_DTYPE),
                        pltpu.VMEM((nbp, hd), F32),
                        pltpu.VMEM((LANES, tq), F32),
                        pltpu.VMEM((2, tq, t), F32),
                        pltpu.VMEM((tq, LANES), F32),
                        pltpu.VMEM((tq, 2 * hd), F32)],
        compiler_params=_cparams(("arbitrary", "arbitrary")),
        name="moba_attention",
    )(qkv, kv, kv, bias_tiles, far)


def _ffn_kernel(te_ref, tr_ref, x_ref, w1_ref, w3_ref, w2_ref, o_ref, w1b_ref, w3b_ref, w2b_ref):
    t = pl.program_id(0)
    f = pl.program_id(1)
    rows = tr_ref[t]
    tm = x_ref.shape[0]
    sub = min(FFN_SUB_ROWS, tm)

    def gate_up(r):
        x = x_ref[r, :]
        a = jnp.dot(x, w1b_ref[...], preferred_element_type=F32)
        b = jnp.dot(x, w3b_ref[...], preferred_element_type=F32)
        return (a * jax.nn.sigmoid(a) * b).astype(MXU_DTYPE)

    def down(g, r):
        o_ref[r, :] += jnp.dot(g, w2b_ref[...], preferred_element_type=F32)

    @pl.when(f == 0)
    def _():
        o_ref[...] = jnp.zeros_like(o_ref)

    n_chunks = (rows + sub - 1) // sub
    full = tm // sub
    for nc in range(1, full + 1):
        @pl.when(n_chunks == nc)
        def _(nc=nc):
            w1b_ref[...] = w1_ref[0].astype(w1b_ref.dtype)
            w3b_ref[...] = w3_ref[0].astype(w3b_ref.dtype)
            w2b_ref[...] = w2_ref[0].astype(w2b_ref.dtype)
            size = min(FFN_FULL_TILE_CHUNK, tm) if nc == full else sub
            for c in range(nc * sub // size):
                r = pl.ds(c * size, size)
                down(gate_up(r), r)


def _swiglu_ffn(xs, w1, w3, w2, tile_expert, tile_rows, tm):
    r, d = xs.shape
    ff = w1.shape[2]
    tf = _largest_divisor(ff, (FFN_F_TILE, 128))
    nf = ff // tf

    def f_eff(t, f, tr):
        return jnp.where(tr[t] > 0, f, nf - 1)

    return pl.pallas_call(
        _ffn_kernel,
        out_shape=jax.ShapeDtypeStruct((r, d), F32),
        grid_spec=pltpu.PrefetchScalarGridSpec(
            num_scalar_prefetch=2, grid=(r // tm, nf),
            in_specs=[pl.BlockSpec((tm, d), lambda t, f, te, tr: (t, 0), pipeline_mode=pl.Buffered(1)),
                      pl.BlockSpec((1, d, tf), lambda t, f, te, tr: (te[t], 0, f_eff(t, f, tr))),
                      pl.BlockSpec((1, d, tf), lambda t, f, te, tr: (te[t], 0, f_eff(t, f, tr))),
                      pl.BlockSpec((1, tf, d), lambda t, f, te, tr: (te[t], f_eff(t, f, tr), 0))],
            out_specs=pl.BlockSpec((tm, d), lambda t, f, te, tr: (t, 0)),
            scratch_shapes=[pltpu.VMEM((d, tf), MXU_DTYPE), pltpu.VMEM((d, tf), MXU_DTYPE),
                            pltpu.VMEM((tf, d), MXU_DTYPE)]),
        compiler_params=_cparams(("arbitrary", "arbitrary")),
        name="swiglu_ffn",
    )(tile_expert, tile_rows, xs, w1, w3, w2)


def _gather_kernel(tok_ref, nrows_ref, h_ref, o_ref, buf_ref, sem):
    t = pl.program_id(0)
    n_chunks = pl.num_programs(0)
    sub = o_ref.shape[0]
    slot = t % 2

    def request(chunk, into):
        def start(r, carry):
            tok = tok_ref[chunk * sub + r]
            pltpu.make_async_copy(h_ref.at[pl.ds(tok, 1)], buf_ref.at[into, pl.ds(r, 1)],
                                  sem.at[into]).start()
            return carry

        lax.fori_loop(0, sub, start, 0, unroll=DMA_ISSUE_UNROLL)

    @pl.when((t == 0) & (nrows_ref[0] > 0))
    def _():
        request(0, 0)

    nxt = jnp.minimum(t + 1, n_chunks - 1)

    @pl.when((t + 1 < n_chunks) & (nrows_ref[nxt] > 0))
    def _():
        request(nxt, 1 - slot)

    @pl.when(nrows_ref[t] > 0)
    def _():
        pltpu.make_async_copy(h_ref.at[pl.ds(0, sub)], buf_ref.at[slot], sem.at[slot]).wait()
        o_ref[...] = buf_ref[slot].astype(o_ref.dtype)

    @pl.when(nrows_ref[t] == 0)
    def _():
        o_ref[...] = jnp.zeros_like(o_ref)


def _gather_rows(h, row_token, sub_rows, sub):
    s, d = h.shape
    r = row_token.shape[0]
    return pl.pallas_call(
        _gather_kernel,
        out_shape=jax.ShapeDtypeStruct((r, d), MXU_DTYPE),
        grid_spec=pltpu.PrefetchScalarGridSpec(
            num_scalar_prefetch=2, grid=(r // sub,),
            in_specs=[pl.BlockSpec(memory_space=pl.ANY)],
            out_specs=pl.BlockSpec((sub, d), lambda t, tok, nr: (t, 0)),
            scratch_shapes=[pltpu.VMEM((2, sub, d), F32), pltpu.SemaphoreType.DMA((2,))]),
        compiler_params=_cparams(("arbitrary",)),
        name="moe_gather",
    )(row_token, sub_rows, h)


def _combine_kernel(pos_ref, x_ref, wts_ref, gt_ref, ys_ref, o_ref, buf_ref, sem):
    t = pl.program_id(0)
    n_tiles = pl.num_programs(0)
    tm = x_ref.shape[0]
    slot = t % 2

    def request(tile, into):
        def start(r, carry):
            for k in range(TOP_K):
                src = pos_ref[(tile * tm + r) * TOP_K + k]
                pltpu.make_async_copy(ys_ref.at[pl.ds(src, 1)], buf_ref.at[into, k, pl.ds(r, 1)],
                                      sem.at[into]).start()
            return carry

        lax.fori_loop(0, tm, start, 0, unroll=DMA_ISSUE_UNROLL)

    @pl.when(t == 0)
    def _():
        request(0, 0)

    @pl.when(t + 1 < n_tiles)
    def _():
        request(t + 1, 1 - slot)

    for k in range(TOP_K):
        pltpu.make_async_copy(ys_ref.at[pl.ds(0, tm)], buf_ref.at[slot, k], sem.at[slot]).wait()
    wts = wts_ref[...]
    y = wts[:, 0:1] * buf_ref[slot, 0]
    for k in range(1, TOP_K):
        y = y + wts[:, k:k + 1] * buf_ref[slot, k]
    o_ref[...] = x_ref[...] + gt_ref[...] * y


def _combine_residual(x, wts, gt, ys, pos):
    s, d = x.shape
    tm = min(COMBINE_TILE, s)
    return pl.pallas_call(
        _combine_kernel,
        out_shape=jax.ShapeDtypeStruct((s, d), F32),
        grid_spec=pltpu.PrefetchScalarGridSpec(
            num_scalar_prefetch=1, grid=(s // tm,),
            in_specs=[pl.BlockSpec((tm, d), lambda t, pos: (t, 0)),
                      pl.BlockSpec((tm, LANES), lambda t, pos: (t, 0)),
                      pl.BlockSpec((1, d), lambda t, pos: (0, 0)),
                      pl.BlockSpec(memory_space=pl.ANY)],
            out_specs=pl.BlockSpec((tm, d), lambda t, pos: (t, 0)),
            scratch_shapes=[pltpu.VMEM((2, TOP_K, tm, d), F32), pltpu.SemaphoreType.DMA((2,))]),
        compiler_params=_cparams(("arbitrary",)),
        name="moe_combine",
    )(pos, x, wts, gt, ys)


def _moe_plan(idx, n_experts, tm, sub):
    s = idx.shape[0]
    n_slots = s * TOP_K
    e_flat = idx.reshape(-1)
    onehot = (e_flat[:, None] == jnp.arange(n_experts, dtype=jnp.int32)[None, :]).astype(jnp.int32)
    cnt = jnp.sum(onehot, axis=0)
    rank = jnp.sum((jnp.cumsum(onehot, axis=0) - onehot) * onehot, axis=1)
    ntile_e = (cnt + tm - 1) // tm
    per_e = (cnt + ntile_e * sub - 1) // jnp.maximum(ntile_e * sub, 1) * sub
    per_e = jnp.maximum(per_e, sub)
    tile_end = jnp.cumsum(ntile_e)
    tile_start = tile_end - ntile_e
    per_slot = per_e[e_flat]
    pos = (tile_start[e_flat] + rank // per_slot) * tm + rank % per_slot
    n_tiles = n_slots // tm + n_experts
    row_token = jnp.zeros((n_tiles * tm,), jnp.int32).at[pos].set(
        jnp.arange(n_slots, dtype=jnp.int32) // TOP_K)
    t_ids = jnp.arange(n_tiles, dtype=jnp.int32)
    te = jnp.sum((t_ids[:, None] >= tile_end[None, :]).astype(jnp.int32), axis=1)
    used = te < n_experts
    te_c = jnp.minimum(te, n_experts - 1)
    rows = jnp.where(used, jnp.clip(cnt[te_c] - (t_ids - tile_start[te_c]) * per_e[te_c], 0, per_e[te_c]), 0)
    e_last = jnp.max(jnp.where(ntile_e > 0, jnp.arange(n_experts, dtype=jnp.int32), 0))
    tile_expert = jnp.where(used, te_c, e_last).astype(jnp.int32)
    per = tm // sub
    s_ids = jnp.arange(n_tiles * per, dtype=jnp.int32)
    sub_rows = jnp.clip(rows[s_ids // per] - (s_ids % per) * sub, 0, sub)
    return pos.astype(jnp.int32), row_token, tile_expert, rows.astype(jnp.int32), sub_rows.astype(jnp.int32)


def kernel(x, c, w_ada, b_ada, g_mix, g_ffn, g_final, rel_bias, w_qkv_fox, w_f_fox, b_f_fox, w_o_fox, w_qkv_moba, w_o_moba, w1_dense, w3_dense, w2_dense, w_router, w1_moe, w3_moe, w2_moe):
    b, s, d = x.shape
    assert b == 1 and d % HEAD_DIM == 0
    depth = w_ada.shape[0]
    n_heads = d // HEAD_DIM
    n_experts = w_router.shape[2]
    tm = min(ROW_TILE, s)
    sub = min(FFN_SUB_ROWS, tm)

    mod = _ada_modulation(c, w_ada, b_ada)
    bias_tiles = _moba_bias_tiles(rel_bias, min(MOBA_Q_TILE, s), min(ATTN_TILE, s))
    dense_rows = jnp.full((s // tm,), tm, jnp.int32)
    moe_w13_shape = (-1,) + w1_moe.shape[2:]
    moe_w2_shape = (-1,) + w2_moe.shape[2:]

    xc = x.reshape(s, d)
    pending = None
    for i in range(depth):
        j = i // 2
        sh1, sc1, gt1, sh2, sc2, gt2 = [mod[i, :, k * d:(k + 1) * d] for k in range(6)]
        g1 = g_mix[i].reshape(1, d)
        g2 = g_ffn[i].reshape(1, d)

        if i % 2 == 0:
            r = _norm_modulate(xc, g1, sc1, sh1, mode="fox", res=pending,
                               w_f=w_f_fox[j], b_f=b_f_fox[j])
        else:
            r = _norm_modulate(xc, g1, sc1, sh1, mode="attn", res=pending)
        xc = r.get("x", xc)
        pending = None
        if i % 2 == 0:
            qkv = _qkv_projection(r["h"], w_qkv_fox, j)
            nt = r["f"].shape[0]
            f_rows = r["f"].transpose(1, 0, 2).reshape(n_heads, nt, 1, r["f"].shape[2])
            o = _fox_attention(qkv, f_rows, n_heads)
            xc = _out_projection_residual(o, w_o_fox, j, xc, gt1)
        else:
            qkv = _qkv_projection(r["h"], w_qkv_moba, j)
            o = _moba_attention(qkv, bias_tiles, rel_bias, n_heads)
            xc = _out_projection_residual(o, w_o_moba, j, xc, gt1)

        if i % 2 == 0:
            r = _norm_modulate(xc, g2, sc2, sh2, mode="dense")
            y = _swiglu_ffn(r["h"], w1_dense, w3_dense, w2_dense,
                            jnp.full((s // tm,), j, jnp.int32), dense_rows, tm)
            pending = (y, gt2)
        else:
            r = _norm_modulate(xc, g2, sc2, sh2, mode="moe", w_router=w_router[j])
            pos, row_token, tile_expert, tile_rows, sub_rows = _moe_plan(
                r["idx"][:, :TOP_K], n_experts, tm, sub)
            xs = _gather_rows(r["h"], row_token, sub_rows, sub)
            ys = _swiglu_ffn(xs, w1_moe.reshape(moe_w13_shape), w3_moe.reshape(moe_w13_shape),
                             w2_moe.reshape(moe_w2_shape), tile_expert + j * n_experts, tile_rows, tm)
            xc = _combine_residual(xc, r["wts"], gt2, ys, pos)

    zero = jnp.zeros((1, d), F32)
    r = _norm_modulate(xc, g_final.reshape(1, d), zero, zero, mode="final", res=pending)
    return r["h"].reshape(b, s, d)
```
